```python
import math
import jax, jax.numpy as jnp
from jax import lax
import numpy as np

D_MODEL = 2048
BATCH = 2
SEQ = 4096
DEPTH = 1

CHUNK = 64
D_CONV = 1024
CONV_HEADS = 16
CONV_WIDTH = 3
D_SSM = 1024
SSM_GROUP = 16
SSM_GROUPS = D_SSM // SSM_GROUP
SSM_STATE = 64
D_MIX = D_CONV + D_SSM
N_IN = 4 * D_CONV + 2 * D_SSM
DT_MIN = 1e-3
DT_MAX = 1e-1
EPS = 1e-6

kernel_name = "hybrid_shortconv_s5_block"


def rms_norm(x, g):
    xf = x.astype(jnp.float32)
    y = xf * lax.rsqrt(jnp.mean(xf * xf, axis=-1, keepdims=True) + EPS)
    return (y * g.astype(jnp.float32)).astype(x.dtype)


def causal_dwconv(h, w, b):
    L = h.shape[1]
    hp = jnp.pad(h, ((0, 0), (CONV_WIDTH - 1, 0), (0, 0)))
    out = b[None, None, :]
    for k in range(CONV_WIDTH):
        out = out + w[k][None, None, :] * hp[:, k:k + L, :]
    return out


def _scan_combine(e1, e2):
    a1r, a1i, b1r, b1i = e1
    a2r, a2i, b2r, b2i = e2
    ar = a2r * a1r - a2i * a1i
    ai = a2r * a1i + a2i * a1r
    br = a2r * b1r - a2i * b1i + b2r
    bi = a2r * b1i + a2i * b1r + b2i
    return (ar, ai, br, bi)


def s5_branch(u, a_re, a_im, log_dt, b_re, b_im, c_re, c_im, d_skip, w_glu, b_glu):
    f32 = jnp.float32
    uf = u.astype(f32)
    Bsz, L, _ = uf.shape
    ug = uf.reshape(Bsz, L, SSM_GROUPS, SSM_GROUP)
    lr, li = a_re.astype(f32), a_im.astype(f32)
    dt = jnp.exp(log_dt.astype(f32))[:, None]
    mag = jnp.exp(lr * dt)
    lbr, lbi = mag * jnp.cos(li * dt), mag * jnp.sin(li * dt)
    nr, ni = lbr - 1.0, lbi
    den = lr * lr + li * li
    qr = (nr * lr + ni * li) / den
    qi = (ni * lr - nr * li) / den
    br_, bi_ = b_re.astype(f32), b_im.astype(f32)
    bbr = qr[..., None] * br_ - qi[..., None] * bi_
    bbi = qr[..., None] * bi_ + qi[..., None] * br_
    bu_r = jnp.einsum('blgh,gph->blgp', ug, bbr)
    bu_i = jnp.einsum('blgh,gph->blgp', ug, bbi)
    ar = jnp.broadcast_to(lbr[None, None], bu_r.shape)
    ai = jnp.broadcast_to(lbi[None, None], bu_i.shape)
    _, _, s_r, s_i = lax.associative_scan(_scan_combine, (ar, ai, bu_r, bu_i), axis=1)
    y = (jnp.einsum('ghp,blgp->blgh', c_re.astype(f32), s_r)
         - jnp.einsum('ghp,blgp->blgh', c_im.astype(f32), s_i))
    y = y.reshape(Bsz, L, D_SSM) + d_skip.astype(f32)[None, None, :] * uf
    y = jax.nn.gelu(y)
    y = y * jax.nn.sigmoid(y @ w_glu.astype(f32) + b_glu.astype(f32))
    return y.astype(u.dtype)


def setup_inputs(seed: int = 0) -> dict:
    key = jax.random.key(seed)
    ks = jax.random.split(key, 20)
    f32 = jnp.float32
    x = jax.random.normal(ks[0], (BATCH, SEQ, D_MODEL), f32)
    norm_pre_g = 1.0 + 0.05 * jax.random.normal(ks[1], (D_MODEL,), f32)
    w_in = jax.random.normal(ks[2], (D_MODEL, N_IN), f32) * D_MODEL ** -0.5
    conv_w = jax.random.normal(ks[3], (CONV_WIDTH, D_CONV), f32) * CONV_WIDTH ** -0.5
    conv_b = 0.02 * jax.random.normal(ks[4], (D_CONV,), f32)
    n = jnp.arange(SSM_STATE, dtype=f32)[None, :]
    ssm_a_re = -0.5 + 0.01 * jax.random.normal(ks[5], (SSM_GROUPS, SSM_STATE), f32)
    ssm_a_im = math.pi * n + 0.01 * jax.random.normal(ks[6], (SSM_GROUPS, SSM_STATE), f32)
    ssm_log_dt = jax.random.uniform(ks[7], (SSM_GROUPS,), f32, math.log(DT_MIN), math.log(DT_MAX))
    bs = (2.0 * SSM_GROUP) ** -0.5
    ssm_b_re = jax.random.normal(ks[8], (SSM_GROUPS, SSM_STATE, SSM_GROUP), f32) * bs
    ssm_b_im = jax.random.normal(ks[9], (SSM_GROUPS, SSM_STATE, SSM_GROUP), f32) * bs
    cs = (2.0 * SSM_STATE) ** -0.5
    ssm_c_re = jax.random.normal(ks[10], (SSM_GROUPS, SSM_GROUP, SSM_STATE), f32) * cs
    ssm_c_im = jax.random.normal(ks[11], (SSM_GROUPS, SSM_GROUP, SSM_STATE), f32) * cs
    ssm_d = jax.random.normal(ks[12], (D_SSM,), f32)
    w_glu = jax.random.normal(ks[13], (D_SSM, D_SSM), f32) * D_SSM ** -0.5
    b_glu = 0.02 * jax.random.normal(ks[14], (D_SSM,), f32)
    w_out = jax.random.normal(ks[15], (D_MIX, D_MODEL), f32) * D_MIX ** -0.5
    norm_post_g = 1.0 + 0.05 * jax.random.normal(ks[16], (D_MODEL,), f32)
    return {"x": x, "norm_pre_g": norm_pre_g, "w_in": w_in, "conv_w": conv_w, "conv_b": conv_b,
            "ssm_a_re": ssm_a_re, "ssm_a_im": ssm_a_im, "ssm_log_dt": ssm_log_dt,
            "ssm_b_re": ssm_b_re, "ssm_b_im": ssm_b_im, "ssm_c_re": ssm_c_re, "ssm_c_im": ssm_c_im,
            "ssm_d": ssm_d, "w_glu": w_glu, "b_glu": b_glu, "w_out": w_out, "norm_post_g": norm_post_g}


def reference(x, norm_pre_g, w_in, conv_w, conv_b, ssm_a_re, ssm_a_im, ssm_log_dt,
              ssm_b_re, ssm_b_im, ssm_c_re, ssm_c_im, ssm_d, w_glu, b_glu, w_out, norm_post_g):
    for _ in range(DEPTH):
        h = rms_norm(x, norm_pre_g)
        proj = h @ w_in
        b_gate, c_gate, v, z_conv = jnp.split(proj[..., :4 * D_CONV], 4, axis=-1)
        u, z_ssm = jnp.split(proj[..., 4 * D_CONV:], 2, axis=-1)
        y_conv = b_gate * causal_dwconv(c_gate * v, conv_w, conv_b)
        y_conv = y_conv * jax.nn.silu(z_conv)
        y_ssm = s5_branch(u, ssm_a_re, ssm_a_im, ssm_log_dt, ssm_b_re, ssm_b_im,
                          ssm_c_re, ssm_c_im, ssm_d, w_glu, b_glu)
        y_ssm = y_ssm * jax.nn.silu(z_ssm)
        mix = jnp.concatenate([y_conv, y_ssm], axis=-1)
        x = x + rms_norm(mix @ w_out, norm_post_g)
    return x
```

```python
import functools

import jax
import jax.numpy as jnp
from jax import lax
from jax.experimental import pallas as pl
from jax.experimental.pallas import tpu as pltpu

D_MODEL = 2048
D_CONV = 1024
D_SSM = 1024
SSM_GROUP = 16
SSM_GROUPS = 64
SSM_STATE = 64
N_IN = 4 * D_CONV + 2 * D_SSM
EPS = 1e-6

S5_CHUNK = 32
S5_TH = S5_CHUNK * SSM_GROUP
S5_POW_ROWS = S5_CHUNK + 8
TOKEN_TILE = 512
COL_BLOCK = 256
VMEM_LIMIT_BYTES = 56 * 1024 * 1024

_F32 = jnp.float32
_BF16 = jnp.bfloat16


def _dot(a, b):
    return jnp.dot(a, b, preferred_element_type=_F32)


def _dot_nt(a, b, precision=None):
    return lax.dot_general(a, b, (((1,), (1,)), ((), ())), precision=precision,
                           preferred_element_type=_F32)


def _sigmoid(z):
    return 1.0 / (1.0 + jnp.exp(-z))


def _inproj_kernel(x_ref, g_ref, w_ref, cw_ref, cb_ref, yc_ref, u_ref, zs_ref, carry_ref, *,
                   tiles_per_seq):
    tm = x_ref.shape[0]
    x = x_ref[...]
    ms = jnp.mean(x * x, axis=-1, keepdims=True)
    h = (x * lax.rsqrt(ms + EPS) * g_ref[...]).astype(_BF16)

    @pl.when(pl.program_id(0) % tiles_per_seq == 0)
    def _():
        carry_ref[...] = jnp.zeros_like(carry_ref)

    for cb in range(D_CONV // COL_BLOCK):
        lo = cb * COL_BLOCK
        sl = slice(lo, lo + COL_BLOCK)
        b_gate = _dot(h, w_ref[:, 0 * D_CONV + lo:0 * D_CONV + lo + COL_BLOCK])
        c_gate = _dot(h, w_ref[:, 1 * D_CONV + lo:1 * D_CONV + lo + COL_BLOCK])
        v = _dot(h, w_ref[:, 2 * D_CONV + lo:2 * D_CONV + lo + COL_BLOCK])
        z = _dot(h, w_ref[:, 3 * D_CONV + lo:3 * D_CONV + lo + COL_BLOCK])
        cv = c_gate * v
        ext = jnp.concatenate([carry_ref[:, sl], cv], axis=0)
        conv = (cb_ref[:, sl] + cw_ref[2:3, sl] * cv + cw_ref[1:2, sl] * ext[7:7 + tm]
                + cw_ref[0:1, sl] * ext[6:6 + tm])
        carry_ref[:, sl] = cv[tm - 8:]
        yc_ref[:, sl] = (b_gate * conv * (z * _sigmoid(z))).astype(_BF16)

    for sb in range(D_SSM // COL_BLOCK):
        lo = sb * COL_BLOCK
        sl = slice(lo, lo + COL_BLOCK)
        u_ref[:, sl] = _dot(h, w_ref[:, 4 * D_CONV + lo:4 * D_CONV + lo + COL_BLOCK]).astype(_BF16)
        z = _dot(h, w_ref[:, 4 * D_CONV + D_SSM + lo:4 * D_CONV + D_SSM + lo + COL_BLOCK])
        zs_ref[:, sl] = (z * _sigmoid(z)).astype(_BF16)


def _inproj(x2d, norm_g, w_in_bf16, conv_w, conv_b, seq_len):
    n_tok = x2d.shape[0]
    tm = TOKEN_TILE
    const = lambda i: (0, 0)
    out_sds = jax.ShapeDtypeStruct((n_tok, D_CONV), _BF16)
    return pl.pallas_call(
        functools.partial(_inproj_kernel, tiles_per_seq=seq_len // tm),
        grid=(n_tok // tm,),
        in_specs=[
            pl.BlockSpec((tm, D_MODEL), lambda i: (i, 0)),
            pl.BlockSpec((1, D_MODEL), const),
            pl.BlockSpec((D_MODEL, N_IN), const, pipeline_mode=pl.Buffered(1)),
            pl.BlockSpec((3, D_CONV), const),
            pl.BlockSpec((1, D_CONV), const),
        ],
        out_specs=[pl.BlockSpec((tm, D_CONV), lambda i: (i, 0))] * 3,
        out_shape=[out_sds] * 3,
        scratch_shapes=[pltpu.VMEM((8, D_CONV), _F32)],
        compiler_params=pltpu.CompilerParams(dimension_semantics=("arbitrary",),
                                             vmem_limit_bytes=VMEM_LIMIT_BYTES),
        name="inproj_conv",
    )(x2d, norm_g, w_in_bf16, conv_w, conv_b)


def _complex_powers(ar, ai, exponent, nbits):
    pr = jnp.ones((exponent.shape[0], ar.shape[1]), _F32)
    pi = jnp.zeros_like(pr)
    cr, ci = ar, ai
    for b in range(nbits):
        bit = ((exponent >> b) & 1) == 1
        fr = jnp.where(bit, cr, 1.0)
        fi = jnp.where(bit, ci, 0.0)
        pr, pi = pr * fr - pi * fi, pr * fi + pi * fr
        cr, ci = cr * cr - ci * ci, 2.0 * cr * ci
    return pr, pi


def _s5_prep_kernel(lr_ref, li_ref, ldt_ref, brt_ref, bit_ref, cr_ref, ci_ref,
                    kf_ref, wsr_ref, wsi_ref, wor_ref, woi_ref, atr_ref, ati_ref):
    t, th, jrows = S5_CHUNK, S5_TH, S5_POW_ROWS
    lr, li = lr_ref[0], li_ref[0]
    dt = jnp.exp(ldt_ref[0])
    mag = jnp.exp(lr * dt)
    ar, ai = mag * jnp.cos(li * dt), mag * jnp.sin(li * dt)
    nr, ni = ar - 1.0, ai
    den = lr * lr + li * li
    qr = (nr * lr + ni * li) / den
    qi = (ni * lr - nr * li) / den
    brt, bit = brt_ref[0], bit_ref[0]
    bbr = qr * brt - qi * bit
    bbi = qr * bit + qi * brt

    nbits = max(1, (jrows - 1).bit_length())
    j = lax.broadcasted_iota(jnp.int32, (jrows, 1), 0)
    pr, pi = _complex_powers(ar, ai, j, nbits)
    rr, ri = _complex_powers(ar, ai, jnp.maximum(t - 1 - j, 0), nbits)

    cr, ci = cr_ref[0], ci_ref[0]
    cpr = (pr[:, None, :] * cr[None] - pi[:, None, :] * ci[None]).reshape(jrows * SSM_GROUP, -1)
    cpi = (pr[:, None, :] * ci[None] + pi[:, None, :] * cr[None]).reshape(jrows * SSM_GROUP, -1)
    hi = lax.Precision.HIGHEST
    kf_ref[0] = _dot_nt(bbr, cpr[:th], hi) - _dot_nt(bbi, cpi[:th], hi)
    wor_ref[0] = cpr[SSM_GROUP:SSM_GROUP + th].astype(_BF16)
    woi_ref[0] = (-cpi[SSM_GROUP:SSM_GROUP + th]).astype(_BF16)
    rr, ri = rr[:t], ri[:t]
    wsr_ref[0] = (rr[:, None, :] * bbr[None] - ri[:, None, :] * bbi[None]).reshape(th, -1).astype(_BF16)
    wsi_ref[0] = (rr[:, None, :] * bbi[None] + ri[:, None, :] * bbr[None]).reshape(th, -1).astype(_BF16)
    atr_ref[0] = pr[t:t + 1]
    ati_ref[0] = pi[t:t + 1]


def _s5_prep(lr, li, ldt, brt, bit, cr, ci):
    g, p, h, th = SSM_GROUPS, SSM_STATE, SSM_GROUP, S5_TH
    row = pl.BlockSpec((1, 1, p), lambda i: (i, 0, 0))
    hp = pl.BlockSpec((1, h, p), lambda i: (i, 0, 0))
    thp = pl.BlockSpec((1, th, p), lambda i: (i, 0, 0))
    return pl.pallas_call(
        _s5_prep_kernel,
        grid=(g,),
        in_specs=[row, row, row, hp, hp, hp, hp],
        out_specs=[pl.BlockSpec((1, h, th), lambda i: (i, 0, 0)), thp, thp, thp, thp, row, row],
        out_shape=[jax.ShapeDtypeStruct((g, h, th), _F32)]
        + [jax.ShapeDtypeStruct((g, th, p), _BF16)] * 4
        + [jax.ShapeDtypeStruct((g, 1, p), _F32)] * 2,
        compiler_params=pltpu.CompilerParams(dimension_semantics=("arbitrary",)),
        name="s5_prep",
    )(lr, li, ldt, brt, bit, cr, ci)


def _s5_kernel(x_ref, kf_ref, wsr_ref, wsi_ref, wor_ref, woi_ref, atr_ref, ati_ref, y_ref, mt_ref, *,
               chunks_per_seq):
    t, th, h = S5_CHUNK, S5_TH, SSM_GROUP
    zk = jnp.concatenate([jnp.zeros((h, th), _F32), kf_ref[0]], axis=1)
    for k in range(t):
        mt_ref[k * h:(k + 1) * h, :] = zk[:, th - k * h:2 * th - k * h].astype(_BF16)

    xg = x_ref[0]
    y_intra = _dot(xg, mt_ref[...])
    sr = _dot(xg, wsr_ref[0])
    si = _dot(xg, wsi_ref[0])

    n = xg.shape[0]
    c = lax.broadcasted_iota(jnp.int32, (n, 1), 0) % chunks_per_seq
    ar, ai = atr_ref[0], ati_ref[0]
    d = 1
    while d < chunks_per_seq:
        pr = pltpu.roll(sr, d, axis=0)
        pi = pltpu.roll(si, d, axis=0)
        keep = c >= d
        sr, si = (sr + jnp.where(keep, ar * pr - ai * pi, 0.0),
                  si + jnp.where(keep, ar * pi + ai * pr, 0.0))
        ar, ai = ar * ar - ai * ai, 2.0 * ar * ai
        d *= 2
    first = c == 0
    in_r = jnp.where(first, 0.0, pltpu.roll(sr, 1, axis=0)).astype(_BF16)
    in_i = jnp.where(first, 0.0, pltpu.roll(si, 1, axis=0)).astype(_BF16)
    y_ref[0] = y_intra + _dot_nt(in_r, wor_ref[0]) + _dot_nt(in_i, woi_ref[0])


def _s5_core(xg, prep, chunks_per_seq):
    g, n, th = xg.shape
    p, h = SSM_STATE, SSM_GROUP
    kf, wsr, wsi, wor, woi, atr, ati = prep
    big = pl.BlockSpec((1, n, th), lambda i: (i, 0, 0))
    thp = pl.BlockSpec((1, th, p), lambda i: (i, 0, 0))
    row = pl.BlockSpec((1, 1, p), lambda i: (i, 0, 0))
    return pl.pallas_call(
        functools.partial(_s5_kernel, chunks_per_seq=chunks_per_seq),
        grid=(g,),
        in_specs=[big, pl.BlockSpec((1, h, th), lambda i: (i, 0, 0)), thp, thp, thp, thp, row, row],
        out_specs=big,
        out_shape=jax.ShapeDtypeStruct((g, n, th), _F32),
        scratch_shapes=[pltpu.VMEM((th, th), _BF16)],
        compiler_params=pltpu.CompilerParams(dimension_semantics=("arbitrary",)),
        name="s5_core",
    )(xg, kf, wsr, wsi, wor, woi, atr, ati)


def _out_kernel(yc_ref, yr_ref, u_ref, zs_ref, x_ref, d_ref, wg_ref, bg_ref, wo_ref, gp_ref, o_ref):
    y = yr_ref[...] + d_ref[...] * u_ref[...].astype(_F32)
    y = jax.nn.gelu(y)
    gate = _sigmoid(_dot(y.astype(_BF16), wg_ref[...]) + bg_ref[...])
    ys = (y * gate * zs_ref[...].astype(_F32)).astype(_BF16)
    o = _dot(yc_ref[...], wo_ref[:D_CONV]) + _dot(ys, wo_ref[D_CONV:])
    ms = jnp.mean(o * o, axis=-1, keepdims=True)
    o_ref[...] = x_ref[...] + o * lax.rsqrt(ms + EPS) * gp_ref[...]


def _out_proj(yc, yr, u, zs, x2d, d_skip, w_glu_bf16, b_glu, w_out_bf16, norm_g):
    n_tok = x2d.shape[0]
    tm = TOKEN_TILE
    const = lambda i: (0, 0)
    tile = lambda w: pl.BlockSpec((tm, w), lambda i: (i, 0))
    return pl.pallas_call(
        _out_kernel,
        grid=(n_tok // tm,),
        in_specs=[
            tile(D_CONV), tile(D_SSM), tile(D_SSM), tile(D_SSM), tile(D_MODEL),
            pl.BlockSpec((1, D_SSM), const),
            pl.BlockSpec((D_SSM, D_SSM), const, pipeline_mode=pl.Buffered(1)),
            pl.BlockSpec((1, D_SSM), const),
            pl.BlockSpec((D_CONV + D_SSM, D_MODEL), const, pipeline_mode=pl.Buffered(1)),
            pl.BlockSpec((1, D_MODEL), const),
        ],
        out_specs=tile(D_MODEL),
        out_shape=jax.ShapeDtypeStruct((n_tok, D_MODEL), _F32),
        compiler_params=pltpu.CompilerParams(dimension_semantics=("arbitrary",),
                                             vmem_limit_bytes=VMEM_LIMIT_BYTES),
        name="glu_outproj",
    )(yc, yr, u, zs, x2d, d_skip, w_glu_bf16, b_glu, w_out_bf16, norm_g)


def kernel(x, norm_pre_g, w_in, conv_w, conv_b, ssm_a_re, ssm_a_im, ssm_log_dt, ssm_b_re, ssm_b_im,
           ssm_c_re, ssm_c_im, ssm_d, w_glu, b_glu, w_out, norm_post_g):
    bsz, seq_len, _ = x.shape
    n_tok = bsz * seq_len
    g, p, h, t = SSM_GROUPS, SSM_STATE, SSM_GROUP, S5_CHUNK
    assert seq_len % TOKEN_TILE == 0 and seq_len % t == 0
    x2d = x.reshape(n_tok, D_MODEL)

    yc, u, zs = _inproj(x2d, norm_pre_g.reshape(1, -1), w_in.astype(_BF16), conv_w,
                        conv_b.reshape(1, -1), seq_len)

    prep = _s5_prep(
        ssm_a_re.reshape(g, 1, p), ssm_a_im.reshape(g, 1, p),
        jnp.broadcast_to(ssm_log_dt.reshape(g, 1, 1), (g, 1, p)),
        ssm_b_re.transpose(0, 2, 1), ssm_b_im.transpose(0, 2, 1), ssm_c_re, ssm_c_im)

    n_chunks = n_tok // t
    xg = u.reshape(n_chunks, t, g, h).transpose(2, 0, 1, 3).reshape(g, n_chunks, t * h)
    yg = _s5_core(xg, prep, seq_len // t)
    yr = yg.reshape(g, n_chunks, t, h).transpose(1, 2, 0, 3).reshape(n_tok, D_SSM)

    out = _out_proj(yc, yr, u, zs, x2d, ssm_d.reshape(1, -1), w_glu.astype(_BF16),
                    b_glu.reshape(1, -1), w_out.astype(_BF16), norm_post_g.reshape(1, -1))
    return out.reshape(bsz, seq_len, D_MODEL)
```

```python
import functools

import jax
import jax.numpy as jnp
from jax import lax
from jax.experimental import pallas as pl
from jax.experimental.pallas import tpu as pltpu

D_MODEL = 2048
D_CONV = 1024
D_SSM = 1024
SSM_GROUP = 16
SSM_GROUPS = 64
SSM_STATE = 64
N_IN = 4 * D_CONV + 2 * D_SSM
EPS = 1e-6

S5_CHUNK = 32
S5_TH = S5_CHUNK * SSM_GROUP
S5_POW_ROWS = S5_CHUNK + 8
LANES = 128
SUBLANES = 8
GROUPS_PER_BLOCK = LANES // SSM_GROUP
RELAYOUT_ROWS = 64
TOKEN_TILE = 512
COL_BLOCK = 256
VMEM_LIMIT_BYTES = 56 * 1024 * 1024

_F32 = jnp.float32
_BF16 = jnp.bfloat16


def _dot(a, b):
    return jnp.dot(a, b, preferred_element_type=_F32)


def _dot_nt(a, b, precision=None):
    return lax.dot_general(a, b, (((1,), (1,)), ((), ())), precision=precision,
                           preferred_element_type=_F32)


def _sigmoid(z):
    return 1.0 / (1.0 + jnp.exp(-z))


def _inproj_kernel(x_ref, g_ref, w_ref, cw_ref, cb_ref, yc_ref, u_ref, zs_ref, carry_ref, *,
                   tiles_per_seq):
    tm = x_ref.shape[0]
    x = x_ref[...]
    ms = jnp.mean(x * x, axis=-1, keepdims=True)
    h = (x * lax.rsqrt(ms + EPS) * g_ref[...]).astype(_BF16)

    @pl.when(pl.program_id(0) % tiles_per_seq == 0)
    def _():
        carry_ref[...] = jnp.zeros_like(carry_ref)

    for cb in range(D_CONV // COL_BLOCK):
        lo = cb * COL_BLOCK
        sl = slice(lo, lo + COL_BLOCK)
        b_gate = _dot(h, w_ref[:, 0 * D_CONV + lo:0 * D_CONV + lo + COL_BLOCK])
        c_gate = _dot(h, w_ref[:, 1 * D_CONV + lo:1 * D_CONV + lo + COL_BLOCK])
        v = _dot(h, w_ref[:, 2 * D_CONV + lo:2 * D_CONV + lo + COL_BLOCK])
        z = _dot(h, w_ref[:, 3 * D_CONV + lo:3 * D_CONV + lo + COL_BLOCK])
        cv = c_gate * v
        ext = jnp.concatenate([carry_ref[:, sl], cv], axis=0)
        conv = (cb_ref[:, sl] + cw_ref[2:3, sl] * cv + cw_ref[1:2, sl] * ext[7:7 + tm]
                + cw_ref[0:1, sl] * ext[6:6 + tm])
        carry_ref[:, sl] = cv[tm - 8:]
        yc_ref[:, sl] = (b_gate * conv * (z * _sigmoid(z))).astype(_BF16)

    for sb in range(D_SSM // COL_BLOCK):
        lo = sb * COL_BLOCK
        sl = slice(lo, lo + COL_BLOCK)
        u_ref[:, sl] = _dot(h, w_ref[:, 4 * D_CONV + lo:4 * D_CONV + lo + COL_BLOCK])
        z = _dot(h, w_ref[:, 4 * D_CONV + D_SSM + lo:4 * D_CONV + D_SSM + lo + COL_BLOCK])
        zs_ref[:, sl] = (z * _sigmoid(z)).astype(_BF16)


def _inproj(x2d, norm_g, w_in_bf16, conv_w, conv_b, seq_len):
    n_tok = x2d.shape[0]
    tm = TOKEN_TILE
    const = lambda i: (0, 0)
    out_sds = jax.ShapeDtypeStruct((n_tok, D_CONV), _BF16)
    return pl.pallas_call(
        functools.partial(_inproj_kernel, tiles_per_seq=seq_len // tm),
        grid=(n_tok // tm,),
        in_specs=[
            pl.BlockSpec((tm, D_MODEL), lambda i: (i, 0)),
            pl.BlockSpec((1, D_MODEL), const),
            pl.BlockSpec((D_MODEL, N_IN), const, pipeline_mode=pl.Buffered(1)),
            pl.BlockSpec((3, D_CONV), const),
            pl.BlockSpec((1, D_CONV), const),
        ],
        out_specs=[pl.BlockSpec((tm, D_CONV), lambda i: (i, 0))] * 3,
        out_shape=[out_sds, jax.ShapeDtypeStruct((n_tok, D_SSM), _F32), out_sds],
        scratch_shapes=[pltpu.VMEM((8, D_CONV), _F32)],
        compiler_params=pltpu.CompilerParams(dimension_semantics=("arbitrary",),
                                             vmem_limit_bytes=VMEM_LIMIT_BYTES),
        name="inproj_conv",
    )(x2d, norm_g, w_in_bf16, conv_w, conv_b)


def _complex_powers(ar, ai, exponent, nbits):
    pr = jnp.ones((exponent.shape[0], ar.shape[1]), _F32)
    pi = jnp.zeros_like(pr)
    cr, ci = ar, ai
    for b in range(nbits):
        bit = ((exponent >> b) & 1) == 1
        fr = jnp.where(bit, cr, 1.0)
        fi = jnp.where(bit, ci, 0.0)
        pr, pi = pr * fr - pi * fi, pr * fi + pi * fr
        cr, ci = cr * cr - ci * ci, 2.0 * cr * ci
    return pr, pi


def _s5_prep_kernel(lr_ref, li_ref, ldt_ref, brt_ref, bit_ref, cr_ref, ci_ref,
                    kf_ref, wsr_ref, wsi_ref, wor_ref, woi_ref, atr_ref, ati_ref):
    t, th, jrows = S5_CHUNK, S5_TH, S5_POW_ROWS
    lr, li = lr_ref[0], li_ref[0]
    dt = jnp.exp(ldt_ref[0])
    mag = jnp.exp(lr * dt)
    ar, ai = mag * jnp.cos(li * dt), mag * jnp.sin(li * dt)
    nr, ni = ar - 1.0, ai
    den = lr * lr + li * li
    qr = (nr * lr + ni * li) / den
    qi = (ni * lr - nr * li) / den
    brt, bit = brt_ref[0], bit_ref[0]
    bbr = qr * brt - qi * bit
    bbi = qr * bit + qi * brt

    nbits = max(1, (jrows - 1).bit_length())
    j = lax.broadcasted_iota(jnp.int32, (jrows, 1), 0)
    pr, pi = _complex_powers(ar, ai, j, nbits)
    rr, ri = _complex_powers(ar, ai, jnp.maximum(t - 1 - j, 0), nbits)

    cr, ci = cr_ref[0], ci_ref[0]
    cpr = (pr[:, None, :] * cr[None] - pi[:, None, :] * ci[None]).reshape(jrows * SSM_GROUP, -1)
    cpi = (pr[:, None, :] * ci[None] + pi[:, None, :] * cr[None]).reshape(jrows * SSM_GROUP, -1)
    hi = lax.Precision.HIGHEST
    kf_ref[0] = _dot_nt(bbr, cpr[:th], hi) - _dot_nt(bbi, cpi[:th], hi)
    wor_ref[0] = cpr[SSM_GROUP:SSM_GROUP + th].astype(_BF16)
    woi_ref[0] = (-cpi[SSM_GROUP:SSM_GROUP + th]).astype(_BF16)
    rr, ri = rr[:t], ri[:t]
    wsr_ref[0] = (rr[:, None, :] * bbr[None] - ri[:, None, :] * bbi[None]).reshape(th, -1).astype(_BF16)
    wsi_ref[0] = (rr[:, None, :] * bbi[None] + ri[:, None, :] * bbr[None]).reshape(th, -1).astype(_BF16)
    atr_ref[0] = pr[t:t + 1]
    ati_ref[0] = pi[t:t + 1]


def _s5_prep(lr, li, ldt, brt, bit, cr, ci):
    g, p, h, th = SSM_GROUPS, SSM_STATE, SSM_GROUP, S5_TH
    row = pl.BlockSpec((1, 1, p), lambda i: (i, 0, 0))
    hp = pl.BlockSpec((1, h, p), lambda i: (i, 0, 0))
    thp = pl.BlockSpec((1, th, p), lambda i: (i, 0, 0))
    return pl.pallas_call(
        _s5_prep_kernel,
        grid=(g,),
        in_specs=[row, row, row, hp, hp, hp, hp],
        out_specs=[pl.BlockSpec((1, h, th), lambda i: (i, 0, 0)), thp, thp, thp, thp, row, row],
        out_shape=[jax.ShapeDtypeStruct((g, h, th), _F32)]
        + [jax.ShapeDtypeStruct((g, th, p), _BF16)] * 4
        + [jax.ShapeDtypeStruct((g, 1, p), _F32)] * 2,
        compiler_params=pltpu.CompilerParams(dimension_semantics=("arbitrary",)),
        name="s5_prep",
    )(lr, li, ldt, brt, bit, cr, ci)


def _lane_block_transpose(xs):
    nb = len(xs)
    blk = lax.broadcasted_iota(jnp.int32, (1, LANES), 1) // SSM_GROUP
    masks = [blk == k for k in range(nb)]
    moved = []
    for d in range(nb):
        w = xs[d]
        for b in range(1, nb):
            w = jnp.where(masks[b], xs[(b + d) % nb], w)
        moved.append(w if d == 0 else pltpu.roll(w, d * SSM_GROUP, axis=1))
    ys = []
    for b in range(nb):
        y = moved[0]
        for d in range(1, nb):
            y = jnp.where(masks[(b + d) % nb], moved[d], y)
        ys.append(y)
    return ys


def _s5_group(xg, kf, wsr, wsi, wor, woi, atr, ati, mt_ref, chunks_per_seq):
    t, th, h = S5_CHUNK, S5_TH, SSM_GROUP
    zk = jnp.concatenate([jnp.zeros((h, th), _F32), kf], axis=1)
    for k in range(t):
        mt_ref[k * h:(k + 1) * h, :] = zk[:, th - k * h:2 * th - k * h].astype(_BF16)

    y_intra = _dot(xg, mt_ref[...])
    sr = _dot(xg, wsr)
    si = _dot(xg, wsi)

    n = xg.shape[0]
    c = lax.broadcasted_iota(jnp.int32, (n, 1), 0) % chunks_per_seq
    ar, ai = atr, ati
    d = 1
    while d < chunks_per_seq:
        pr = pltpu.roll(sr, d, axis=0)
        pi = pltpu.roll(si, d, axis=0)
        keep = c >= d
        sr, si = (sr + jnp.where(keep, ar * pr - ai * pi, 0.0),
                  si + jnp.where(keep, ar * pi + ai * pr, 0.0))
        ar, ai = ar * ar - ai * ai, 2.0 * ar * ai
        d *= 2
    first = c == 0
    in_r = jnp.where(first, 0.0, pltpu.roll(sr, 1, axis=0)).astype(_BF16)
    in_i = jnp.where(first, 0.0, pltpu.roll(si, 1, axis=0)).astype(_BF16)
    return y_intra + _dot_nt(in_r, wor) + _dot_nt(in_i, woi)


def _s5_kernel(u_ref, kf_ref, wsr_ref, wsi_ref, wor_ref, woi_ref, atr_ref, ati_ref, y_ref,
               xs_ref, ys_ref, mt_ref, *, chunks_per_seq):
    nb, sub = GROUPS_PER_BLOCK, SUBLANES
    n_rows = u_ref.shape[0] // sub
    t_hi_n = S5_CHUNK // sub
    n_chunks = n_rows // t_hi_n

    def relayout_in(i, carry):
        r0 = pl.multiple_of(i * RELAYOUT_ROWS, RELAYOUT_ROWS)
        a = [u_ref[pl.ds(r0 * sub + tl, RELAYOUT_ROWS, stride=sub), :] for tl in range(sub)]
        o = _lane_block_transpose(a)
        for g in range(nb):
            xs_ref[g, pl.ds(r0, RELAYOUT_ROWS), :] = o[g]
        return carry

    lax.fori_loop(0, n_rows // RELAYOUT_ROWS, relayout_in, 0)

    for g in range(nb):
        xg = jnp.concatenate([xs_ref[g, pl.ds(th, n_chunks, stride=t_hi_n), :]
                              for th in range(t_hi_n)], axis=1).astype(_BF16)
        y = _s5_group(xg, kf_ref[g], wsr_ref[g], wsi_ref[g], wor_ref[g], woi_ref[g],
                      atr_ref[g], ati_ref[g], mt_ref.at[g], chunks_per_seq)
        for th in range(t_hi_n):
            ys_ref[g, pl.ds(th, n_chunks, stride=t_hi_n), :] = y[:, th * LANES:(th + 1) * LANES]

    def relayout_out(i, carry):
        r0 = pl.multiple_of(i * RELAYOUT_ROWS, RELAYOUT_ROWS)
        o = [ys_ref[g, pl.ds(r0, RELAYOUT_ROWS), :] for g in range(nb)]
        a = _lane_block_transpose(o)
        for tl in range(sub):
            y_ref[pl.ds(r0 * sub + tl, RELAYOUT_ROWS, stride=sub), :] = a[tl]
        return carry

    lax.fori_loop(0, n_rows // RELAYOUT_ROWS, relayout_out, 0)


def _s5_core(u, prep, chunks_per_seq):
    n_tok = u.shape[0]
    p, h, th, nb = SSM_STATE, SSM_GROUP, S5_TH, GROUPS_PER_BLOCK
    kf, wsr, wsi, wor, woi, atr, ati = prep
    tok = pl.BlockSpec((n_tok, LANES), lambda i: (0, i))
    thp = pl.BlockSpec((nb, th, p), lambda i: (i, 0, 0))
    row = pl.BlockSpec((nb, 1, p), lambda i: (i, 0, 0))
    return pl.pallas_call(
        functools.partial(_s5_kernel, chunks_per_seq=chunks_per_seq),
        grid=(SSM_GROUPS // nb,),
        in_specs=[tok, pl.BlockSpec((nb, h, th), lambda i: (i, 0, 0)), thp, thp, thp, thp, row, row],
        out_specs=tok,
        out_shape=jax.ShapeDtypeStruct((n_tok, D_SSM), _F32),
        scratch_shapes=[pltpu.VMEM((nb, n_tok // SUBLANES, LANES), _F32),
                        pltpu.VMEM((nb, n_tok // SUBLANES, LANES), _F32),
                        pltpu.VMEM((nb, th, th), _BF16)],
        compiler_params=pltpu.CompilerParams(dimension_semantics=("arbitrary",),
                                             vmem_limit_bytes=VMEM_LIMIT_BYTES),
        name="s5_core",
    )(u, kf, wsr, wsi, wor, woi, atr, ati)


def _out_kernel(yc_ref, yr_ref, u_ref, zs_ref, x_ref, d_ref, wg_ref, bg_ref, wo_ref, gp_ref, o_ref):
    y = yr_ref[...] + d_ref[...] * u_ref[...]
    y = jax.nn.gelu(y)
    gate = _sigmoid(_dot(y.astype(_BF16), wg_ref[...]) + bg_ref[...])
    ys = (y * gate * zs_ref[...].astype(_F32)).astype(_BF16)
    o = _dot(yc_ref[...], wo_ref[:D_CONV]) + _dot(ys, wo_ref[D_CONV:])
    ms = jnp.mean(o * o, axis=-1, keepdims=True)
    o_ref[...] = x_ref[...] + o * lax.rsqrt(ms + EPS) * gp_ref[...]


def _out_proj(yc, yr, u, zs, x2d, d_skip, w_glu_bf16, b_glu, w_out_bf16, norm_g):
    n_tok = x2d.shape[0]
    tm = TOKEN_TILE
    const = lambda i: (0, 0)
    tile = lambda w: pl.BlockSpec((tm, w), lambda i: (i, 0))
    return pl.pallas_call(
        _out_kernel,
        grid=(n_tok // tm,),
        in_specs=[
            tile(D_CONV), tile(D_SSM), tile(D_SSM), tile(D_SSM), tile(D_MODEL),
            pl.BlockSpec((1, D_SSM), const),
            pl.BlockSpec((D_SSM, D_SSM), const, pipeline_mode=pl.Buffered(1)),
            pl.BlockSpec((1, D_SSM), const),
            pl.BlockSpec((D_CONV + D_SSM, D_MODEL), const, pipeline_mode=pl.Buffered(1)),
            pl.BlockSpec((1, D_MODEL), const),
        ],
        out_specs=tile(D_MODEL),
        out_shape=jax.ShapeDtypeStruct((n_tok, D_MODEL), _F32),
        compiler_params=pltpu.CompilerParams(dimension_semantics=("arbitrary",),
                                             vmem_limit_bytes=VMEM_LIMIT_BYTES),
        name="glu_outproj",
    )(yc, yr, u, zs, x2d, d_skip, w_glu_bf16, b_glu, w_out_bf16, norm_g)


def kernel(x, norm_pre_g, w_in, conv_w, conv_b, ssm_a_re, ssm_a_im, ssm_log_dt, ssm_b_re, ssm_b_im,
           ssm_c_re, ssm_c_im, ssm_d, w_glu, b_glu, w_out, norm_post_g):
    bsz, seq_len, _ = x.shape
    n_tok = bsz * seq_len
    g, p, h, t = SSM_GROUPS, SSM_STATE, SSM_GROUP, S5_CHUNK
    assert seq_len % TOKEN_TILE == 0 and seq_len % t == 0
    x2d = x.reshape(n_tok, D_MODEL)

    yc, u, zs = _inproj(x2d, norm_pre_g.reshape(1, -1), w_in.astype(_BF16), conv_w,
                        conv_b.reshape(1, -1), seq_len)

    prep = _s5_prep(
        ssm_a_re.reshape(g, 1, p), ssm_a_im.reshape(g, 1, p),
        jnp.broadcast_to(ssm_log_dt.reshape(g, 1, 1), (g, 1, p)),
        ssm_b_re.transpose(0, 2, 1), ssm_b_im.transpose(0, 2, 1), ssm_c_re, ssm_c_im)

    yr = _s5_core(u, prep, seq_len // t)

    out = _out_proj(yc, yr, u, zs, x2d, ssm_d.reshape(1, -1), w_glu.astype(_BF16),
                    b_glu.reshape(1, -1), w_out.astype(_BF16), norm_post_g.reshape(1, -1))
    return out.reshape(bsz, seq_len, D_MODEL)
```

```python
import functools

import jax
import jax.numpy as jnp
from jax import lax
from jax.experimental import pallas as pl
from jax.experimental.pallas import tpu as pltpu

D_MODEL = 2048
D_CONV = 1024
D_SSM = 1024
SSM_GROUP = 16
SSM_GROUPS = 64
SSM_STATE = 64
N_IN = 4 * D_CONV + 2 * D_SSM
EPS = 1e-6

S5_CHUNK = 32
S5_TH = S5_CHUNK * SSM_GROUP
S5_POW_ROWS = S5_CHUNK + 8
LANES = 128
SUBLANES = 8
GROUPS_PER_BLOCK = LANES // SSM_GROUP
TOKEN_TILE = 512
COL_BLOCK = 256
VMEM_LIMIT_BYTES = 56 * 1024 * 1024

_F32 = jnp.float32
_BF16 = jnp.bfloat16


def _dot(a, b):
    return jnp.dot(a, b, preferred_element_type=_F32)


def _dot_nt(a, b, precision=None):
    return lax.dot_general(a, b, (((1,), (1,)), ((), ())), precision=precision,
                           preferred_element_type=_F32)


def _sigmoid(z):
    return 1.0 / (1.0 + jnp.exp(-z))


def _lane_block_transpose(xs):
    nb = len(xs)
    blk = lax.broadcasted_iota(jnp.int32, (1, LANES), 1) // SSM_GROUP
    masks = [blk == k for k in range(nb)]
    moved = []
    for d in range(nb):
        w = xs[d]
        for b in range(1, nb):
            w = jnp.where(masks[b], xs[(b + d) % nb], w)
        moved.append(w if d == 0 else pltpu.roll(w, d * SSM_GROUP, axis=1))
    ys = []
    for b in range(nb):
        y = moved[0]
        for d in range(1, nb):
            y = jnp.where(masks[(b + d) % nb], moved[d], y)
        ys.append(y)
    return ys


def _inproj_kernel(x_ref, g_ref, w_ref, cw_ref, cb_ref, yc_ref, ug_ref, zs_ref, carry_ref, slab_ref, *,
                   tiles_per_seq):
    tm = x_ref.shape[0]
    nb, sub = GROUPS_PER_BLOCK, SUBLANES
    x = x_ref[...]
    ms = jnp.mean(x * x, axis=-1, keepdims=True)
    h = (x * lax.rsqrt(ms + EPS) * g_ref[...]).astype(_BF16)

    @pl.when(pl.program_id(0) % tiles_per_seq == 0)
    def _():
        carry_ref[...] = jnp.zeros_like(carry_ref)

    for cb in range(D_CONV // COL_BLOCK):
        lo = cb * COL_BLOCK
        sl = slice(lo, lo + COL_BLOCK)
        b_gate = _dot(h, w_ref[:, 0 * D_CONV + lo:0 * D_CONV + lo + COL_BLOCK])
        c_gate = _dot(h, w_ref[:, 1 * D_CONV + lo:1 * D_CONV + lo + COL_BLOCK])
        v = _dot(h, w_ref[:, 2 * D_CONV + lo:2 * D_CONV + lo + COL_BLOCK])
        z = _dot(h, w_ref[:, 3 * D_CONV + lo:3 * D_CONV + lo + COL_BLOCK])
        cv = c_gate * v
        ext = jnp.concatenate([carry_ref[:, sl], cv], axis=0)
        conv = (cb_ref[:, sl] + cw_ref[2:3, sl] * cv + cw_ref[1:2, sl] * ext[7:7 + tm]
                + cw_ref[0:1, sl] * ext[6:6 + tm])
        carry_ref[:, sl] = cv[tm - 8:]
        yc_ref[:, sl] = (b_gate * conv * (z * _sigmoid(z))).astype(_BF16)

    for sb in range(D_SSM // COL_BLOCK):
        lo = sb * COL_BLOCK
        sl = slice(lo, lo + COL_BLOCK)
        u = _dot(h, w_ref[:, 4 * D_CONV + lo:4 * D_CONV + lo + COL_BLOCK])
        z = _dot(h, w_ref[:, 4 * D_CONV + D_SSM + lo:4 * D_CONV + D_SSM + lo + COL_BLOCK])
        zs_ref[:, sl] = (z * _sigmoid(z)).astype(_BF16)
        for k in range(COL_BLOCK // LANES):
            gv = sb * (COL_BLOCK // LANES) + k
            slab_ref[gv] = u[:, k * LANES:(k + 1) * LANES]
            rows = [slab_ref[gv, pl.ds(tl, tm // sub, stride=sub), :] for tl in range(sub)]
            for gl, o in enumerate(_lane_block_transpose(rows)):
                ug_ref[gv * nb + gl] = o


def _inproj(x2d, norm_g, w_in_bf16, conv_w, conv_b, seq_len):
    n_tok = x2d.shape[0]
    tm = TOKEN_TILE
    const = lambda i: (0, 0)
    out_sds = jax.ShapeDtypeStruct((n_tok, D_CONV), _BF16)
    return pl.pallas_call(
        functools.partial(_inproj_kernel, tiles_per_seq=seq_len // tm),
        grid=(n_tok // tm,),
        in_specs=[
            pl.BlockSpec((tm, D_MODEL), lambda i: (i, 0)),
            pl.BlockSpec((1, D_MODEL), const),
            pl.BlockSpec((D_MODEL, N_IN), const, pipeline_mode=pl.Buffered(1)),
            pl.BlockSpec((3, D_CONV), const),
            pl.BlockSpec((1, D_CONV), const),
        ],
        out_specs=[pl.BlockSpec((tm, D_CONV), lambda i: (i, 0)),
                   pl.BlockSpec((SSM_GROUPS, tm // SUBLANES, LANES), lambda i: (0, i, 0)),
                   pl.BlockSpec((tm, D_SSM), lambda i: (i, 0))],
        out_shape=[out_sds,
                   jax.ShapeDtypeStruct((SSM_GROUPS, n_tok // SUBLANES, LANES), _F32),
                   out_sds],
        scratch_shapes=[pltpu.VMEM((8, D_CONV), _F32),
                        pltpu.VMEM((D_SSM // LANES, tm, LANES), _F32)],
        compiler_params=pltpu.CompilerParams(dimension_semantics=("arbitrary",),
                                             vmem_limit_bytes=VMEM_LIMIT_BYTES),
        name="inproj_conv",
    )(x2d, norm_g, w_in_bf16, conv_w, conv_b)


def _complex_powers(ar, ai, exponent, nbits):
    pr = jnp.ones((exponent.shape[0], ar.shape[1]), _F32)
    pi = jnp.zeros_like(pr)
    cr, ci = ar, ai
    for b in range(nbits):
        bit = ((exponent >> b) & 1) == 1
        fr = jnp.where(bit, cr, 1.0)
        fi = jnp.where(bit, ci, 0.0)
        pr, pi = pr * fr - pi * fi, pr * fi + pi * fr
        cr, ci = cr * cr - ci * ci, 2.0 * cr * ci
    return pr, pi


def _re_im_sign():
    lane = lax.broadcasted_iota(jnp.int32, (1, LANES), 1)
    return jnp.where(lane < SSM_STATE, -1.0, 1.0).astype(_F32)


def _s5_prep_kernel(lr_ref, li_ref, ldt_ref, bt1_ref, bt2_ref, ct1_ref, ct2_ref, d_ref,
                    kf_ref, ws_ref, wo_ref, a1_ref, a2_ref):
    t, th, jrows, h = S5_CHUNK, S5_TH, S5_POW_ROWS, SSM_GROUP
    sgn = _re_im_sign()
    nbits = max(1, (jrows - 1).bit_length())
    j = lax.broadcasted_iota(jnp.int32, (jrows, 1), 0)
    diag = (lax.broadcasted_iota(jnp.int32, (h, th), 0) == lax.broadcasted_iota(jnp.int32, (h, th), 1))
    hi = lax.Precision.HIGHEST
    for g in range(lr_ref.shape[0]):
        lr, li = lr_ref[g], li_ref[g]
        dt = jnp.exp(ldt_ref[g])
        mag = jnp.exp(lr * dt)
        ar, ai = mag * jnp.cos(li * dt), mag * jnp.sin(li * dt)
        nr, ni = ar - 1.0, ai
        den = lr * lr + li * li
        qr = (nr * lr + ni * li) / den
        qi = (ni * lr - nr * li) / den
        bt1, bt2 = bt1_ref[g], bt2_ref[g]
        b1 = qr * bt1 + (qi * sgn) * bt2
        b2 = (qr * sgn) * bt2 - qi * bt1
        c1 = ct1_ref[g] * (-sgn)
        c2 = -ct2_ref[g]
        pr, pi = _complex_powers(ar, ai, j, nbits)
        rr, ri = _complex_powers(ar, ai, jnp.maximum(t - 1 - j, 0), nbits)
        cps = (pr[:, None, :] * c1[None] + pi[:, None, :] * c2[None]).reshape(jrows * h, LANES)
        kf_ref[g] = _dot_nt(b1, cps[:th], hi) + jnp.where(diag, d_ref[g], 0.0)
        wo_ref[g] = cps[h:h + th].astype(_BF16)
        ws_ref[g] = (rr[:t, None, :] * b1[None] + ri[:t, None, :] * b2[None]).reshape(th, LANES).astype(_BF16)
        a1_ref[g] = pr[t:t + 1]
        a2_ref[g] = pi[t:t + 1]


def _s5_prep(lr, li, ldt, bt1, bt2, ct1, ct2, d_pad):
    g, h, th, nb = SSM_GROUPS, SSM_GROUP, S5_TH, GROUPS_PER_BLOCK
    row = pl.BlockSpec((nb, 1, LANES), lambda i: (i, 0, 0))
    hp = pl.BlockSpec((nb, h, LANES), lambda i: (i, 0, 0))
    thp = pl.BlockSpec((nb, th, LANES), lambda i: (i, 0, 0))
    return pl.pallas_call(
        _s5_prep_kernel,
        grid=(g // nb,),
        in_specs=[row, row, row, hp, hp, hp, hp, pl.BlockSpec((nb, 1, th), lambda i: (i, 0, 0))],
        out_specs=[pl.BlockSpec((nb, h, th), lambda i: (i, 0, 0)), thp, thp, row, row],
        out_shape=[jax.ShapeDtypeStruct((g, h, th), _F32)]
        + [jax.ShapeDtypeStruct((g, th, LANES), _BF16)] * 2
        + [jax.ShapeDtypeStruct((g, 1, LANES), _F32)] * 2,
        compiler_params=pltpu.CompilerParams(dimension_semantics=("arbitrary",)),
        name="s5_prep",
    )(lr, li, ldt, bt1, bt2, ct1, ct2, d_pad)


def _s5_group(xg, kf, ws, wo, a1, a2, mt_ref, chunks_per_seq):
    t, th, h = S5_CHUNK, S5_TH, SSM_GROUP
    zk = jnp.concatenate([jnp.zeros((h, th), _F32), kf], axis=1)
    for k in range(t):
        mt_ref[k * h:(k + 1) * h, :] = zk[:, th - k * h:2 * th - k * h].astype(_BF16)

    y_intra = _dot(xg, mt_ref[...])
    s = _dot(xg, ws)

    n = xg.shape[0]
    c = lax.broadcasted_iota(jnp.int32, (n, 1), 0) % chunks_per_seq
    sgn = _re_im_sign()
    ar, ai = a1, a2
    d = 1
    while d < chunks_per_seq:
        prev = pltpu.roll(s, d, axis=0)
        swapped = pltpu.roll(prev, SSM_STATE, axis=1)
        s = s + jnp.where(c >= d, ar * prev + (ai * sgn) * swapped, 0.0)
        ar, ai = ar * ar - ai * ai, 2.0 * ar * ai
        d *= 2
    s_in = jnp.where(c == 0, 0.0, pltpu.roll(s, 1, axis=0)).astype(_BF16)
    return y_intra + _dot_nt(s_in, wo)


def _s5_kernel(ug_ref, kf_ref, ws_ref, wo_ref, a1_ref, a2_ref, yg_ref, mt_ref, *, chunks_per_seq):
    t_hi_n = S5_CHUNK // SUBLANES
    n_chunks = ug_ref.shape[1] // t_hi_n
    for g in range(ug_ref.shape[0]):
        xg = jnp.concatenate([ug_ref[g, pl.ds(th, n_chunks, stride=t_hi_n), :]
                              for th in range(t_hi_n)], axis=1).astype(_BF16)
        y = _s5_group(xg, kf_ref[g], ws_ref[g], wo_ref[g], a1_ref[g], a2_ref[g], mt_ref.at[g],
                      chunks_per_seq)
        for th in range(t_hi_n):
            yg_ref[g, pl.ds(th, n_chunks, stride=t_hi_n), :] = y[:, th * LANES:(th + 1) * LANES]


def _s5_core(ug, prep, chunks_per_seq):
    _, n_rows, _ = ug.shape
    h, th, nb = SSM_GROUP, S5_TH, GROUPS_PER_BLOCK
    kf, ws, wo, a1, a2 = prep
    rows = pl.BlockSpec((nb, n_rows, LANES), lambda i: (i, 0, 0))
    thp = pl.BlockSpec((nb, th, LANES), lambda i: (i, 0, 0))
    row = pl.BlockSpec((nb, 1, LANES), lambda i: (i, 0, 0))
    return pl.pallas_call(
        functools.partial(_s5_kernel, chunks_per_seq=chunks_per_seq),
        grid=(SSM_GROUPS // nb,),
        in_specs=[rows, pl.BlockSpec((nb, h, th), lambda i: (i, 0, 0)), thp, thp, row, row],
        out_specs=rows,
        out_shape=jax.ShapeDtypeStruct(ug.shape, _F32),
        scratch_shapes=[pltpu.VMEM((nb, th, th), _BF16)],
        compiler_params=pltpu.CompilerParams(dimension_semantics=("arbitrary",),
                                             vmem_limit_bytes=VMEM_LIMIT_BYTES),
        name="s5_core",
    )(ug, kf, ws, wo, a1, a2)


def _out_kernel(yc_ref, yg_ref, zs_ref, x_ref, wg_ref, bg_ref, wo_ref, gp_ref, o_ref, slab_ref):
    tm = x_ref.shape[0]
    nb, sub = GROUPS_PER_BLOCK, SUBLANES
    for gv in range(D_SSM // LANES):
        rows = _lane_block_transpose([yg_ref[gv * nb + gl] for gl in range(nb)])
        for tl in range(sub):
            slab_ref[gv, pl.ds(tl, tm // sub, stride=sub), :] = rows[tl]
    y = jnp.concatenate([slab_ref[gv] for gv in range(D_SSM // LANES)], axis=1)
    y = jax.nn.gelu(y)
    gate = _sigmoid(_dot(y.astype(_BF16), wg_ref[...]) + bg_ref[...])
    ys = (y * gate * zs_ref[...].astype(_F32)).astype(_BF16)
    o = _dot(yc_ref[...], wo_ref[:D_CONV]) + _dot(ys, wo_ref[D_CONV:])
    ms = jnp.mean(o * o, axis=-1, keepdims=True)
    o_ref[...] = x_ref[...] + o * lax.rsqrt(ms + EPS) * gp_ref[...]


def _out_proj(yc, yg, zs, x2d, w_glu_bf16, b_glu, w_out_bf16, norm_g):
    n_tok = x2d.shape[0]
    tm = TOKEN_TILE
    const = lambda i: (0, 0)
    tile = lambda w: pl.BlockSpec((tm, w), lambda i: (i, 0))
    return pl.pallas_call(
        _out_kernel,
        grid=(n_tok // tm,),
        in_specs=[
            tile(D_CONV),
            pl.BlockSpec((SSM_GROUPS, tm // SUBLANES, LANES), lambda i: (0, i, 0)),
            tile(D_SSM), tile(D_MODEL),
            pl.BlockSpec((D_SSM, D_SSM), const, pipeline_mode=pl.Buffered(1)),
            pl.BlockSpec((1, D_SSM), const),
            pl.BlockSpec((D_CONV + D_SSM, D_MODEL), const, pipeline_mode=pl.Buffered(1)),
            pl.BlockSpec((1, D_MODEL), const),
        ],
        out_specs=tile(D_MODEL),
        out_shape=jax.ShapeDtypeStruct((n_tok, D_MODEL), _F32),
        scratch_shapes=[pltpu.VMEM((D_SSM // LANES, tm, LANES), _F32)],
        compiler_params=pltpu.CompilerParams(dimension_semantics=("arbitrary",),
                                             vmem_limit_bytes=VMEM_LIMIT_BYTES),
        name="glu_outproj",
    )(yc, yg, zs, x2d, w_glu_bf16, b_glu, w_out_bf16, norm_g)


def kernel(x, norm_pre_g, w_in, conv_w, conv_b, ssm_a_re, ssm_a_im, ssm_log_dt, ssm_b_re, ssm_b_im,
           ssm_c_re, ssm_c_im, ssm_d, w_glu, b_glu, w_out, norm_post_g):
    bsz, seq_len, _ = x.shape
    n_tok = bsz * seq_len
    g, p, h, t = SSM_GROUPS, SSM_STATE, SSM_GROUP, S5_CHUNK
    assert seq_len % TOKEN_TILE == 0 and TOKEN_TILE % t == 0
    x2d = x.reshape(n_tok, D_MODEL)

    yc, ug, zs = _inproj(x2d, norm_pre_g.reshape(1, -1), w_in.astype(_BF16), conv_w,
                         conv_b.reshape(1, -1), seq_len)

    both = lambda a, b: jnp.concatenate([a, b], axis=-1)
    brt, bit = ssm_b_re.transpose(0, 2, 1), ssm_b_im.transpose(0, 2, 1)
    a_re, a_im = ssm_a_re.reshape(g, 1, p), ssm_a_im.reshape(g, 1, p)
    d_pad = jnp.pad(ssm_d.reshape(g, 1, h), ((0, 0), (0, 0), (0, S5_TH - h)))
    prep = _s5_prep(
        both(a_re, a_re), both(a_im, a_im),
        jnp.broadcast_to(ssm_log_dt.reshape(g, 1, 1), (g, 1, LANES)),
        both(brt, bit), both(bit, brt), both(ssm_c_re, ssm_c_im), both(ssm_c_im, ssm_c_re), d_pad)

    yg = _s5_core(ug, prep, seq_len // t)

    out = _out_proj(yc, yg, zs, x2d, w_glu.astype(_BF16), b_glu.reshape(1, -1),
                    w_out.astype(_BF16), norm_post_g.reshape(1, -1))
    return out.reshape(bsz, seq_len, D_MODEL)
```

```python
import functools

import jax
import jax.numpy as jnp
from jax import lax
from jax.experimental import pallas as pl
from jax.experimental.pallas import tpu as pltpu

D_MODEL = 2048
D_CONV = 1024
D_SSM = 1024
SSM_GROUP = 16
SSM_GROUPS = 64
SSM_STATE = 64
N_IN = 4 * D_CONV + 2 * D_SSM
EPS = 1e-6

S5_CHUNK = 32
S5_TH = S5_CHUNK * SSM_GROUP
S5_POW_ROWS = S5_CHUNK + 8
S5_SCAN_PITCH = 24
LANES = 128
SUBLANES = 8
GROUPS_PER_BLOCK = LANES // SSM_GROUP
TOKEN_TILE = 512
COL_BLOCK = 256
VMEM_LIMIT_BYTES = 56 * 1024 * 1024

_F32 = jnp.float32
_BF16 = jnp.bfloat16


def _dot(a, b):
    return jnp.dot(a, b, preferred_element_type=_F32)


def _dot_nt(a, b, precision=None):
    return lax.dot_general(a, b, (((1,), (1,)), ((), ())), precision=precision,
                           preferred_element_type=_F32)


def _sigmoid(z):
    return 1.0 / (1.0 + jnp.exp(-z))


def _lane_block_transpose(xs):
    nb = len(xs)
    blk = lax.broadcasted_iota(jnp.int32, (1, LANES), 1) // SSM_GROUP
    masks = [blk == k for k in range(nb)]
    moved = []
    for d in range(nb):
        w = xs[d]
        for b in range(1, nb):
            w = jnp.where(masks[b], xs[(b + d) % nb], w)
        moved.append(w if d == 0 else pltpu.roll(w, d * SSM_GROUP, axis=1))
    ys = []
    for b in range(nb):
        y = moved[0]
        for d in range(1, nb):
            y = jnp.where(masks[(b + d) % nb], moved[d], y)
        ys.append(y)
    return ys


def _inproj_kernel(x_ref, g_ref, w_ref, cw_ref, cb_ref, yc_ref, ug_ref, zs_ref, carry_ref, slab_ref, *,
                   tiles_per_seq):
    tm = x_ref.shape[0]
    nb, sub = GROUPS_PER_BLOCK, SUBLANES
    x = x_ref[...]
    ms = jnp.mean(x * x, axis=-1, keepdims=True)
    h = (x * lax.rsqrt(ms + EPS) * g_ref[...]).astype(_BF16)

    @pl.when(pl.program_id(0) % tiles_per_seq == 0)
    def _():
        carry_ref[...] = jnp.zeros_like(carry_ref)

    def ssm_block(lo):
        sl = slice(lo, lo + COL_BLOCK)
        u = _dot(h, w_ref[:, 4 * D_CONV + lo:4 * D_CONV + lo + COL_BLOCK])
        z = _dot(h, w_ref[:, 4 * D_CONV + D_SSM + lo:4 * D_CONV + D_SSM + lo + COL_BLOCK])
        zs_ref[:, sl] = (z * _sigmoid(z)).astype(_BF16)
        for k in range(COL_BLOCK // LANES):
            gv = lo // LANES + k
            slab_ref[gv] = u[:, k * LANES:(k + 1) * LANES]
            rows = [slab_ref[gv, pl.ds(tl, tm // sub, stride=sub), :] for tl in range(sub)]
            for gl, o in enumerate(_lane_block_transpose(rows)):
                ug_ref[gv * nb + gl] = o

    def conv_block(lo):
        sl = slice(lo, lo + COL_BLOCK)
        b_gate = _dot(h, w_ref[:, 0 * D_CONV + lo:0 * D_CONV + lo + COL_BLOCK])
        c_gate = _dot(h, w_ref[:, 1 * D_CONV + lo:1 * D_CONV + lo + COL_BLOCK])
        v = _dot(h, w_ref[:, 2 * D_CONV + lo:2 * D_CONV + lo + COL_BLOCK])
        z = _dot(h, w_ref[:, 3 * D_CONV + lo:3 * D_CONV + lo + COL_BLOCK])
        cv = c_gate * v
        ext = jnp.concatenate([carry_ref[:, sl], cv], axis=0)
        conv = (cb_ref[:, sl] + cw_ref[2:3, sl] * cv + cw_ref[1:2, sl] * ext[7:7 + tm]
                + cw_ref[0:1, sl] * ext[6:6 + tm])
        carry_ref[:, sl] = cv[tm - 8:]
        yc_ref[:, sl] = (b_gate * conv * (z * _sigmoid(z))).astype(_BF16)

    for blk in range(D_CONV // COL_BLOCK):
        ssm_block(blk * COL_BLOCK)
        conv_block(blk * COL_BLOCK)


def _inproj(x2d, norm_g, w_in_bf16, conv_w, conv_b, seq_len):
    n_tok = x2d.shape[0]
    tm = TOKEN_TILE
    const = lambda i: (0, 0)
    out_sds = jax.ShapeDtypeStruct((n_tok, D_CONV), _BF16)
    return pl.pallas_call(
        functools.partial(_inproj_kernel, tiles_per_seq=seq_len // tm),
        grid=(n_tok // tm,),
        in_specs=[
            pl.BlockSpec((tm, D_MODEL), lambda i: (i, 0)),
            pl.BlockSpec((1, D_MODEL), const),
            pl.BlockSpec((D_MODEL, N_IN), const, pipeline_mode=pl.Buffered(1)),
            pl.BlockSpec((3, D_CONV), const),
            pl.BlockSpec((1, D_CONV), const),
        ],
        out_specs=[pl.BlockSpec((tm, D_CONV), lambda i: (i, 0)),
                   pl.BlockSpec((SSM_GROUPS, tm // SUBLANES, LANES), lambda i: (0, i, 0)),
                   pl.BlockSpec((tm, D_SSM), lambda i: (i, 0))],
        out_shape=[out_sds,
                   jax.ShapeDtypeStruct((SSM_GROUPS, n_tok // SUBLANES, LANES), _F32),
                   out_sds],
        scratch_shapes=[pltpu.VMEM((8, D_CONV), _F32),
                        pltpu.VMEM((D_SSM // LANES, tm, LANES), _F32)],
        compiler_params=pltpu.CompilerParams(dimension_semantics=("arbitrary",),
                                             vmem_limit_bytes=VMEM_LIMIT_BYTES),
        name="inproj_conv",
    )(x2d, norm_g, w_in_bf16, conv_w, conv_b)


def _complex_powers(ar, ai, exponent, nbits):
    pr = jnp.ones((exponent.shape[0], ar.shape[1]), _F32)
    pi = jnp.zeros_like(pr)
    cr, ci = ar, ai
    for b in range(nbits):
        bit = ((exponent >> b) & 1) == 1
        fr = jnp.where(bit, cr, 1.0)
        fi = jnp.where(bit, ci, 0.0)
        pr, pi = pr * fr - pi * fi, pr * fi + pi * fr
        cr, ci = cr * cr - ci * ci, 2.0 * cr * ci
    return pr, pi


def _re_im_sign():
    lane = lax.broadcasted_iota(jnp.int32, (1, LANES), 1)
    return jnp.where(lane < SSM_STATE, -1.0, 1.0).astype(_F32)


def _s5_prep_kernel(lr_ref, li_ref, ldt_ref, bt1_ref, bt2_ref, ct1_ref, ct2_ref, d_ref,
                    kf_ref, ws_ref, wo_ref, a1_ref, a2_ref):
    t, th, jrows, h = S5_CHUNK, S5_TH, S5_POW_ROWS, SSM_GROUP
    sgn = _re_im_sign()
    nbits = max(1, (jrows - 1).bit_length())
    j = lax.broadcasted_iota(jnp.int32, (jrows, 1), 0)
    diag = (lax.broadcasted_iota(jnp.int32, (h, th), 0) == lax.broadcasted_iota(jnp.int32, (h, th), 1))
    hi = lax.Precision.HIGHEST
    for g in range(lr_ref.shape[0]):
        lr, li = lr_ref[g], li_ref[g]
        dt = jnp.exp(ldt_ref[g])
        mag = jnp.exp(lr * dt)
        ar, ai = mag * jnp.cos(li * dt), mag * jnp.sin(li * dt)
        nr, ni = ar - 1.0, ai
        den = lr * lr + li * li
        qr = (nr * lr + ni * li) / den
        qi = (ni * lr - nr * li) / den
        bt1, bt2 = bt1_ref[g], bt2_ref[g]
        b1 = qr * bt1 + (qi * sgn) * bt2
        b2 = (qr * sgn) * bt2 - qi * bt1
        c1 = ct1_ref[g] * (-sgn)
        c2 = -ct2_ref[g]
        pr, pi = _complex_powers(ar, ai, j, nbits)
        rr, ri = _complex_powers(ar, ai, jnp.maximum(t - 1 - j, 0), nbits)
        cps = (pr[:, None, :] * c1[None] + pi[:, None, :] * c2[None]).reshape(jrows * h, LANES)
        kf_ref[g] = _dot_nt(b1, cps[:th], hi) + jnp.where(diag, d_ref[g], 0.0)
        wo_ref[g] = cps[h:h + th].astype(_BF16)
        ws_ref[g] = (rr[:t, None, :] * b1[None] + ri[:t, None, :] * b2[None]).reshape(th, LANES).astype(_BF16)
        a1_ref[g] = pr[t:t + 1]
        a2_ref[g] = pi[t:t + 1]


def _s5_prep(lr, li, ldt, bt1, bt2, ct1, ct2, d_pad):
    g, h, th, nb = SSM_GROUPS, SSM_GROUP, S5_TH, GROUPS_PER_BLOCK
    row = pl.BlockSpec((nb, 1, LANES), lambda i: (i, 0, 0))
    hp = pl.BlockSpec((nb, h, LANES), lambda i: (i, 0, 0))
    thp = pl.BlockSpec((nb, th, LANES), lambda i: (i, 0, 0))
    return pl.pallas_call(
        _s5_prep_kernel,
        grid=(g // nb,),
        in_specs=[row, row, row, hp, hp, hp, hp, pl.BlockSpec((nb, 1, th), lambda i: (i, 0, 0))],
        out_specs=[pl.BlockSpec((nb, h, th), lambda i: (i, 0, 0)), thp, thp, row, row],
        out_shape=[jax.ShapeDtypeStruct((g, h, th), _F32)]
        + [jax.ShapeDtypeStruct((g, th, LANES), _BF16)] * 2
        + [jax.ShapeDtypeStruct((g, 1, LANES), _F32)] * 2,
        compiler_params=pltpu.CompilerParams(dimension_semantics=("arbitrary",)),
        name="s5_prep",
    )(lr, li, ldt, bt1, bt2, ct1, ct2, d_pad)


def _s5_kernel(ug_ref, kf_ref, ws_ref, wo_ref, a1_ref, a2_ref, yg_ref,
               mt_ref, xg_ref, yi_ref, sl_ref, slsw_ref, sin_ref, *, chunks_per_seq):
    nb = ug_ref.shape[0]
    t, th, h = S5_CHUNK, S5_TH, SSM_GROUP
    t_hi_n = t // SUBLANES
    n_chunks = ug_ref.shape[1] // t_hi_n
    n_seq = n_chunks // chunks_per_seq
    pitch = S5_SCAN_PITCH

    for g in range(nb):
        xg = jnp.concatenate([ug_ref[g, pl.ds(k, n_chunks, stride=t_hi_n), :]
                              for k in range(t_hi_n)], axis=1).astype(_BF16)
        xg_ref[g] = xg
        s = _dot(xg, ws_ref[g])
        s_sw = pltpu.roll(s, SSM_STATE, axis=1)
        for b in range(n_seq):
            rows = slice(b * chunks_per_seq, (b + 1) * chunks_per_seq)
            sl_ref[pl.ds(b * nb + g, chunks_per_seq, stride=pitch), :] = s[rows]
            slsw_ref[pl.ds(b * nb + g, chunks_per_seq, stride=pitch), :] = s_sw[rows]

    a1 = jnp.concatenate([a1_ref[g] for g in range(nb)] * n_seq, axis=0)
    a2 = jnp.concatenate([a2_ref[g] for g in range(nb)] * n_seq, axis=0) * _re_im_sign()
    s = jnp.zeros((n_seq * nb, LANES), _F32)
    s_sw = s
    for c in range(chunks_per_seq):
        sin_ref[c * pitch:c * pitch + n_seq * nb, :] = s
        s, s_sw = (a1 * s + a2 * s_sw + sl_ref[c * pitch:c * pitch + n_seq * nb, :],
                   a1 * s_sw - a2 * s + slsw_ref[c * pitch:c * pitch + n_seq * nb, :])

    for g in range(nb):
        zk = jnp.concatenate([jnp.zeros((h, th), _F32), kf_ref[g]], axis=1)
        for k in range(t):
            mt_ref[g, k * h:(k + 1) * h, :] = zk[:, th - k * h:2 * th - k * h].astype(_BF16)
        yi_ref[g] = _dot(xg_ref[g], mt_ref[g])

    for g in range(nb):
        s_in = jnp.concatenate([sin_ref[pl.ds(b * nb + g, chunks_per_seq, stride=pitch), :]
                                for b in range(n_seq)], axis=0).astype(_BF16)
        y = yi_ref[g] + _dot_nt(s_in, wo_ref[g])
        for k in range(t_hi_n):
            yg_ref[g, pl.ds(k, n_chunks, stride=t_hi_n), :] = y[:, k * LANES:(k + 1) * LANES]


def _s5_core(ug, prep, chunks_per_seq):
    _, n_rows, _ = ug.shape
    h, th, nb = SSM_GROUP, S5_TH, GROUPS_PER_BLOCK
    kf, ws, wo, a1, a2 = prep
    rows = pl.BlockSpec((nb, n_rows, LANES), lambda i: (i, 0, 0))
    thp = pl.BlockSpec((nb, th, LANES), lambda i: (i, 0, 0))
    row = pl.BlockSpec((nb, 1, LANES), lambda i: (i, 0, 0))
    return pl.pallas_call(
        functools.partial(_s5_kernel, chunks_per_seq=chunks_per_seq),
        grid=(SSM_GROUPS // nb,),
        in_specs=[rows, pl.BlockSpec((nb, h, th), lambda i: (i, 0, 0)), thp, thp, row, row],
        out_specs=rows,
        out_shape=jax.ShapeDtypeStruct(ug.shape, _F32),
        scratch_shapes=[pltpu.VMEM((nb, th, th), _BF16),
                        pltpu.VMEM((nb, n_rows * SUBLANES // S5_CHUNK, th), _BF16),
                        pltpu.VMEM((nb, n_rows * SUBLANES // S5_CHUNK, th), _F32)]
        + [pltpu.VMEM((chunks_per_seq * S5_SCAN_PITCH, LANES), _F32)] * 3,
        compiler_params=pltpu.CompilerParams(dimension_semantics=("arbitrary",),
                                             vmem_limit_bytes=VMEM_LIMIT_BYTES),
        name="s5_core",
    )(ug, kf, ws, wo, a1, a2)


def _out_kernel(yc_ref, yg_ref, zs_ref, x_ref, wg_ref, bg_ref, wo_ref, gp_ref, o_ref, slab_ref):
    tm = x_ref.shape[0]
    nb, sub = GROUPS_PER_BLOCK, SUBLANES
    o = _dot(yc_ref[...], wo_ref[:D_CONV])
    ys, pre = [], bg_ref[...]
    slabs_per_block = COL_BLOCK // LANES
    for blk in range(D_SSM // COL_BLOCK):
        for gv in range(blk * slabs_per_block, (blk + 1) * slabs_per_block):
            rows = _lane_block_transpose([yg_ref[gv * nb + gl] for gl in range(nb)])
            for tl in range(sub):
                slab_ref[gv, pl.ds(tl, tm // sub, stride=sub), :] = rows[tl]
        y = jax.nn.gelu(jnp.concatenate(
            [slab_ref[gv] for gv in range(blk * slabs_per_block, (blk + 1) * slabs_per_block)], axis=1))
        ys.append(y)
        pre = pre + _dot(y.astype(_BF16), wg_ref[blk * COL_BLOCK:(blk + 1) * COL_BLOCK, :])
    y = jnp.concatenate(ys, axis=1)
    ys = (y * _sigmoid(pre) * zs_ref[...].astype(_F32)).astype(_BF16)
    o = o + _dot(ys, wo_ref[D_CONV:])
    ms = jnp.mean(o * o, axis=-1, keepdims=True)
    o_ref[...] = x_ref[...] + o * lax.rsqrt(ms + EPS) * gp_ref[...]


def _out_proj(yc, yg, zs, x2d, w_glu_bf16, b_glu, w_out_bf16, norm_g):
    n_tok = x2d.shape[0]
    tm = TOKEN_TILE
    const = lambda i: (0, 0)
    tile = lambda w: pl.BlockSpec((tm, w), lambda i: (i, 0))
    return pl.pallas_call(
        _out_kernel,
        grid=(n_tok // tm,),
        in_specs=[
            tile(D_CONV),
            pl.BlockSpec((SSM_GROUPS, tm // SUBLANES, LANES), lambda i: (0, i, 0)),
            tile(D_SSM), tile(D_MODEL),
            pl.BlockSpec((D_SSM, D_SSM), const, pipeline_mode=pl.Buffered(1)),
            pl.BlockSpec((1, D_SSM), const),
            pl.BlockSpec((D_CONV + D_SSM, D_MODEL), const, pipeline_mode=pl.Buffered(1)),
            pl.BlockSpec((1, D_MODEL), const),
        ],
        out_specs=tile(D_MODEL),
        out_shape=jax.ShapeDtypeStruct((n_tok, D_MODEL), _F32),
        scratch_shapes=[pltpu.VMEM((D_SSM // LANES, tm, LANES), _F32)],
        compiler_params=pltpu.CompilerParams(dimension_semantics=("arbitrary",),
                                             vmem_limit_bytes=VMEM_LIMIT_BYTES),
        name="glu_outproj",
    )(yc, yg, zs, x2d, w_glu_bf16, b_glu, w_out_bf16, norm_g)


def kernel(x, norm_pre_g, w_in, conv_w, conv_b, ssm_a_re, ssm_a_im, ssm_log_dt, ssm_b_re, ssm_b_im,
           ssm_c_re, ssm_c_im, ssm_d, w_glu, b_glu, w_out, norm_post_g):
    bsz, seq_len, _ = x.shape
    n_tok = bsz * seq_len
    g, p, h, t = SSM_GROUPS, SSM_STATE, SSM_GROUP, S5_CHUNK
    assert seq_len % TOKEN_TILE == 0 and TOKEN_TILE % t == 0
    x2d = x.reshape(n_tok, D_MODEL)

    yc, ug, zs = _inproj(x2d, norm_pre_g.reshape(1, -1), w_in.astype(_BF16), conv_w,
                         conv_b.reshape(1, -1), seq_len)

    both = lambda a, b: jnp.concatenate([a, b], axis=-1)
    brt, bit = ssm_b_re.transpose(0, 2, 1), ssm_b_im.transpose(0, 2, 1)
    a_re, a_im = ssm_a_re.reshape(g, 1, p), ssm_a_im.reshape(g, 1, p)
    d_pad = jnp.pad(ssm_d.reshape(g, 1, h), ((0, 0), (0, 0), (0, S5_TH - h)))
    prep = _s5_prep(
        both(a_re, a_re), both(a_im, a_im),
        jnp.broadcast_to(ssm_log_dt.reshape(g, 1, 1), (g, 1, LANES)),
        both(brt, bit), both(bit, brt), both(ssm_c_re, ssm_c_im), both(ssm_c_im, ssm_c_re), d_pad)

    yg = _s5_core(ug, prep, seq_len // t)

    out = _out_proj(yc, yg, zs, x2d, w_glu.astype(_BF16), b_glu.reshape(1, -1),
                    w_out.astype(_BF16), norm_post_g.reshape(1, -1))
    return out.reshape(bsz, seq_len, D_MODEL)
```

```python
import functools

import jax
import jax.numpy as jnp
from jax import lax
from jax.experimental import pallas as pl
from jax.experimental.pallas import tpu as pltpu

D_MODEL = 2048
D_CONV = 1024
D_SSM = 1024
SSM_GROUP = 16
SSM_GROUPS = 64
SSM_STATE = 64
N_IN = 4 * D_CONV + 2 * D_SSM
EPS = 1e-6

S5_CHUNK = 32
S5_TH = S5_CHUNK * SSM_GROUP
S5_POW_ROWS = S5_CHUNK + 8
S5_SCAN_PITCH = 24
LANES = 128
SUBLANES = 8
GROUPS_PER_BLOCK = LANES // SSM_GROUP
TOKEN_TILE = 512
COL_BLOCK = 256
VMEM_LIMIT_BYTES = 56 * 1024 * 1024

_F32 = jnp.float32
_BF16 = jnp.bfloat16


def _dot(a, b):
    return jnp.dot(a, b, preferred_element_type=_F32)


def _dot_nt(a, b, precision=None):
    return lax.dot_general(a, b, (((1,), (1,)), ((), ())), precision=precision,
                           preferred_element_type=_F32)


def _sigmoid(z):
    return 1.0 / (1.0 + jnp.exp(-z))


def _lane_block_transpose(xs):
    nb = len(xs)
    blk = lax.broadcasted_iota(jnp.int32, (1, LANES), 1) // SSM_GROUP
    masks = [blk == k for k in range(nb)]
    moved = []
    for d in range(nb):
        w = xs[d]
        for b in range(1, nb):
            w = jnp.where(masks[b], xs[(b + d) % nb], w)
        moved.append(w if d == 0 else pltpu.roll(w, d * SSM_GROUP, axis=1))
    ys = []
    for b in range(nb):
        y = moved[0]
        for d in range(1, nb):
            y = jnp.where(masks[(b + d) % nb], moved[d], y)
        ys.append(y)
    return ys


def _inproj_kernel(x_ref, g_ref, w_ref, cw_ref, cb_ref, yc_ref, ug_ref, zs_ref, carry_ref, slab_ref,
                   raw_ref, *, tiles_per_seq):
    tm = x_ref.shape[0]
    nb, sub = GROUPS_PER_BLOCK, SUBLANES
    x = x_ref[...]
    ms = jnp.mean(x * x, axis=-1, keepdims=True)
    h = (x * lax.rsqrt(ms + EPS) * g_ref[...]).astype(_BF16)

    slabs_per_block = COL_BLOCK // LANES
    bases = (0, D_CONV, 2 * D_CONV, 3 * D_CONV, 4 * D_CONV + D_SSM)

    def project_u(blk):
        lo = blk * COL_BLOCK
        u = _dot(h, w_ref[:, 4 * D_CONV + lo:4 * D_CONV + lo + COL_BLOCK])
        for k in range(slabs_per_block):
            slab_ref[blk * slabs_per_block + k] = u[:, k * LANES:(k + 1) * LANES]

    def project_raw(blk, j):
        lo = bases[j] + blk * COL_BLOCK
        raw_ref[blk % 2, j] = _dot(h, w_ref[:, lo:lo + COL_BLOCK])

    def project(blk):
        return ([functools.partial(project_u, blk)]
                + [functools.partial(project_raw, blk, j) for j in range(len(bases))])

    def finish_gate(blk):
        sl = slice(blk * COL_BLOCK, (blk + 1) * COL_BLOCK)
        z_ssm = raw_ref[blk % 2, 4]
        zs_ref[:, sl] = (z_ssm * _sigmoid(z_ssm)).astype(_BF16)

    def finish_u(gv):
        rows = [slab_ref[gv, pl.ds(tl, tm // sub, stride=sub), :] for tl in range(sub)]
        for gl, o in enumerate(_lane_block_transpose(rows)):
            ug_ref[gv * nb + gl] = o

    def finish_conv(blk, half):
        sl = slice(blk * COL_BLOCK + half * LANES, blk * COL_BLOCK + (half + 1) * LANES)
        hl = slice(half * LANES, (half + 1) * LANES)
        b_gate, c_gate, v, z = (raw_ref[blk % 2, j, :, hl] for j in range(4))
        cv = c_gate * v
        ext = jnp.concatenate([carry_ref[:, sl], cv], axis=0)
        conv = (cb_ref[:, sl] + cw_ref[2:3, sl] * cv + cw_ref[1:2, sl] * ext[7:7 + tm]
                + cw_ref[0:1, sl] * ext[6:6 + tm])
        carry_ref[:, sl] = cv[tm - 8:]
        yc_ref[:, sl] = (b_gate * conv * (z * _sigmoid(z))).astype(_BF16)

    def finish(blk):
        return ([functools.partial(finish_gate, blk)]
                + [functools.partial(finish_u, blk * slabs_per_block + k) for k in range(slabs_per_block)]
                + [functools.partial(finish_conv, blk, half) for half in range(COL_BLOCK // LANES)])

    def interleave(mxu_items, vpu_items):
        n = max(len(mxu_items), 1)
        done = 0
        for i, item in enumerate(mxu_items):
            item()
            upto = (len(vpu_items) * (i + 1)) // n
            for piece in vpu_items[done:upto]:
                piece()
            done = upto
        for piece in vpu_items[done:]:
            piece()

    @pl.when(pl.program_id(0) % tiles_per_seq == 0)
    def _():
        carry_ref[...] = jnp.zeros_like(carry_ref)

    n_blocks = D_CONV // COL_BLOCK
    interleave(project(0), [])
    for blk in range(1, n_blocks):
        interleave(project(blk), finish(blk - 1))
    interleave([], finish(n_blocks - 1))


def _inproj(x2d, norm_g, w_in_bf16, conv_w, conv_b, seq_len):
    n_tok = x2d.shape[0]
    tm = TOKEN_TILE
    n_tiles = n_tok // tm
    const = lambda s: (0, 0)
    read = write = lambda s: (s, 0)
    out_sds = jax.ShapeDtypeStruct((n_tok, D_CONV), _BF16)
    return pl.pallas_call(
        functools.partial(_inproj_kernel, tiles_per_seq=seq_len // tm),
        grid=(n_tiles,),
        in_specs=[
            pl.BlockSpec((tm, D_MODEL), read),
            pl.BlockSpec((1, D_MODEL), const),
            pl.BlockSpec((D_MODEL, N_IN), const, pipeline_mode=pl.Buffered(1)),
            pl.BlockSpec((3, D_CONV), const),
            pl.BlockSpec((1, D_CONV), const),
        ],
        out_specs=[pl.BlockSpec((tm, D_CONV), write),
                   pl.BlockSpec((SSM_GROUPS, tm // SUBLANES, LANES),
                                lambda s: (0, s, 0)),
                   pl.BlockSpec((tm, D_SSM), write)],
        out_shape=[out_sds,
                   jax.ShapeDtypeStruct((SSM_GROUPS, n_tok // SUBLANES, LANES), _F32),
                   out_sds],
        scratch_shapes=[pltpu.VMEM((8, D_CONV), _F32),
                        pltpu.VMEM((D_SSM // LANES, tm, LANES), _F32),
                        pltpu.VMEM((2, 5, tm, COL_BLOCK), _F32)],
        compiler_params=pltpu.CompilerParams(dimension_semantics=("arbitrary",),
                                             vmem_limit_bytes=VMEM_LIMIT_BYTES),
        name="inproj_conv",
    )(x2d, norm_g, w_in_bf16, conv_w, conv_b)


def _complex_powers(ar, ai, exponent, nbits):
    pr = jnp.ones((exponent.shape[0], ar.shape[1]), _F32)
    pi = jnp.zeros_like(pr)
    cr, ci = ar, ai
    for b in range(nbits):
        bit = ((exponent >> b) & 1) == 1
        fr = jnp.where(bit, cr, 1.0)
        fi = jnp.where(bit, ci, 0.0)
        pr, pi = pr * fr - pi * fi, pr * fi + pi * fr
        cr, ci = cr * cr - ci * ci, 2.0 * cr * ci
    return pr, pi


def _re_im_sign():
    lane = lax.broadcasted_iota(jnp.int32, (1, LANES), 1)
    return jnp.where(lane < SSM_STATE, -1.0, 1.0).astype(_F32)


def _s5_prep_kernel(lr_ref, li_ref, ldt_ref, bt1_ref, bt2_ref, ct1_ref, ct2_ref, d_ref,
                    kf_ref, ws_ref, wo_ref, a1_ref, a2_ref):
    t, th, jrows, h = S5_CHUNK, S5_TH, S5_POW_ROWS, SSM_GROUP
    sgn = _re_im_sign()
    nbits = max(1, (jrows - 1).bit_length())
    j = lax.broadcasted_iota(jnp.int32, (jrows, 1), 0)
    diag = (lax.broadcasted_iota(jnp.int32, (h, th), 0) == lax.broadcasted_iota(jnp.int32, (h, th), 1))
    hi = lax.Precision.HIGHEST
    for g in range(lr_ref.shape[0]):
        lr, li = lr_ref[g], li_ref[g]
        dt = jnp.exp(ldt_ref[g])
        mag = jnp.exp(lr * dt)
        ar, ai = mag * jnp.cos(li * dt), mag * jnp.sin(li * dt)
        nr, ni = ar - 1.0, ai
        den = lr * lr + li * li
        qr = (nr * lr + ni * li) / den
        qi = (ni * lr - nr * li) / den
        bt1, bt2 = bt1_ref[g], bt2_ref[g]
        b1 = qr * bt1 + (qi * sgn) * bt2
        b2 = (qr * sgn) * bt2 - qi * bt1
        c1 = ct1_ref[g] * (-sgn)
        c2 = -ct2_ref[g]
        pr, pi = _complex_powers(ar, ai, j, nbits)
        rr, ri = _complex_powers(ar, ai, jnp.maximum(t - 1 - j, 0), nbits)
        cps = (pr[:, None, :] * c1[None] + pi[:, None, :] * c2[None]).reshape(jrows * h, LANES)
        kf_ref[g] = _dot_nt(b1, cps[:th], hi) + jnp.where(diag, d_ref[g], 0.0)
        wo_ref[g] = cps[h:h + th].astype(_BF16)
        ws_ref[g] = (rr[:t, None, :] * b1[None] + ri[:t, None, :] * b2[None]).reshape(th, LANES).astype(_BF16)
        a1_ref[g] = pr[t:t + 1]
        a2_ref[g] = pi[t:t + 1]


def _s5_prep(lr, li, ldt, bt1, bt2, ct1, ct2, d_pad):
    g, h, th, nb = SSM_GROUPS, SSM_GROUP, S5_TH, GROUPS_PER_BLOCK
    row = pl.BlockSpec((nb, 1, LANES), lambda i: (i, 0, 0))
    hp = pl.BlockSpec((nb, h, LANES), lambda i: (i, 0, 0))
    thp = pl.BlockSpec((nb, th, LANES), lambda i: (i, 0, 0))
    return pl.pallas_call(
        _s5_prep_kernel,
        grid=(g // nb,),
        in_specs=[row, row, row, hp, hp, hp, hp, pl.BlockSpec((nb, 1, th), lambda i: (i, 0, 0))],
        out_specs=[pl.BlockSpec((nb, h, th), lambda i: (i, 0, 0)), thp, thp, row, row],
        out_shape=[jax.ShapeDtypeStruct((g, h, th), _F32)]
        + [jax.ShapeDtypeStruct((g, th, LANES), _BF16)] * 2
        + [jax.ShapeDtypeStruct((g, 1, LANES), _F32)] * 2,
        compiler_params=pltpu.CompilerParams(dimension_semantics=("arbitrary",)),
        name="s5_prep",
    )(lr, li, ldt, bt1, bt2, ct1, ct2, d_pad)


def _s5_kernel(ug_ref, kf_ref, ws_ref, wo_ref, a1_ref, a2_ref, yg_ref,
               mt_ref, xg_ref, yi_ref, sl_ref, slsw_ref, sin_ref, *, chunks_per_seq):
    nb = ug_ref.shape[0]
    t, th, h = S5_CHUNK, S5_TH, SSM_GROUP
    t_hi_n = t // SUBLANES
    n_chunks = ug_ref.shape[1] // t_hi_n
    n_seq = n_chunks // chunks_per_seq
    pitch = S5_SCAN_PITCH

    for g in range(nb):
        xg = jnp.concatenate([ug_ref[g, pl.ds(k, n_chunks, stride=t_hi_n), :]
                              for k in range(t_hi_n)], axis=1).astype(_BF16)
        xg_ref[g] = xg
        s = _dot(xg, ws_ref[g])
        s_sw = pltpu.roll(s, SSM_STATE, axis=1)
        for b in range(n_seq):
            rows = slice(b * chunks_per_seq, (b + 1) * chunks_per_seq)
            sl_ref[pl.ds(b * nb + g, chunks_per_seq, stride=pitch), :] = s[rows]
            slsw_ref[pl.ds(b * nb + g, chunks_per_seq, stride=pitch), :] = s_sw[rows]

    a1 = jnp.concatenate([a1_ref[g] for g in range(nb)] * n_seq, axis=0)
    a2 = jnp.concatenate([a2_ref[g] for g in range(nb)] * n_seq, axis=0) * _re_im_sign()
    s = jnp.zeros((n_seq * nb, LANES), _F32)
    s_sw = s
    for c in range(chunks_per_seq):
        sin_ref[c * pitch:c * pitch + n_seq * nb, :] = s
        s, s_sw = (a1 * s + a2 * s_sw + sl_ref[c * pitch:c * pitch + n_seq * nb, :],
                   a1 * s_sw - a2 * s + slsw_ref[c * pitch:c * pitch + n_seq * nb, :])

    for g in range(nb):
        zk = jnp.concatenate([jnp.zeros((h, th), _F32), kf_ref[g]], axis=1)
        for k in range(t):
            mt_ref[g, k * h:(k + 1) * h, :] = zk[:, th - k * h:2 * th - k * h].astype(_BF16)
        yi_ref[g] = _dot(xg_ref[g], mt_ref[g])

    for g in range(nb):
        s_in = jnp.concatenate([sin_ref[pl.ds(b * nb + g, chunks_per_seq, stride=pitch), :]
                                for b in range(n_seq)], axis=0).astype(_BF16)
        y = yi_ref[g] + _dot_nt(s_in, wo_ref[g])
        for k in range(t_hi_n):
            yg_ref[g, pl.ds(k, n_chunks, stride=t_hi_n), :] = y[:, k * LANES:(k + 1) * LANES]


def _s5_core(ug, prep, chunks_per_seq):
    _, n_rows, _ = ug.shape
    h, th, nb = SSM_GROUP, S5_TH, GROUPS_PER_BLOCK
    kf, ws, wo, a1, a2 = prep
    rows = pl.BlockSpec((nb, n_rows, LANES), lambda i: (i, 0, 0))
    thp = pl.BlockSpec((nb, th, LANES), lambda i: (i, 0, 0))
    row = pl.BlockSpec((nb, 1, LANES), lambda i: (i, 0, 0))
    return pl.pallas_call(
        functools.partial(_s5_kernel, chunks_per_seq=chunks_per_seq),
        grid=(SSM_GROUPS // nb,),
        in_specs=[rows, pl.BlockSpec((nb, h, th), lambda i: (i, 0, 0)), thp, thp, row, row],
        out_specs=rows,
        out_shape=jax.ShapeDtypeStruct(ug.shape, _F32),
        scratch_shapes=[pltpu.VMEM((nb, th, th), _BF16),
                        pltpu.VMEM((nb, n_rows * SUBLANES // S5_CHUNK, th), _BF16),
                        pltpu.VMEM((nb, n_rows * SUBLANES // S5_CHUNK, th), _F32)]
        + [pltpu.VMEM((chunks_per_seq * S5_SCAN_PITCH, LANES), _F32)] * 3,
        compiler_params=pltpu.CompilerParams(dimension_semantics=("arbitrary",),
                                             vmem_limit_bytes=VMEM_LIMIT_BYTES),
        name="s5_core",
    )(ug, kf, ws, wo, a1, a2)


def _out_kernel(yc_ref, yg_ref, zs_ref, x_ref, wg_ref, bg_ref, wo_ref, gp_ref, o_ref, slab_ref):
    tm = x_ref.shape[0]
    nb, sub = GROUPS_PER_BLOCK, SUBLANES
    o = _dot(yc_ref[...], wo_ref[:D_CONV])
    ys, pre = [], bg_ref[...]
    slabs_per_block = COL_BLOCK // LANES
    for blk in range(D_SSM // COL_BLOCK):
        for gv in range(blk * slabs_per_block, (blk + 1) * slabs_per_block):
            rows = _lane_block_transpose([yg_ref[gv * nb + gl] for gl in range(nb)])
            for tl in range(sub):
                slab_ref[gv, pl.ds(tl, tm // sub, stride=sub), :] = rows[tl]
        y = jax.nn.gelu(jnp.concatenate(
            [slab_ref[gv] for gv in range(blk * slabs_per_block, (blk + 1) * slabs_per_block)], axis=1))
        ys.append(y)
        pre = pre + _dot(y.astype(_BF16), wg_ref[blk * COL_BLOCK:(blk + 1) * COL_BLOCK, :])
    y = jnp.concatenate(ys, axis=1)
    ys = (y * _sigmoid(pre) * zs_ref[...].astype(_F32)).astype(_BF16)
    o = o + _dot(ys, wo_ref[D_CONV:])
    ms = jnp.mean(o * o, axis=-1, keepdims=True)
    o_ref[...] = x_ref[...] + o * lax.rsqrt(ms + EPS) * gp_ref[...]


def _out_proj(yc, yg, zs, x2d, w_glu_bf16, b_glu, w_out_bf16, norm_g):
    n_tok = x2d.shape[0]
    tm = TOKEN_TILE
    const = lambda i: (0, 0)
    tile = lambda w: pl.BlockSpec((tm, w), lambda i: (i, 0))
    return pl.pallas_call(
        _out_kernel,
        grid=(n_tok // tm,),
        in_specs=[
            tile(D_CONV),
            pl.BlockSpec((SSM_GROUPS, tm // SUBLANES, LANES), lambda i: (0, i, 0)),
            tile(D_SSM), tile(D_MODEL),
            pl.BlockSpec((D_SSM, D_SSM), const, pipeline_mode=pl.Buffered(1)),
            pl.BlockSpec((1, D_SSM), const),
            pl.BlockSpec((D_CONV + D_SSM, D_MODEL), const, pipeline_mode=pl.Buffered(1)),
            pl.BlockSpec((1, D_MODEL), const),
        ],
        out_specs=tile(D_MODEL),
        out_shape=jax.ShapeDtypeStruct((n_tok, D_MODEL), _F32),
        scratch_shapes=[pltpu.VMEM((D_SSM // LANES, tm, LANES), _F32)],
        compiler_params=pltpu.CompilerParams(dimension_semantics=("arbitrary",),
                                             vmem_limit_bytes=VMEM_LIMIT_BYTES),
        name="glu_outproj",
    )(yc, yg, zs, x2d, w_glu_bf16, b_glu, w_out_bf16, norm_g)


def kernel(x, norm_pre_g, w_in, conv_w, conv_b, ssm_a_re, ssm_a_im, ssm_log_dt, ssm_b_re, ssm_b_im,
           ssm_c_re, ssm_c_im, ssm_d, w_glu, b_glu, w_out, norm_post_g):
    bsz, seq_len, _ = x.shape
    n_tok = bsz * seq_len
    g, p, h, t = SSM_GROUPS, SSM_STATE, SSM_GROUP, S5_CHUNK
    assert seq_len % TOKEN_TILE == 0 and TOKEN_TILE % t == 0
    x2d = x.reshape(n_tok, D_MODEL)

    yc, ug, zs = _inproj(x2d, norm_pre_g.reshape(1, -1), w_in.astype(_BF16), conv_w,
                         conv_b.reshape(1, -1), seq_len)

    both = lambda a, b: jnp.concatenate([a, b], axis=-1)
    brt, bit = ssm_b_re.transpose(0, 2, 1), ssm_b_im.transpose(0, 2, 1)
    a_re, a_im = ssm_a_re.reshape(g, 1, p), ssm_a_im.reshape(g, 1, p)
    d_pad = jnp.pad(ssm_d.reshape(g, 1, h), ((0, 0), (0, 0), (0, S5_TH - h)))
    prep = _s5_prep(
        both(a_re, a_re), both(a_im, a_im),
        jnp.broadcast_to(ssm_log_dt.reshape(g, 1, 1), (g, 1, LANES)),
        both(brt, bit), both(bit, brt), both(ssm_c_re, ssm_c_im), both(ssm_c_im, ssm_c_re), d_pad)

    yg = _s5_core(ug, prep, seq_len // t)

    out = _out_proj(yc, yg, zs, x2d, w_glu.astype(_BF16), b_glu.reshape(1, -1),
                    w_out.astype(_BF16), norm_post_g.reshape(1, -1))
    return out.reshape(bsz, seq_len, D_MODEL)
```

```python
import functools

import jax
import jax.numpy as jnp
from jax import lax
from jax.experimental import pallas as pl
from jax.experimental.pallas import tpu as pltpu

D_MODEL = 2048
D_CONV = 1024
D_SSM = 1024
SSM_GROUP = 16
SSM_GROUPS = 64
SSM_STATE = 64
N_IN = 4 * D_CONV + 2 * D_SSM
EPS = 1e-6

S5_CHUNK = 32
S5_TH = S5_CHUNK * SSM_GROUP
S5_POW_ROWS = S5_CHUNK + 8
S5_SCAN_PITCH = 24
LANES = 128
SUBLANES = 8
GROUPS_PER_BLOCK = LANES // SSM_GROUP
TOKEN_TILE = 512
COL_BLOCK = 256
VMEM_LIMIT_BYTES = 56 * 1024 * 1024

_F32 = jnp.float32
_BF16 = jnp.bfloat16


def _dot(a, b):
    return jnp.dot(a, b, preferred_element_type=_F32)


def _dot_nt(a, b, precision=None):
    return lax.dot_general(a, b, (((1,), (1,)), ((), ())), precision=precision,
                           preferred_element_type=_F32)


def _sigmoid(z):
    return 1.0 / (1.0 + jnp.exp(-z))


def _load_weight_as_bf16(w_hbm, w_ref, stage, sem, cols):
    a, r, n = w_hbm.shape

    def chunk(c, slot):
        return pltpu.make_async_copy(w_hbm.at[:, :, pl.ds(c * cols, cols)], stage(slot), sem.at[slot])

    chunk(0, 0).start()
    for c in range(n // cols):
        slot = c % 2
        if c + 1 < n // cols:
            chunk(c + 1, 1 - slot).start()
        chunk(c, slot).wait()
        for k in range(a):
            w_ref[k * r:(k + 1) * r, c * cols:(c + 1) * cols] = stage(slot)[k].astype(_BF16)


def _lane_block_transpose(xs):
    nb = len(xs)
    blk = lax.broadcasted_iota(jnp.int32, (1, LANES), 1) // SSM_GROUP
    masks = [blk == k for k in range(nb)]
    moved = []
    for d in range(nb):
        w = xs[d]
        for b in range(1, nb):
            w = jnp.where(masks[b], xs[(b + d) % nb], w)
        moved.append(w if d == 0 else pltpu.roll(w, d * SSM_GROUP, axis=1))
    ys = []
    for b in range(nb):
        y = moved[0]
        for d in range(1, nb):
            y = jnp.where(masks[(b + d) % nb], moved[d], y)
        ys.append(y)
    return ys


def _inproj_kernel(x_ref, g_ref, w_hbm, cw_ref, cb_ref, yc_ref, ug_ref, zs_ref, carry_ref, slab_ref,
                   raw_ref, w_ref, sem, *, tiles_per_seq):
    tm = x_ref.shape[0]
    nb, sub = GROUPS_PER_BLOCK, SUBLANES

    @pl.when(pl.program_id(0) == 0)
    def _():
        _load_weight_as_bf16(w_hbm, w_ref, lambda slot: raw_ref.at[slot, pl.ds(0, w_hbm.shape[0])],
                             sem, COL_BLOCK)

    x = x_ref[...]
    ms = jnp.mean(x * x, axis=-1, keepdims=True)
    h = (x * lax.rsqrt(ms + EPS) * g_ref[...]).astype(_BF16)

    slabs_per_block = COL_BLOCK // LANES
    bases = (0, D_CONV, 2 * D_CONV, 3 * D_CONV, 4 * D_CONV + D_SSM)

    def project_u(blk):
        lo = blk * COL_BLOCK
        u = _dot(h, w_ref[:, 4 * D_CONV + lo:4 * D_CONV + lo + COL_BLOCK])
        for k in range(slabs_per_block):
            slab_ref[blk * slabs_per_block + k] = u[:, k * LANES:(k + 1) * LANES]

    def project_raw(blk, j):
        lo = bases[j] + blk * COL_BLOCK
        raw_ref[blk % 2, j] = _dot(h, w_ref[:, lo:lo + COL_BLOCK])

    def project(blk):
        return ([functools.partial(project_u, blk)]
                + [functools.partial(project_raw, blk, j) for j in range(len(bases))])

    def finish_gate(blk):
        sl = slice(blk * COL_BLOCK, (blk + 1) * COL_BLOCK)
        z_ssm = raw_ref[blk % 2, 4]
        zs_ref[:, sl] = (z_ssm * _sigmoid(z_ssm)).astype(_BF16)

    def finish_u(gv):
        rows = [slab_ref[gv, pl.ds(tl, tm // sub, stride=sub), :] for tl in range(sub)]
        for gl, o in enumerate(_lane_block_transpose(rows)):
            ug_ref[gv * nb + gl] = o

    def finish_conv(blk, half):
        sl = slice(blk * COL_BLOCK + half * LANES, blk * COL_BLOCK + (half + 1) * LANES)
        hl = slice(half * LANES, (half + 1) * LANES)
        b_gate, c_gate, v, z = (raw_ref[blk % 2, j, :, hl] for j in range(4))
        cv = c_gate * v
        ext = jnp.concatenate([carry_ref[:, sl], cv], axis=0)
        conv = (cb_ref[:, sl] + cw_ref[2:3, sl] * cv + cw_ref[1:2, sl] * ext[7:7 + tm]
                + cw_ref[0:1, sl] * ext[6:6 + tm])
        carry_ref[:, sl] = cv[tm - 8:]
        yc_ref[:, sl] = (b_gate * conv * (z * _sigmoid(z))).astype(_BF16)

    def finish(blk):
        return ([functools.partial(finish_gate, blk)]
                + [functools.partial(finish_u, blk * slabs_per_block + k) for k in range(slabs_per_block)]
                + [functools.partial(finish_conv, blk, half) for half in range(COL_BLOCK // LANES)])

    def interleave(mxu_items, vpu_items):
        n = max(len(mxu_items), 1)
        done = 0
        for i, item in enumerate(mxu_items):
            item()
            upto = (len(vpu_items) * (i + 1)) // n
            for piece in vpu_items[done:upto]:
                piece()
            done = upto
        for piece in vpu_items[done:]:
            piece()

    @pl.when(pl.program_id(0) % tiles_per_seq == 0)
    def _():
        carry_ref[...] = jnp.zeros_like(carry_ref)

    n_blocks = D_CONV // COL_BLOCK
    interleave(project(0), [])
    for blk in range(1, n_blocks):
        interleave(project(blk), finish(blk - 1))
    interleave([], finish(n_blocks - 1))


def _inproj(x2d, norm_g, w_in, conv_w, conv_b, seq_len):
    n_tok = x2d.shape[0]
    tm = TOKEN_TILE
    n_tiles = n_tok // tm
    const = lambda s: (0, 0)
    read = write = lambda s: (s, 0)
    out_sds = jax.ShapeDtypeStruct((n_tok, D_CONV), _BF16)
    return pl.pallas_call(
        functools.partial(_inproj_kernel, tiles_per_seq=seq_len // tm),
        grid=(n_tiles,),
        in_specs=[
            pl.BlockSpec((tm, D_MODEL), read),
            pl.BlockSpec((1, D_MODEL), const),
            pl.BlockSpec(memory_space=pl.ANY),
            pl.BlockSpec((3, D_CONV), const),
            pl.BlockSpec((1, D_CONV), const),
        ],
        out_specs=[pl.BlockSpec((tm, D_CONV), write),
                   pl.BlockSpec((SSM_GROUPS, tm // SUBLANES, LANES),
                                lambda s: (0, s, 0)),
                   pl.BlockSpec((tm, D_SSM), write)],
        out_shape=[out_sds,
                   jax.ShapeDtypeStruct((SSM_GROUPS, n_tok // SUBLANES, LANES), _F32),
                   out_sds],
        scratch_shapes=[pltpu.VMEM((8, D_CONV), _F32),
                        pltpu.VMEM((D_SSM // LANES, tm, LANES), _F32),
                        pltpu.VMEM((2, 5, tm, COL_BLOCK), _F32),
                        pltpu.VMEM((D_MODEL, N_IN), _BF16),
                        pltpu.SemaphoreType.DMA((2,))],
        compiler_params=pltpu.CompilerParams(dimension_semantics=("arbitrary",),
                                             vmem_limit_bytes=VMEM_LIMIT_BYTES),
        name="inproj_conv",
    )(x2d, norm_g, w_in.reshape(D_MODEL // tm, tm, N_IN), conv_w, conv_b)


def _complex_powers(ar, ai, exponent, nbits):
    pr = jnp.ones((exponent.shape[0], ar.shape[1]), _F32)
    pi = jnp.zeros_like(pr)
    cr, ci = ar, ai
    for b in range(nbits):
        bit = ((exponent >> b) & 1) == 1
        fr = jnp.where(bit, cr, 1.0)
        fi = jnp.where(bit, ci, 0.0)
        pr, pi = pr * fr - pi * fi, pr * fi + pi * fr
        cr, ci = cr * cr - ci * ci, 2.0 * cr * ci
    return pr, pi


def _re_im_sign():
    lane = lax.broadcasted_iota(jnp.int32, (1, LANES), 1)
    return jnp.where(lane < SSM_STATE, -1.0, 1.0).astype(_F32)


def _s5_prep_kernel(lr_ref, li_ref, ldt_ref, bt1_ref, bt2_ref, ct1_ref, ct2_ref, d_ref,
                    kf_ref, ws_ref, wo_ref, a1_ref, a2_ref):
    t, th, jrows, h = S5_CHUNK, S5_TH, S5_POW_ROWS, SSM_GROUP
    sgn = _re_im_sign()
    nbits = max(1, (jrows - 1).bit_length())
    j = lax.broadcasted_iota(jnp.int32, (jrows, 1), 0)
    diag = (lax.broadcasted_iota(jnp.int32, (h, th), 0) == lax.broadcasted_iota(jnp.int32, (h, th), 1))
    hi = lax.Precision.HIGHEST
    for g in range(lr_ref.shape[0]):
        lr, li = lr_ref[g], li_ref[g]
        dt = jnp.exp(ldt_ref[g])
        mag = jnp.exp(lr * dt)
        ar, ai = mag * jnp.cos(li * dt), mag * jnp.sin(li * dt)
        nr, ni = ar - 1.0, ai
        den = lr * lr + li * li
        qr = (nr * lr + ni * li) / den
        qi = (ni * lr - nr * li) / den
        bt1, bt2 = bt1_ref[g], bt2_ref[g]
        b1 = qr * bt1 + (qi * sgn) * bt2
        b2 = (qr * sgn) * bt2 - qi * bt1
        c1 = ct1_ref[g] * (-sgn)
        c2 = -ct2_ref[g]
        pr, pi = _complex_powers(ar, ai, j, nbits)
        rr, ri = _complex_powers(ar, ai, jnp.maximum(t - 1 - j, 0), nbits)
        cps = (pr[:, None, :] * c1[None] + pi[:, None, :] * c2[None]).reshape(jrows * h, LANES)
        kf_ref[g] = _dot_nt(b1, cps[:th], hi) + jnp.where(diag, d_ref[g], 0.0)
        wo_ref[g] = cps[h:h + th].astype(_BF16)
        ws_ref[g] = (rr[:t, None, :] * b1[None] + ri[:t, None, :] * b2[None]).reshape(th, LANES).astype(_BF16)
        a1_ref[g] = pr[t:t + 1]
        a2_ref[g] = pi[t:t + 1]


def _s5_prep(lr, li, ldt, bt1, bt2, ct1, ct2, d_pad):
    g, h, th, nb = SSM_GROUPS, SSM_GROUP, S5_TH, GROUPS_PER_BLOCK
    row = pl.BlockSpec((nb, 1, LANES), lambda i: (i, 0, 0))
    hp = pl.BlockSpec((nb, h, LANES), lambda i: (i, 0, 0))
    thp = pl.BlockSpec((nb, th, LANES), lambda i: (i, 0, 0))
    return pl.pallas_call(
        _s5_prep_kernel,
        grid=(g // nb,),
        in_specs=[row, row, row, hp, hp, hp, hp, pl.BlockSpec((nb, 1, th), lambda i: (i, 0, 0))],
        out_specs=[pl.BlockSpec((nb, h, th), lambda i: (i, 0, 0)), thp, thp, row, row],
        out_shape=[jax.ShapeDtypeStruct((g, h, th), _F32)]
        + [jax.ShapeDtypeStruct((g, th, LANES), _BF16)] * 2
        + [jax.ShapeDtypeStruct((g, 1, LANES), _F32)] * 2,
        compiler_params=pltpu.CompilerParams(dimension_semantics=("arbitrary",)),
        name="s5_prep",
    )(lr, li, ldt, bt1, bt2, ct1, ct2, d_pad)


def _s5_kernel(ug_ref, kf_ref, ws_ref, wo_ref, a1_ref, a2_ref, yg_ref,
               mt_ref, xg_ref, yi_ref, sl_ref, slsw_ref, sin_ref, *, chunks_per_seq):
    nb = ug_ref.shape[0]
    t, th, h = S5_CHUNK, S5_TH, SSM_GROUP
    t_hi_n = t // SUBLANES
    n_chunks = ug_ref.shape[1] // t_hi_n
    n_seq = n_chunks // chunks_per_seq
    pitch = S5_SCAN_PITCH

    for g in range(nb):
        xg = jnp.concatenate([ug_ref[g, pl.ds(k, n_chunks, stride=t_hi_n), :]
                              for k in range(t_hi_n)], axis=1).astype(_BF16)
        xg_ref[g] = xg
        s = _dot(xg, ws_ref[g])
        s_sw = pltpu.roll(s, SSM_STATE, axis=1)
        for b in range(n_seq):
            rows = slice(b * chunks_per_seq, (b + 1) * chunks_per_seq)
            sl_ref[pl.ds(b * nb + g, chunks_per_seq, stride=pitch), :] = s[rows]
            slsw_ref[pl.ds(b * nb + g, chunks_per_seq, stride=pitch), :] = s_sw[rows]

    a1 = jnp.concatenate([a1_ref[g] for g in range(nb)] * n_seq, axis=0)
    a2 = jnp.concatenate([a2_ref[g] for g in range(nb)] * n_seq, axis=0) * _re_im_sign()
    s = jnp.zeros((n_seq * nb, LANES), _F32)
    s_sw = s
    for c in range(chunks_per_seq):
        sin_ref[c * pitch:c * pitch + n_seq * nb, :] = s
        s, s_sw = (a1 * s + a2 * s_sw + sl_ref[c * pitch:c * pitch + n_seq * nb, :],
                   a1 * s_sw - a2 * s + slsw_ref[c * pitch:c * pitch + n_seq * nb, :])

    for g in range(nb):
        zk = jnp.concatenate([jnp.zeros((h, th), _F32), kf_ref[g]], axis=1)
        for k in range(t):
            mt_ref[g, k * h:(k + 1) * h, :] = zk[:, th - k * h:2 * th - k * h].astype(_BF16)
        yi_ref[g] = _dot(xg_ref[g], mt_ref[g])

    for g in range(nb):
        s_in = jnp.concatenate([sin_ref[pl.ds(b * nb + g, chunks_per_seq, stride=pitch), :]
                                for b in range(n_seq)], axis=0).astype(_BF16)
        y = yi_ref[g] + _dot_nt(s_in, wo_ref[g])
        for k in range(t_hi_n):
            yg_ref[g, pl.ds(k, n_chunks, stride=t_hi_n), :] = y[:, k * LANES:(k + 1) * LANES]


def _s5_core(ug, prep, chunks_per_seq):
    _, n_rows, _ = ug.shape
    h, th, nb = SSM_GROUP, S5_TH, GROUPS_PER_BLOCK
    kf, ws, wo, a1, a2 = prep
    rows = pl.BlockSpec((nb, n_rows, LANES), lambda i: (i, 0, 0))
    thp = pl.BlockSpec((nb, th, LANES), lambda i: (i, 0, 0))
    row = pl.BlockSpec((nb, 1, LANES), lambda i: (i, 0, 0))
    return pl.pallas_call(
        functools.partial(_s5_kernel, chunks_per_seq=chunks_per_seq),
        grid=(SSM_GROUPS // nb,),
        in_specs=[rows, pl.BlockSpec((nb, h, th), lambda i: (i, 0, 0)), thp, thp, row, row],
        out_specs=rows,
        out_shape=jax.ShapeDtypeStruct(ug.shape, _F32),
        scratch_shapes=[pltpu.VMEM((nb, th, th), _BF16),
                        pltpu.VMEM((nb, n_rows * SUBLANES // S5_CHUNK, th), _BF16),
                        pltpu.VMEM((nb, n_rows * SUBLANES // S5_CHUNK, th), _F32)]
        + [pltpu.VMEM((chunks_per_seq * S5_SCAN_PITCH, LANES), _F32)] * 3,
        compiler_params=pltpu.CompilerParams(dimension_semantics=("arbitrary",),
                                             vmem_limit_bytes=VMEM_LIMIT_BYTES),
        name="s5_core",
    )(ug, kf, ws, wo, a1, a2)


def _out_kernel(yc_ref, yg_ref, zs_ref, x_ref, wg_hbm, bg_ref, wo_hbm, gp_ref, o_ref, slab_ref,
                wg_ref, wo_ref, sem):
    tm = x_ref.shape[0]
    nb, sub = GROUPS_PER_BLOCK, SUBLANES

    @pl.when(pl.program_id(0) == 0)
    def _():
        half = slab_ref.shape[0] // 2
        for w_hbm, w_ref in ((wo_hbm, wo_ref), (wg_hbm, wg_ref)):
            _load_weight_as_bf16(
                w_hbm, w_ref, lambda slot, a=w_hbm.shape[0]: slab_ref.at[pl.ds(slot * half, a)], sem, LANES)

    o = _dot(yc_ref[...], wo_ref[:D_CONV])
    ys, pre = [], bg_ref[...]
    slabs_per_block = COL_BLOCK // LANES
    for blk in range(D_SSM // COL_BLOCK):
        for gv in range(blk * slabs_per_block, (blk + 1) * slabs_per_block):
            rows = _lane_block_transpose([yg_ref[gv * nb + gl] for gl in range(nb)])
            for tl in range(sub):
                slab_ref[gv, pl.ds(tl, tm // sub, stride=sub), :] = rows[tl]
        y = jax.nn.gelu(jnp.concatenate(
            [slab_ref[gv] for gv in range(blk * slabs_per_block, (blk + 1) * slabs_per_block)], axis=1))
        ys.append(y)
        pre = pre + _dot(y.astype(_BF16), wg_ref[blk * COL_BLOCK:(blk + 1) * COL_BLOCK, :])
    y = jnp.concatenate(ys, axis=1)
    ys = (y * _sigmoid(pre) * zs_ref[...].astype(_F32)).astype(_BF16)
    o = o + _dot(ys, wo_ref[D_CONV:])
    ms = jnp.mean(o * o, axis=-1, keepdims=True)
    o_ref[...] = x_ref[...] + o * lax.rsqrt(ms + EPS) * gp_ref[...]


def _out_proj(yc, yg, zs, x2d, w_glu, b_glu, w_out, norm_g):
    n_tok = x2d.shape[0]
    tm = TOKEN_TILE
    const = lambda i: (0, 0)
    tile = lambda w: pl.BlockSpec((tm, w), lambda i: (i, 0))
    return pl.pallas_call(
        _out_kernel,
        grid=(n_tok // tm,),
        in_specs=[
            tile(D_CONV),
            pl.BlockSpec((SSM_GROUPS, tm // SUBLANES, LANES), lambda i: (0, i, 0)),
            tile(D_SSM), tile(D_MODEL),
            pl.BlockSpec(memory_space=pl.ANY),
            pl.BlockSpec((1, D_SSM), const),
            pl.BlockSpec(memory_space=pl.ANY),
            pl.BlockSpec((1, D_MODEL), const),
        ],
        out_specs=tile(D_MODEL),
        out_shape=jax.ShapeDtypeStruct((n_tok, D_MODEL), _F32),
        scratch_shapes=[pltpu.VMEM((D_SSM // LANES, tm, LANES), _F32),
                        pltpu.VMEM((D_SSM, D_SSM), _BF16),
                        pltpu.VMEM((D_CONV + D_SSM, D_MODEL), _BF16),
                        pltpu.SemaphoreType.DMA((2,))],
        compiler_params=pltpu.CompilerParams(dimension_semantics=("arbitrary",),
                                             vmem_limit_bytes=VMEM_LIMIT_BYTES),
        name="glu_outproj",
    )(yc, yg, zs, x2d, w_glu.reshape(D_SSM // tm, tm, D_SSM), b_glu,
      w_out.reshape((D_CONV + D_SSM) // tm, tm, D_MODEL), norm_g)


def kernel(x, norm_pre_g, w_in, conv_w, conv_b, ssm_a_re, ssm_a_im, ssm_log_dt, ssm_b_re, ssm_b_im,
           ssm_c_re, ssm_c_im, ssm_d, w_glu, b_glu, w_out, norm_post_g):
    bsz, seq_len, _ = x.shape
    n_tok = bsz * seq_len
    g, p, h, t = SSM_GROUPS, SSM_STATE, SSM_GROUP, S5_CHUNK
    assert seq_len % TOKEN_TILE == 0 and TOKEN_TILE % t == 0
    x2d = x.reshape(n_tok, D_MODEL)

    yc, ug, zs = _inproj(x2d, norm_pre_g.reshape(1, -1), w_in, conv_w,
                         conv_b.reshape(1, -1), seq_len)

    both = lambda a, b: jnp.concatenate([a, b], axis=-1)
    brt, bit = ssm_b_re.transpose(0, 2, 1), ssm_b_im.transpose(0, 2, 1)
    a_re, a_im = ssm_a_re.reshape(g, 1, p), ssm_a_im.reshape(g, 1, p)
    d_pad = jnp.pad(ssm_d.reshape(g, 1, h), ((0, 0), (0, 0), (0, S5_TH - h)))
    prep = _s5_prep(
        both(a_re, a_re), both(a_im, a_im),
        jnp.broadcast_to(ssm_log_dt.reshape(g, 1, 1), (g, 1, LANES)),
        both(brt, bit), both(bit, brt), both(ssm_c_re, ssm_c_im), both(ssm_c_im, ssm_c_re), d_pad)

    yg = _s5_core(ug, prep, seq_len // t)

    out = _out_proj(yc, yg, zs, x2d, w_glu, b_glu.reshape(1, -1), w_out, norm_post_g.reshape(1, -1))
    return out.reshape(bsz, seq_len, D_MODEL)
```

```python
import functools

import jax
import jax.numpy as jnp
from jax import lax
from jax.experimental import pallas as pl
from jax.experimental.pallas import tpu as pltpu

D_MODEL = 2048
D_CONV = 1024
D_SSM = 1024
SSM_GROUP = 16
SSM_GROUPS = 64
SSM_STATE = 64
N_IN = 4 * D_CONV + 2 * D_SSM
EPS = 1e-6

S5_CHUNK = 32
S5_TH = S5_CHUNK * SSM_GROUP
S5_POW_ROWS = S5_CHUNK + 8
S5_SCAN_PITCH = 24
LANES = 128
SUBLANES = 8
GROUPS_PER_BLOCK = LANES // SSM_GROUP
TOKEN_TILE = 512
COL_BLOCK = 256
OUT_ROW_PARTS = 2
OUT_COL_BLOCK = 512
WEIGHT_STAGE_SLOTS = 4
WEIGHT_STAGE_ROWS = 128
VMEM_LIMIT_BYTES = 56 * 1024 * 1024

_F32 = jnp.float32
_BF16 = jnp.bfloat16


def _dot(a, b):
    return jnp.dot(a, b, preferred_element_type=_F32)


def _dot_nt(a, b, precision=None):
    return lax.dot_general(a, b, (((1,), (1,)), ((), ())), precision=precision,
                           preferred_element_type=_F32)


def _sigmoid(z):
    return 1.0 / (1.0 + jnp.exp(-z))


def _load_weight_as_bf16(w_hbm, w_ref, stage, sem, cols):
    a, r, n = w_hbm.shape

    def chunk(c, slot):
        return pltpu.make_async_copy(w_hbm.at[:, :, pl.ds(c * cols, cols)], stage(slot), sem.at[slot])

    chunk(0, 0).start()
    for c in range(n // cols):
        slot = c % 2
        if c + 1 < n // cols:
            chunk(c + 1, 1 - slot).start()
        chunk(c, slot).wait()
        for k in range(a):
            w_ref[k * r:(k + 1) * r, c * cols:(c + 1) * cols] = stage(slot)[k].astype(_BF16)


def _load_rows_as_bf16(w_hbm, w_ref, stage_ref, sem):
    slots, rb, _ = stage_ref.shape
    rows, n = w_hbm.shape

    def block(c):
        return pltpu.make_async_copy(w_hbm.at[pl.ds(c * rb, rb), :],
                                     stage_ref.at[c % slots, :, pl.ds(0, n)], sem.at[c % slots])

    for c in range(min(slots - 1, rows // rb)):
        block(c).start()
    for c in range(rows // rb):
        if c + slots - 1 < rows // rb:
            block(c + slots - 1).start()
        block(c).wait()
        w_ref[c * rb:(c + 1) * rb, :] = stage_ref[c % slots, :, :n].astype(_BF16)


def _lane_block_transpose(xs):
    nb = len(xs)
    blk = lax.broadcasted_iota(jnp.int32, (1, LANES), 1) // SSM_GROUP
    masks = [blk == k for k in range(nb)]
    moved = []
    for d in range(nb):
        w = xs[d]
        for b in range(1, nb):
            w = jnp.where(masks[b], xs[(b + d) % nb], w)
        moved.append(w if d == 0 else pltpu.roll(w, d * SSM_GROUP, axis=1))
    ys = []
    for b in range(nb):
        y = moved[0]
        for d in range(1, nb):
            y = jnp.where(masks[(b + d) % nb], moved[d], y)
        ys.append(y)
    return ys


def _inproj_kernel(x_ref, g_ref, w_hbm, cw_ref, cb_ref, yc_ref, ug_ref, zs_ref, carry_ref, slab_ref,
                   raw_ref, w_ref, sem, *, tiles_per_seq):
    tm = x_ref.shape[0]
    nb, sub = GROUPS_PER_BLOCK, SUBLANES

    @pl.when(pl.program_id(0) == 0)
    def _():
        _load_weight_as_bf16(w_hbm, w_ref, lambda slot: raw_ref.at[slot, pl.ds(0, w_hbm.shape[0])],
                             sem, COL_BLOCK)

    x = x_ref[...]
    ms = jnp.mean(x * x, axis=-1, keepdims=True)
    h = (x * lax.rsqrt(ms + EPS) * g_ref[...]).astype(_BF16)

    slabs_per_block = COL_BLOCK // LANES
    bases = (0, D_CONV, 2 * D_CONV, 3 * D_CONV, 4 * D_CONV + D_SSM)

    def project_u(blk):
        lo = blk * COL_BLOCK
        u = _dot(h, w_ref[:, 4 * D_CONV + lo:4 * D_CONV + lo + COL_BLOCK])
        for k in range(slabs_per_block):
            slab_ref[blk * slabs_per_block + k] = u[:, k * LANES:(k + 1) * LANES]

    def project_raw(blk, j):
        lo = bases[j] + blk * COL_BLOCK
        raw_ref[blk % 2, j] = _dot(h, w_ref[:, lo:lo + COL_BLOCK])

    def project(blk):
        return ([functools.partial(project_u, blk)]
                + [functools.partial(project_raw, blk, j) for j in range(len(bases))])

    def finish_gate(blk):
        sl = slice(blk * COL_BLOCK, (blk + 1) * COL_BLOCK)
        z_ssm = raw_ref[blk % 2, 4]
        zs_ref[:, sl] = (z_ssm * _sigmoid(z_ssm)).astype(_BF16)

    def finish_u(gv):
        rows = [slab_ref[gv, pl.ds(tl, tm // sub, stride=sub), :] for tl in range(sub)]
        for gl, o in enumerate(_lane_block_transpose(rows)):
            ug_ref[gv * nb + gl] = o

    def finish_conv(blk, half):
        sl = slice(blk * COL_BLOCK + half * LANES, blk * COL_BLOCK + (half + 1) * LANES)
        hl = slice(half * LANES, (half + 1) * LANES)
        b_gate, c_gate, v, z = (raw_ref[blk % 2, j, :, hl] for j in range(4))
        cv = c_gate * v
        ext = jnp.concatenate([carry_ref[:, sl], cv], axis=0)
        conv = (cb_ref[:, sl] + cw_ref[2:3, sl] * cv + cw_ref[1:2, sl] * ext[7:7 + tm]
                + cw_ref[0:1, sl] * ext[6:6 + tm])
        carry_ref[:, sl] = cv[tm - 8:]
        yc_ref[:, sl] = (b_gate * conv * (z * _sigmoid(z))).astype(_BF16)

    def finish(blk):
        return ([functools.partial(finish_gate, blk)]
                + [functools.partial(finish_u, blk * slabs_per_block + k) for k in range(slabs_per_block)]
                + [functools.partial(finish_conv, blk, half) for half in range(COL_BLOCK // LANES)])

    def interleave(mxu_items, vpu_items):
        n = max(len(mxu_items), 1)
        done = 0
        for i, item in enumerate(mxu_items):
            item()
            upto = (len(vpu_items) * (i + 1)) // n
            for piece in vpu_items[done:upto]:
                piece()
            done = upto
        for piece in vpu_items[done:]:
            piece()

    @pl.when(pl.program_id(0) % tiles_per_seq == 0)
    def _():
        carry_ref[...] = jnp.zeros_like(carry_ref)

    n_blocks = D_CONV // COL_BLOCK
    interleave(project(0), [])
    for blk in range(1, n_blocks):
        interleave(project(blk), finish(blk - 1))
    interleave([], finish(n_blocks - 1))


def _inproj(x2d, norm_g, w_in, conv_w, conv_b, seq_len):
    n_tok = x2d.shape[0]
    tm = TOKEN_TILE
    n_tiles = n_tok // tm
    const = lambda s: (0, 0)
    read = write = lambda s: (s, 0)
    out_sds = jax.ShapeDtypeStruct((n_tok, D_CONV), _BF16)
    return pl.pallas_call(
        functools.partial(_inproj_kernel, tiles_per_seq=seq_len // tm),
        grid=(n_tiles,),
        in_specs=[
            pl.BlockSpec((tm, D_MODEL), read),
            pl.BlockSpec((1, D_MODEL), const),
            pl.BlockSpec(memory_space=pl.ANY),
            pl.BlockSpec((3, D_CONV), const),
            pl.BlockSpec((1, D_CONV), const),
        ],
        out_specs=[pl.BlockSpec((tm, D_CONV), write),
                   pl.BlockSpec((SSM_GROUPS, tm // SUBLANES, LANES),
                                lambda s: (0, s, 0)),
                   pl.BlockSpec((tm, D_SSM), write)],
        out_shape=[out_sds,
                   jax.ShapeDtypeStruct((SSM_GROUPS, n_tok // SUBLANES, LANES), _F32),
                   out_sds],
        scratch_shapes=[pltpu.VMEM((8, D_CONV), _F32),
                        pltpu.VMEM((D_SSM // LANES, tm, LANES), _F32),
                        pltpu.VMEM((2, 5, tm, COL_BLOCK), _F32),
                        pltpu.VMEM((D_MODEL, N_IN), _BF16),
                        pltpu.SemaphoreType.DMA((2,))],
        compiler_params=pltpu.CompilerParams(dimension_semantics=("arbitrary",),
                                             vmem_limit_bytes=VMEM_LIMIT_BYTES),
        name="inproj_conv",
    )(x2d, norm_g, w_in.reshape(D_MODEL // tm, tm, N_IN), conv_w, conv_b)


def _complex_powers(ar, ai, exponent, nbits):
    pr = jnp.ones((exponent.shape[0], ar.shape[1]), _F32)
    pi = jnp.zeros_like(pr)
    cr, ci = ar, ai
    for b in range(nbits):
        bit = ((exponent >> b) & 1) == 1
        fr = jnp.where(bit, cr, 1.0)
        fi = jnp.where(bit, ci, 0.0)
        pr, pi = pr * fr - pi * fi, pr * fi + pi * fr
        cr, ci = cr * cr - ci * ci, 2.0 * cr * ci
    return pr, pi


def _re_im_sign():
    lane = lax.broadcasted_iota(jnp.int32, (1, LANES), 1)
    return jnp.where(lane < SSM_STATE, -1.0, 1.0).astype(_F32)


def _s5_prep_kernel(lr_ref, li_ref, ldt_ref, bt1_ref, bt2_ref, ct1_ref, ct2_ref, d_ref,
                    kf_ref, ws_ref, wo_ref, a1_ref, a2_ref):
    t, th, jrows, h = S5_CHUNK, S5_TH, S5_POW_ROWS, SSM_GROUP
    sgn = _re_im_sign()
    nbits = max(1, (jrows - 1).bit_length())
    j = lax.broadcasted_iota(jnp.int32, (jrows, 1), 0)
    diag = (lax.broadcasted_iota(jnp.int32, (h, th), 0) == lax.broadcasted_iota(jnp.int32, (h, th), 1))
    hi = lax.Precision.HIGHEST
    for g in range(lr_ref.shape[0]):
        lr, li = lr_ref[g], li_ref[g]
        dt = jnp.exp(ldt_ref[g])
        mag = jnp.exp(lr * dt)
        ar, ai = mag * jnp.cos(li * dt), mag * jnp.sin(li * dt)
        nr, ni = ar - 1.0, ai
        den = lr * lr + li * li
        qr = (nr * lr + ni * li) / den
        qi = (ni * lr - nr * li) / den
        bt1, bt2 = bt1_ref[g], bt2_ref[g]
        b1 = qr * bt1 + (qi * sgn) * bt2
        b2 = (qr * sgn) * bt2 - qi * bt1
        c1 = ct1_ref[g] * (-sgn)
        c2 = -ct2_ref[g]
        pr, pi = _complex_powers(ar, ai, j, nbits)
        rr, ri = _complex_powers(ar, ai, jnp.maximum(t - 1 - j, 0), nbits)
        cps = (pr[:, None, :] * c1[None] + pi[:, None, :] * c2[None]).reshape(jrows * h, LANES)
        kf_ref[g] = _dot_nt(b1, cps[:th], hi) + jnp.where(diag, d_ref[g], 0.0)
        wo_ref[g] = cps[h:h + th].astype(_BF16)
        ws_ref[g] = (rr[:t, None, :] * b1[None] + ri[:t, None, :] * b2[None]).reshape(th, LANES).astype(_BF16)
        a1_ref[g] = pr[t:t + 1]
        a2_ref[g] = pi[t:t + 1]


def _s5_prep(lr, li, ldt, bt1, bt2, ct1, ct2, d_pad):
    g, h, th, nb = SSM_GROUPS, SSM_GROUP, S5_TH, GROUPS_PER_BLOCK
    row = pl.BlockSpec((nb, 1, LANES), lambda i: (i, 0, 0))
    hp = pl.BlockSpec((nb, h, LANES), lambda i: (i, 0, 0))
    thp = pl.BlockSpec((nb, th, LANES), lambda i: (i, 0, 0))
    return pl.pallas_call(
        _s5_prep_kernel,
        grid=(g // nb,),
        in_specs=[row, row, row, hp, hp, hp, hp, pl.BlockSpec((nb, 1, th), lambda i: (i, 0, 0))],
        out_specs=[pl.BlockSpec((nb, h, th), lambda i: (i, 0, 0)), thp, thp, row, row],
        out_shape=[jax.ShapeDtypeStruct((g, h, th), _F32)]
        + [jax.ShapeDtypeStruct((g, th, LANES), _BF16)] * 2
        + [jax.ShapeDtypeStruct((g, 1, LANES), _F32)] * 2,
        compiler_params=pltpu.CompilerParams(dimension_semantics=("arbitrary",)),
        name="s5_prep",
    )(lr, li, ldt, bt1, bt2, ct1, ct2, d_pad)


def _s5_kernel(ug_ref, kf_ref, ws_ref, wo_ref, a1_ref, a2_ref, yg_ref,
               mt_ref, xg_ref, yi_ref, sl_ref, slsw_ref, sin_ref, *, chunks_per_seq):
    nb = ug_ref.shape[0]
    t, th, h = S5_CHUNK, S5_TH, SSM_GROUP
    t_hi_n = t // SUBLANES
    n_chunks = ug_ref.shape[1] // t_hi_n
    n_seq = n_chunks // chunks_per_seq
    pitch = S5_SCAN_PITCH

    for g in range(nb):
        xg = jnp.concatenate([ug_ref[g, pl.ds(k, n_chunks, stride=t_hi_n), :]
                              for k in range(t_hi_n)], axis=1).astype(_BF16)
        xg_ref[g] = xg
        s = _dot(xg, ws_ref[g])
        s_sw = pltpu.roll(s, SSM_STATE, axis=1)
        for b in range(n_seq):
            rows = slice(b * chunks_per_seq, (b + 1) * chunks_per_seq)
            sl_ref[pl.ds(b * nb + g, chunks_per_seq, stride=pitch), :] = s[rows]
            slsw_ref[pl.ds(b * nb + g, chunks_per_seq, stride=pitch), :] = s_sw[rows]

    a1 = jnp.concatenate([a1_ref[g] for g in range(nb)] * n_seq, axis=0)
    a2 = jnp.concatenate([a2_ref[g] for g in range(nb)] * n_seq, axis=0) * _re_im_sign()
    s = jnp.zeros((n_seq * nb, LANES), _F32)
    s_sw = s
    for c in range(chunks_per_seq):
        sin_ref[c * pitch:c * pitch + n_seq * nb, :] = s
        s, s_sw = (a1 * s + a2 * s_sw + sl_ref[c * pitch:c * pitch + n_seq * nb, :],
                   a1 * s_sw - a2 * s + slsw_ref[c * pitch:c * pitch + n_seq * nb, :])

    for g in range(nb):
        zk = jnp.concatenate([jnp.zeros((h, th), _F32), kf_ref[g]], axis=1)
        for k in range(t):
            mt_ref[g, k * h:(k + 1) * h, :] = zk[:, th - k * h:2 * th - k * h].astype(_BF16)
        yi_ref[g] = _dot(xg_ref[g], mt_ref[g])

    for g in range(nb):
        s_in = jnp.concatenate([sin_ref[pl.ds(b * nb + g, chunks_per_seq, stride=pitch), :]
                                for b in range(n_seq)], axis=0).astype(_BF16)
        y = yi_ref[g] + _dot_nt(s_in, wo_ref[g])
        for k in range(t_hi_n):
            yg_ref[g, pl.ds(k, n_chunks, stride=t_hi_n), :] = y[:, k * LANES:(k + 1) * LANES]


def _s5_core(ug, prep, chunks_per_seq):
    _, n_rows, _ = ug.shape
    h, th, nb = SSM_GROUP, S5_TH, GROUPS_PER_BLOCK
    kf, ws, wo, a1, a2 = prep
    rows = pl.BlockSpec((nb, n_rows, LANES), lambda i: (i, 0, 0))
    thp = pl.BlockSpec((nb, th, LANES), lambda i: (i, 0, 0))
    row = pl.BlockSpec((nb, 1, LANES), lambda i: (i, 0, 0))
    return pl.pallas_call(
        functools.partial(_s5_kernel, chunks_per_seq=chunks_per_seq),
        grid=(SSM_GROUPS // nb,),
        in_specs=[rows, pl.BlockSpec((nb, h, th), lambda i: (i, 0, 0)), thp, thp, row, row],
        out_specs=rows,
        out_shape=jax.ShapeDtypeStruct(ug.shape, _F32),
        scratch_shapes=[pltpu.VMEM((nb, th, th), _BF16),
                        pltpu.VMEM((nb, n_rows * SUBLANES // S5_CHUNK, th), _BF16),
                        pltpu.VMEM((nb, n_rows * SUBLANES // S5_CHUNK, th), _F32)]
        + [pltpu.VMEM((chunks_per_seq * S5_SCAN_PITCH, LANES), _F32)] * 3,
        compiler_params=pltpu.CompilerParams(dimension_semantics=("arbitrary",),
                                             vmem_limit_bytes=VMEM_LIMIT_BYTES),
        name="s5_core",
    )(ug, kf, ws, wo, a1, a2)


def _interleave(mxu_items, vpu_items):
    n = max(len(mxu_items), 1)
    done = 0
    for i, item in enumerate(mxu_items):
        item()
        upto = -(-len(vpu_items) * (i + 1) // n)
        for piece in vpu_items[done:upto]:
            piece()
        done = upto
    for piece in vpu_items[done:]:
        piece()


def _out_kernel(yc_ref, yg_ref, zs_ref, x_ref, wg_hbm, bg_ref, wo_hbm, gp_ref, o_ref, slab_ref,
                wg_ref, wo_ref, stage_ref, ys_ref, sem):
    tm = x_ref.shape[0]
    nb, sub = GROUPS_PER_BLOCK, SUBLANES
    rp = tm // OUT_ROW_PARTS
    slabs_per_block = COL_BLOCK // LANES

    @pl.when(pl.program_id(0) == 0)
    def _():
        _load_rows_as_bf16(wo_hbm, wo_ref, stage_ref, sem)
        _load_rows_as_bf16(wg_hbm, wg_ref, stage_ref, sem)

    def activation(part):
        rows = slice(part * rp, (part + 1) * rp)
        grows = slice(part * rp // sub, (part + 1) * rp // sub)
        acc = {"pre": bg_ref[...], "y": []}

        def block(blk):
            gvs = range(blk * slabs_per_block, (blk + 1) * slabs_per_block)
            for gv in gvs:
                tok = _lane_block_transpose([yg_ref[gv * nb + gl, grows, :] for gl in range(nb)])
                for tl in range(sub):
                    slab_ref[gv, pl.ds(part * rp + tl, rp // sub, stride=sub), :] = tok[tl]
            y = jax.nn.gelu(jnp.concatenate([slab_ref[gv, rows, :] for gv in gvs], axis=1))
            acc["y"].append(y)
            acc["pre"] = acc["pre"] + _dot(y.astype(_BF16), wg_ref[blk * COL_BLOCK:(blk + 1) * COL_BLOCK, :])

        def gate():
            y = jnp.concatenate(acc["y"], axis=1)
            ys_ref[part] = (y * _sigmoid(acc["pre"]) * zs_ref[rows, :].astype(_F32)).astype(_BF16)

        return [functools.partial(block, blk) for blk in range(D_SSM // COL_BLOCK)] + [gate]

    def projection(part):
        rows = slice(part * rp, (part + 1) * rp)

        def cols(c):
            sl = slice(c * OUT_COL_BLOCK, (c + 1) * OUT_COL_BLOCK)
            o_ref[rows, sl] = (_dot(yc_ref[rows, :], wo_ref[:D_CONV, sl])
                               + _dot(ys_ref[part], wo_ref[D_CONV:, sl]))

        return [functools.partial(cols, c) for c in range(D_MODEL // OUT_COL_BLOCK)]

    def norm_residual(part):
        rows = slice(part * rp, (part + 1) * rp)
        o = o_ref[rows, :]
        ms = jnp.mean(o * o, axis=-1, keepdims=True)
        o_ref[rows, :] = x_ref[rows, :] + o * lax.rsqrt(ms + EPS) * gp_ref[...]

    _interleave([], activation(0))
    for part in range(OUT_ROW_PARTS):
        beside = activation(part + 1) if part + 1 < OUT_ROW_PARTS else []
        if part > 0:
            beside = [functools.partial(norm_residual, part - 1)] + beside
        _interleave(projection(part), beside)
    norm_residual(OUT_ROW_PARTS - 1)


def _out_proj(yc, yg, zs, x2d, w_glu, b_glu, w_out, norm_g):
    n_tok = x2d.shape[0]
    tm = TOKEN_TILE
    const = lambda i: (0, 0)
    tile = lambda w: pl.BlockSpec((tm, w), lambda i: (i, 0))
    return pl.pallas_call(
        _out_kernel,
        grid=(n_tok // tm,),
        in_specs=[
            tile(D_CONV),
            pl.BlockSpec((SSM_GROUPS, tm // SUBLANES, LANES), lambda i: (0, i, 0)),
            tile(D_SSM), tile(D_MODEL),
            pl.BlockSpec(memory_space=pl.ANY),
            pl.BlockSpec((1, D_SSM), const),
            pl.BlockSpec(memory_space=pl.ANY),
            pl.BlockSpec((1, D_MODEL), const),
        ],
        out_specs=tile(D_MODEL),
        out_shape=jax.ShapeDtypeStruct((n_tok, D_MODEL), _F32),
        scratch_shapes=[pltpu.VMEM((D_SSM // LANES, tm, LANES), _F32),
                        pltpu.VMEM((D_SSM, D_SSM), _BF16),
                        pltpu.VMEM((D_CONV + D_SSM, D_MODEL), _BF16),
                        pltpu.VMEM((WEIGHT_STAGE_SLOTS, WEIGHT_STAGE_ROWS, D_MODEL), _F32),
                        pltpu.VMEM((OUT_ROW_PARTS, tm // OUT_ROW_PARTS, D_SSM), _BF16),
                        pltpu.SemaphoreType.DMA((WEIGHT_STAGE_SLOTS,))],
        compiler_params=pltpu.CompilerParams(dimension_semantics=("arbitrary",),
                                             vmem_limit_bytes=VMEM_LIMIT_BYTES),
        name="glu_outproj",
    )(yc, yg, zs, x2d, w_glu, b_glu, w_out, norm_g)


def kernel(x, norm_pre_g, w_in, conv_w, conv_b, ssm_a_re, ssm_a_im, ssm_log_dt, ssm_b_re, ssm_b_im,
           ssm_c_re, ssm_c_im, ssm_d, w_glu, b_glu, w_out, norm_post_g):
    bsz, seq_len, _ = x.shape
    n_tok = bsz * seq_len
    g, p, h, t = SSM_GROUPS, SSM_STATE, SSM_GROUP, S5_CHUNK
    assert seq_len % TOKEN_TILE == 0 and TOKEN_TILE % t == 0
    x2d = x.reshape(n_tok, D_MODEL)

    yc, ug, zs = _inproj(x2d, norm_pre_g.reshape(1, -1), w_in, conv_w,
                         conv_b.reshape(1, -1), seq_len)

    both = lambda a, b: jnp.concatenate([a, b], axis=-1)
    brt, bit = ssm_b_re.transpose(0, 2, 1), ssm_b_im.transpose(0, 2, 1)
    a_re, a_im = ssm_a_re.reshape(g, 1, p), ssm_a_im.reshape(g, 1, p)
    d_pad = jnp.pad(ssm_d.reshape(g, 1, h), ((0, 0), (0, 0), (0, S5_TH - h)))
    prep = _s5_prep(
        both(a_re, a_re), both(a_im, a_im),
        jnp.broadcast_to(ssm_log_dt.reshape(g, 1, 1), (g, 1, LANES)),
        both(brt, bit), both(bit, brt), both(ssm_c_re, ssm_c_im), both(ssm_c_im, ssm_c_re), d_pad)

    yg = _s5_core(ug, prep, seq_len // t)

    out = _out_proj(yc, yg, zs, x2d, w_glu, b_glu.reshape(1, -1), w_out, norm_post_g.reshape(1, -1))
    return out.reshape(bsz, seq_len, D_MODEL)
```

```python
import functools

import jax
import jax.numpy as jnp
from jax import lax
from jax.experimental import pallas as pl
from jax.experimental.pallas import tpu as pltpu

D_MODEL = 2048
D_CONV = 1024
D_SSM = 1024
SSM_GROUP = 16
SSM_GROUPS = 64
SSM_STATE = 64
N_IN = 4 * D_CONV + 2 * D_SSM
EPS = 1e-6

S5_CHUNK = 32
S5_TH = S5_CHUNK * SSM_GROUP
S5_POW_ROWS = S5_CHUNK + 8
S5_SCAN_PITCH = 24
LANES = 128
SUBLANES = 8
GROUPS_PER_BLOCK = LANES // SSM_GROUP
TOKEN_TILE = 512
COL_BLOCK = 256
OUT_ROW_PARTS = 2
OUT_COL_BLOCK = 512
WEIGHT_STAGE_SLOTS = 4
WEIGHT_STAGE_ROWS = 128
VMEM_LIMIT_BYTES = 56 * 1024 * 1024

_F32 = jnp.float32
_BF16 = jnp.bfloat16


def _dot(a, b):
    return jnp.dot(a, b, preferred_element_type=_F32)


def _dot_nt(a, b, precision=None):
    return lax.dot_general(a, b, (((1,), (1,)), ((), ())), precision=precision,
                           preferred_element_type=_F32)


def _sigmoid(z):
    return 1.0 / (1.0 + jnp.exp(-z))


def _load_weight_as_bf16(w_hbm, w_ref, stage, sem, cols):
    a, r, n = w_hbm.shape

    def chunk(c, slot):
        return pltpu.make_async_copy(w_hbm.at[:, :, pl.ds(c * cols, cols)], stage(slot), sem.at[slot])

    chunk(0, 0).start()
    for c in range(n // cols):
        slot = c % 2
        if c + 1 < n // cols:
            chunk(c + 1, 1 - slot).start()
        chunk(c, slot).wait()
        for k in range(a):
            w_ref[k * r:(k + 1) * r, c * cols:(c + 1) * cols] = stage(slot)[k].astype(_BF16)


def _load_rows_as_bf16(w_hbm, w_ref, stage_ref, sem):
    slots, rb, _ = stage_ref.shape
    rows, n = w_hbm.shape

    def block(c):
        return pltpu.make_async_copy(w_hbm.at[pl.ds(c * rb, rb), :],
                                     stage_ref.at[c % slots, :, pl.ds(0, n)], sem.at[c % slots])

    for c in range(min(slots - 1, rows // rb)):
        block(c).start()
    for c in range(rows // rb):
        if c + slots - 1 < rows // rb:
            block(c + slots - 1).start()
        block(c).wait()
        w_ref[c * rb:(c + 1) * rb, :] = stage_ref[c % slots, :, :n].astype(_BF16)


def _lane_block_transpose(xs):
    nb = len(xs)
    diag = _diagonal_gather(xs)
    moved = [w if d == 0 else pltpu.roll(w, d * SSM_GROUP, axis=1) for d, w in enumerate(diag)]
    back = _diagonal_gather(moved)
    return [back[(-b) % nb] for b in range(nb)]


def _diagonal_gather(arrs):
    assert len(arrs) == 8
    c = lax.broadcasted_iota(jnp.int32, (1, LANES), 1) // SSM_GROUP
    c0, c1, c2 = c & 1, (c >> 1) & 1, (c >> 2) & 1
    s1 = {(k0, hi): jnp.where((c0 ^ k0) == 1, arrs[2 * hi + 1], arrs[2 * hi])
          for k0 in range(2) for hi in range(4)}
    s2 = {}
    for k0 in range(2):
        for k1 in range(2):
            e1 = c1 ^ k1 ^ (c0 & k0)
            for top in range(2):
                s2[(k0, k1, top)] = jnp.where(e1 == 1, s1[(k0, 2 * top + 1)], s1[(k0, 2 * top)])
    out = []
    for k in range(8):
        k0, k1, k2 = k & 1, (k >> 1) & 1, (k >> 2) & 1
        carry0 = c0 & k0
        carry1 = (c1 & k1) | (c1 & carry0) | (k1 & carry0)
        e2 = c2 ^ k2 ^ carry1
        out.append(jnp.where(e2 == 1, s2[(k0, k1, 1)], s2[(k0, k1, 0)]))
    return out


def _inproj_kernel(x_ref, g_ref, w_hbm, cw_ref, cb_ref, yc_ref, ug_ref, zs_ref, carry_ref, slab_ref,
                   raw_ref, w_ref, sem, *, tiles_per_seq):
    tm = x_ref.shape[0]
    nb, sub = GROUPS_PER_BLOCK, SUBLANES

    @pl.when(pl.program_id(0) == 0)
    def _():
        _load_weight_as_bf16(w_hbm, w_ref, lambda slot: raw_ref.at[slot, pl.ds(0, w_hbm.shape[0])],
                             sem, COL_BLOCK)

    x = x_ref[...]
    ms = jnp.mean(x * x, axis=-1, keepdims=True)
    h = (x * lax.rsqrt(ms + EPS) * g_ref[...]).astype(_BF16)

    slabs_per_block = COL_BLOCK // LANES
    bases = (0, D_CONV, 2 * D_CONV, 3 * D_CONV, 4 * D_CONV + D_SSM)

    def project_u(blk):
        lo = blk * COL_BLOCK
        u = _dot(h, w_ref[:, 4 * D_CONV + lo:4 * D_CONV + lo + COL_BLOCK])
        for k in range(slabs_per_block):
            slab_ref[blk * slabs_per_block + k] = u[:, k * LANES:(k + 1) * LANES]

    def project_raw(blk, j):
        lo = bases[j] + blk * COL_BLOCK
        raw_ref[blk % 2, j] = _dot(h, w_ref[:, lo:lo + COL_BLOCK])

    def project(blk):
        return ([functools.partial(project_u, blk)]
                + [functools.partial(project_raw, blk, j) for j in range(len(bases))])

    def finish_gate(blk):
        sl = slice(blk * COL_BLOCK, (blk + 1) * COL_BLOCK)
        z_ssm = raw_ref[blk % 2, 4]
        zs_ref[:, sl] = (z_ssm * _sigmoid(z_ssm)).astype(_BF16)

    def finish_u(gv):
        rows = [slab_ref[gv, pl.ds(tl, tm // sub, stride=sub), :] for tl in range(sub)]
        for gl, o in enumerate(_lane_block_transpose(rows)):
            ug_ref[gv * nb + gl] = o

    def finish_conv(blk, half):
        sl = slice(blk * COL_BLOCK + half * LANES, blk * COL_BLOCK + (half + 1) * LANES)
        hl = slice(half * LANES, (half + 1) * LANES)
        b_gate, c_gate, v, z = (raw_ref[blk % 2, j, :, hl] for j in range(4))
        cv = c_gate * v
        ext = jnp.concatenate([carry_ref[:, sl], cv], axis=0)
        conv = (cb_ref[:, sl] + cw_ref[2:3, sl] * cv + cw_ref[1:2, sl] * ext[7:7 + tm]
                + cw_ref[0:1, sl] * ext[6:6 + tm])
        carry_ref[:, sl] = cv[tm - 8:]
        yc_ref[:, sl] = (b_gate * conv * (z * _sigmoid(z))).astype(_BF16)

    def finish(blk):
        return ([functools.partial(finish_gate, blk)]
                + [functools.partial(finish_u, blk * slabs_per_block + k) for k in range(slabs_per_block)]
                + [functools.partial(finish_conv, blk, half) for half in range(COL_BLOCK // LANES)])

    def interleave(mxu_items, vpu_items):
        n = max(len(mxu_items), 1)
        done = 0
        for i, item in enumerate(mxu_items):
            item()
            upto = (len(vpu_items) * (i + 1)) // n
            for piece in vpu_items[done:upto]:
                piece()
            done = upto
        for piece in vpu_items[done:]:
            piece()

    @pl.when(pl.program_id(0) % tiles_per_seq == 0)
    def _():
        carry_ref[...] = jnp.zeros_like(carry_ref)

    n_blocks = D_CONV // COL_BLOCK
    interleave(project(0), [])
    for blk in range(1, n_blocks):
        interleave(project(blk), finish(blk - 1))
    interleave([], finish(n_blocks - 1))


def _inproj(x2d, norm_g, w_in, conv_w, conv_b, seq_len):
    n_tok = x2d.shape[0]
    tm = TOKEN_TILE
    n_tiles = n_tok // tm
    const = lambda s: (0, 0)
    read = write = lambda s: (s, 0)
    out_sds = jax.ShapeDtypeStruct((n_tok, D_CONV), _BF16)
    return pl.pallas_call(
        functools.partial(_inproj_kernel, tiles_per_seq=seq_len // tm),
        grid=(n_tiles,),
        in_specs=[
            pl.BlockSpec((tm, D_MODEL), read),
            pl.BlockSpec((1, D_MODEL), const),
            pl.BlockSpec(memory_space=pl.ANY),
            pl.BlockSpec((3, D_CONV), const),
            pl.BlockSpec((1, D_CONV), const),
        ],
        out_specs=[pl.BlockSpec((tm, D_CONV), write),
                   pl.BlockSpec((SSM_GROUPS, tm // SUBLANES, LANES),
                                lambda s: (0, s, 0)),
                   pl.BlockSpec((tm, D_SSM), write)],
        out_shape=[out_sds,
                   jax.ShapeDtypeStruct((SSM_GROUPS, n_tok // SUBLANES, LANES), _F32),
                   out_sds],
        scratch_shapes=[pltpu.VMEM((8, D_CONV), _F32),
                        pltpu.VMEM((D_SSM // LANES, tm, LANES), _F32),
                        pltpu.VMEM((2, 5, tm, COL_BLOCK), _F32),
                        pltpu.VMEM((D_MODEL, N_IN), _BF16),
                        pltpu.SemaphoreType.DMA((2,))],
        compiler_params=pltpu.CompilerParams(dimension_semantics=("arbitrary",),
                                             vmem_limit_bytes=VMEM_LIMIT_BYTES),
        name="inproj_conv",
    )(x2d, norm_g, w_in.reshape(D_MODEL // tm, tm, N_IN), conv_w, conv_b)


def _complex_powers(ar, ai, exponent, nbits):
    pr = jnp.ones((exponent.shape[0], ar.shape[1]), _F32)
    pi = jnp.zeros_like(pr)
    cr, ci = ar, ai
    for b in range(nbits):
        bit = ((exponent >> b) & 1) == 1
        fr = jnp.where(bit, cr, 1.0)
        fi = jnp.where(bit, ci, 0.0)
        pr, pi = pr * fr - pi * fi, pr * fi + pi * fr
        cr, ci = cr * cr - ci * ci, 2.0 * cr * ci
    return pr, pi


def _re_im_sign():
    lane = lax.broadcasted_iota(jnp.int32, (1, LANES), 1)
    return jnp.where(lane < SSM_STATE, -1.0, 1.0).astype(_F32)


def _s5_prep_kernel(lr_ref, li_ref, ldt_ref, bt1_ref, bt2_ref, ct1_ref, ct2_ref, d_ref,
                    kf_ref, ws_ref, wo_ref, a1_ref, a2_ref):
    t, th, jrows, h = S5_CHUNK, S5_TH, S5_POW_ROWS, SSM_GROUP
    sgn = _re_im_sign()
    nbits = max(1, (jrows - 1).bit_length())
    j = lax.broadcasted_iota(jnp.int32, (jrows, 1), 0)
    diag = (lax.broadcasted_iota(jnp.int32, (h, th), 0) == lax.broadcasted_iota(jnp.int32, (h, th), 1))
    hi = lax.Precision.HIGHEST
    for g in range(lr_ref.shape[0]):
        lr, li = lr_ref[g], li_ref[g]
        dt = jnp.exp(ldt_ref[g])
        mag = jnp.exp(lr * dt)
        ar, ai = mag * jnp.cos(li * dt), mag * jnp.sin(li * dt)
        nr, ni = ar - 1.0, ai
        den = lr * lr + li * li
        qr = (nr * lr + ni * li) / den
        qi = (ni * lr - nr * li) / den
        bt1, bt2 = bt1_ref[g], bt2_ref[g]
        b1 = qr * bt1 + (qi * sgn) * bt2
        b2 = (qr * sgn) * bt2 - qi * bt1
        c1 = ct1_ref[g] * (-sgn)
        c2 = -ct2_ref[g]
        pr, pi = _complex_powers(ar, ai, j, nbits)
        rr, ri = _complex_powers(ar, ai, jnp.maximum(t - 1 - j, 0), nbits)
        cps = (pr[:, None, :] * c1[None] + pi[:, None, :] * c2[None]).reshape(jrows * h, LANES)
        kf_ref[g] = _dot_nt(b1, cps[:th], hi) + jnp.where(diag, d_ref[g], 0.0)
        wo_ref[g] = cps[h:h + th].astype(_BF16)
        ws_ref[g] = (rr[:t, None, :] * b1[None] + ri[:t, None, :] * b2[None]).reshape(th, LANES).astype(_BF16)
        a1_ref[g] = pr[t:t + 1]
        a2_ref[g] = pi[t:t + 1]


def _s5_prep(lr, li, ldt, bt1, bt2, ct1, ct2, d_pad):
    g, h, th, nb = SSM_GROUPS, SSM_GROUP, S5_TH, GROUPS_PER_BLOCK
    row = pl.BlockSpec((nb, 1, LANES), lambda i: (i, 0, 0))
    hp = pl.BlockSpec((nb, h, LANES), lambda i: (i, 0, 0))
    thp = pl.BlockSpec((nb, th, LANES), lambda i: (i, 0, 0))
    return pl.pallas_call(
        _s5_prep_kernel,
        grid=(g // nb,),
        in_specs=[row, row, row, hp, hp, hp, hp, pl.BlockSpec((nb, 1, th), lambda i: (i, 0, 0))],
        out_specs=[pl.BlockSpec((nb, h, th), lambda i: (i, 0, 0)), thp, thp, row, row],
        out_shape=[jax.ShapeDtypeStruct((g, h, th), _F32)]
        + [jax.ShapeDtypeStruct((g, th, LANES), _BF16)] * 2
        + [jax.ShapeDtypeStruct((g, 1, LANES), _F32)] * 2,
        compiler_params=pltpu.CompilerParams(dimension_semantics=("arbitrary",)),
        name="s5_prep",
    )(lr, li, ldt, bt1, bt2, ct1, ct2, d_pad)


def _s5_kernel(ug_ref, kf_ref, ws_ref, wo_ref, a1_ref, a2_ref, yg_ref,
               mt_ref, xg_ref, yi_ref, sl_ref, slsw_ref, sin_ref, *, chunks_per_seq):
    nb = ug_ref.shape[0]
    t, th, h = S5_CHUNK, S5_TH, SSM_GROUP
    t_hi_n = t // SUBLANES
    n_chunks = ug_ref.shape[1] // t_hi_n
    n_seq = n_chunks // chunks_per_seq
    pitch = S5_SCAN_PITCH

    for g in range(nb):
        xg = jnp.concatenate([ug_ref[g, pl.ds(k, n_chunks, stride=t_hi_n), :]
                              for k in range(t_hi_n)], axis=1).astype(_BF16)
        xg_ref[g] = xg
        s = _dot(xg, ws_ref[g])
        s_sw = pltpu.roll(s, SSM_STATE, axis=1)
        for b in range(n_seq):
            rows = slice(b * chunks_per_seq, (b + 1) * chunks_per_seq)
            sl_ref[pl.ds(b * nb + g, chunks_per_seq, stride=pitch), :] = s[rows]
            slsw_ref[pl.ds(b * nb + g, chunks_per_seq, stride=pitch), :] = s_sw[rows]

    a1 = jnp.concatenate([a1_ref[g] for g in range(nb)] * n_seq, axis=0)
    a2 = jnp.concatenate([a2_ref[g] for g in range(nb)] * n_seq, axis=0) * _re_im_sign()
    s = jnp.zeros((n_seq * nb, LANES), _F32)
    s_sw = s
    for c in range(chunks_per_seq):
        sin_ref[c * pitch:c * pitch + n_seq * nb, :] = s
        s, s_sw = (a1 * s + a2 * s_sw + sl_ref[c * pitch:c * pitch + n_seq * nb, :],
                   a1 * s_sw - a2 * s + slsw_ref[c * pitch:c * pitch + n_seq * nb, :])

    for g in range(nb):
        zk = jnp.concatenate([jnp.zeros((h, th), _F32), kf_ref[g]], axis=1)
        for k in range(t):
            mt_ref[g, k * h:(k + 1) * h, :] = zk[:, th - k * h:2 * th - k * h].astype(_BF16)
        yi_ref[g] = _dot(xg_ref[g], mt_ref[g])

    for g in range(nb):
        s_in = jnp.concatenate([sin_ref[pl.ds(b * nb + g, chunks_per_seq, stride=pitch), :]
                                for b in range(n_seq)], axis=0).astype(_BF16)
        y = yi_ref[g] + _dot_nt(s_in, wo_ref[g])
        for k in range(t_hi_n):
            yg_ref[g, pl.ds(k, n_chunks, stride=t_hi_n), :] = y[:, k * LANES:(k + 1) * LANES]


def _s5_core(ug, prep, chunks_per_seq):
    _, n_rows, _ = ug.shape
    h, th, nb = SSM_GROUP, S5_TH, GROUPS_PER_BLOCK
    kf, ws, wo, a1, a2 = prep
    rows = pl.BlockSpec((nb, n_rows, LANES), lambda i: (i, 0, 0))
    thp = pl.BlockSpec((nb, th, LANES), lambda i: (i, 0, 0))
    row = pl.BlockSpec((nb, 1, LANES), lambda i: (i, 0, 0))
    return pl.pallas_call(
        functools.partial(_s5_kernel, chunks_per_seq=chunks_per_seq),
        grid=(SSM_GROUPS // nb,),
        in_specs=[rows, pl.BlockSpec((nb, h, th), lambda i: (i, 0, 0)), thp, thp, row, row],
        out_specs=rows,
        out_shape=jax.ShapeDtypeStruct(ug.shape, _F32),
        scratch_shapes=[pltpu.VMEM((nb, th, th), _BF16),
                        pltpu.VMEM((nb, n_rows * SUBLANES // S5_CHUNK, th), _BF16),
                        pltpu.VMEM((nb, n_rows * SUBLANES // S5_CHUNK, th), _F32)]
        + [pltpu.VMEM((chunks_per_seq * S5_SCAN_PITCH, LANES), _F32)] * 3,
        compiler_params=pltpu.CompilerParams(dimension_semantics=("arbitrary",),
                                             vmem_limit_bytes=VMEM_LIMIT_BYTES),
        name="s5_core",
    )(ug, kf, ws, wo, a1, a2)


def _interleave(mxu_items, vpu_items):
    n = max(len(mxu_items), 1)
    done = 0
    for i, item in enumerate(mxu_items):
        item()
        upto = -(-len(vpu_items) * (i + 1) // n)
        for piece in vpu_items[done:upto]:
            piece()
        done = upto
    for piece in vpu_items[done:]:
        piece()


def _out_kernel(yc_ref, yg_ref, zs_ref, x_ref, wg_hbm, bg_ref, wo_hbm, gp_ref, o_ref, slab_ref,
                wg_ref, wo_ref, stage_ref, ys_ref, sem):
    tm = x_ref.shape[0]
    nb, sub = GROUPS_PER_BLOCK, SUBLANES
    rp = tm // OUT_ROW_PARTS
    slabs_per_block = COL_BLOCK // LANES

    @pl.when(pl.program_id(0) == 0)
    def _():
        _load_rows_as_bf16(wo_hbm, wo_ref, stage_ref, sem)
        _load_rows_as_bf16(wg_hbm, wg_ref, stage_ref, sem)

    def activation(part):
        rows = slice(part * rp, (part + 1) * rp)
        grows = slice(part * rp // sub, (part + 1) * rp // sub)
        acc = {"pre": bg_ref[...], "y": []}

        def block(blk):
            gvs = range(blk * slabs_per_block, (blk + 1) * slabs_per_block)
            for gv in gvs:
                tok = _lane_block_transpose([yg_ref[gv * nb + gl, grows, :] for gl in range(nb)])
                for tl in range(sub):
                    slab_ref[gv, pl.ds(part * rp + tl, rp // sub, stride=sub), :] = tok[tl]
            y = jax.nn.gelu(jnp.concatenate([slab_ref[gv, rows, :] for gv in gvs], axis=1))
            acc["y"].append(y)
            acc["pre"] = acc["pre"] + _dot(y.astype(_BF16), wg_ref[blk * COL_BLOCK:(blk + 1) * COL_BLOCK, :])

        def gate():
            y = jnp.concatenate(acc["y"], axis=1)
            ys_ref[part] = (y * _sigmoid(acc["pre"]) * zs_ref[rows, :].astype(_F32)).astype(_BF16)

        return [functools.partial(block, blk) for blk in range(D_SSM // COL_BLOCK)] + [gate]

    def projection(part):
        rows = slice(part * rp, (part + 1) * rp)

        def cols(c):
            sl = slice(c * OUT_COL_BLOCK, (c + 1) * OUT_COL_BLOCK)
            o_ref[rows, sl] = (_dot(yc_ref[rows, :], wo_ref[:D_CONV, sl])
                               + _dot(ys_ref[part], wo_ref[D_CONV:, sl]))

        return [functools.partial(cols, c) for c in range(D_MODEL // OUT_COL_BLOCK)]

    def norm_residual(part):
        rows = slice(part * rp, (part + 1) * rp)
        o = o_ref[rows, :]
        ms = jnp.mean(o * o, axis=-1, keepdims=True)
        o_ref[rows, :] = x_ref[rows, :] + o * lax.rsqrt(ms + EPS) * gp_ref[...]

    _interleave([], activation(0))
    for part in range(OUT_ROW_PARTS):
        beside = activation(part + 1) if part + 1 < OUT_ROW_PARTS else []
        if part > 0:
            beside = [functools.partial(norm_residual, part - 1)] + beside
        _interleave(projection(part), beside)
    norm_residual(OUT_ROW_PARTS - 1)


def _out_proj(yc, yg, zs, x2d, w_glu, b_glu, w_out, norm_g):
    n_tok = x2d.shape[0]
    tm = TOKEN_TILE
    const = lambda i: (0, 0)
    tile = lambda w: pl.BlockSpec((tm, w), lambda i: (i, 0))
    return pl.pallas_call(
        _out_kernel,
        grid=(n_tok // tm,),
        in_specs=[
            tile(D_CONV),
            pl.BlockSpec((SSM_GROUPS, tm // SUBLANES, LANES), lambda i: (0, i, 0)),
            tile(D_SSM), tile(D_MODEL),
            pl.BlockSpec(memory_space=pl.ANY),
            pl.BlockSpec((1, D_SSM), const),
            pl.BlockSpec(memory_space=pl.ANY),
            pl.BlockSpec((1, D_MODEL), const),
        ],
        out_specs=tile(D_MODEL),
        out_shape=jax.ShapeDtypeStruct((n_tok, D_MODEL), _F32),
        scratch_shapes=[pltpu.VMEM((D_SSM // LANES, tm, LANES), _F32),
                        pltpu.VMEM((D_SSM, D_SSM), _BF16),
                        pltpu.VMEM((D_CONV + D_SSM, D_MODEL), _BF16),
                        pltpu.VMEM((WEIGHT_STAGE_SLOTS, WEIGHT_STAGE_ROWS, D_MODEL), _F32),
                        pltpu.VMEM((OUT_ROW_PARTS, tm // OUT_ROW_PARTS, D_SSM), _BF16),
                        pltpu.SemaphoreType.DMA((WEIGHT_STAGE_SLOTS,))],
        compiler_params=pltpu.CompilerParams(dimension_semantics=("arbitrary",),
                                             vmem_limit_bytes=VMEM_LIMIT_BYTES),
        name="glu_outproj",
    )(yc, yg, zs, x2d, w_glu, b_glu, w_out, norm_g)


def kernel(x, norm_pre_g, w_in, conv_w, conv_b, ssm_a_re, ssm_a_im, ssm_log_dt, ssm_b_re, ssm_b_im,
           ssm_c_re, ssm_c_im, ssm_d, w_glu, b_glu, w_out, norm_post_g):
    bsz, seq_len, _ = x.shape
    n_tok = bsz * seq_len
    g, p, h, t = SSM_GROUPS, SSM_STATE, SSM_GROUP, S5_CHUNK
    assert seq_len % TOKEN_TILE == 0 and TOKEN_TILE % t == 0
    x2d = x.reshape(n_tok, D_MODEL)

    yc, ug, zs = _inproj(x2d, norm_pre_g.reshape(1, -1), w_in, conv_w,
                         conv_b.reshape(1, -1), seq_len)

    both = lambda a, b: jnp.concatenate([a, b], axis=-1)
    brt, bit = ssm_b_re.transpose(0, 2, 1), ssm_b_im.transpose(0, 2, 1)
    a_re, a_im = ssm_a_re.reshape(g, 1, p), ssm_a_im.reshape(g, 1, p)
    d_pad = jnp.pad(ssm_d.reshape(g, 1, h), ((0, 0), (0, 0), (0, S5_TH - h)))
    prep = _s5_prep(
        both(a_re, a_re), both(a_im, a_im),
        jnp.broadcast_to(ssm_log_dt.reshape(g, 1, 1), (g, 1, LANES)),
        both(brt, bit), both(bit, brt), both(ssm_c_re, ssm_c_im), both(ssm_c_im, ssm_c_re), d_pad)

    yg = _s5_core(ug, prep, seq_len // t)

    out = _out_proj(yc, yg, zs, x2d, w_glu, b_glu.reshape(1, -1), w_out, norm_post_g.reshape(1, -1))
    return out.reshape(bsz, seq_len, D_MODEL)
```

```python
import functools

import jax
import jax.numpy as jnp
from jax import lax
from jax.experimental import pallas as pl
from jax.experimental.pallas import tpu as pltpu

D_MODEL = 2048
D_CONV = 1024
D_SSM = 1024
SSM_GROUP = 16
SSM_GROUPS = 64
SSM_STATE = 64
N_IN = 4 * D_CONV + 2 * D_SSM
EPS = 1e-6

S5_CHUNK = 32
S5_TH = S5_CHUNK * SSM_GROUP
S5_POW_ROWS = S5_CHUNK + 8
S5_SCAN_PITCH = 24
LANES = 128
SUBLANES = 8
GROUPS_PER_BLOCK = LANES // SSM_GROUP
TOKEN_TILE = 512
COL_BLOCK = 256
OUT_ROW_PARTS = 2
OUT_COL_BLOCK = 512
WEIGHT_STAGE_SLOTS = 4
WEIGHT_STAGE_ROWS = 128
VMEM_LIMIT_BYTES = 56 * 1024 * 1024

_F32 = jnp.float32
_BF16 = jnp.bfloat16


def _dot(a, b):
    return jnp.dot(a, b, preferred_element_type=_F32)


def _dot_nt(a, b):
    return lax.dot_general(a, b, (((1,), (1,)), ((), ())), preferred_element_type=_F32)


def _sigmoid(z):
    return 1.0 / (1.0 + jnp.exp(-z))


def _interleave(mxu_items, vpu_items):
    n = max(len(mxu_items), 1)
    done = 0
    for i, item in enumerate(mxu_items):
        item()
        upto = -(-len(vpu_items) * (i + 1) // n)
        for piece in vpu_items[done:upto]:
            piece()
        done = upto
    for piece in vpu_items[done:]:
        piece()


def _load_weight_as_bf16(w_hbm, w_ref, stage, sem, cols):
    a, r, n = w_hbm.shape

    def chunk(c, slot):
        return pltpu.make_async_copy(w_hbm.at[:, :, pl.ds(c * cols, cols)], stage(slot), sem.at[slot])

    chunk(0, 0).start()
    for c in range(n // cols):
        slot = c % 2
        if c + 1 < n // cols:
            chunk(c + 1, 1 - slot).start()
        chunk(c, slot).wait()
        for k in range(a):
            w_ref[k * r:(k + 1) * r, c * cols:(c + 1) * cols] = stage(slot)[k].astype(_BF16)


def _load_rows_as_bf16(w_hbm, w_ref, stage_ref, sem):
    slots, rb, _ = stage_ref.shape
    rows, n = w_hbm.shape

    def block(c):
        return pltpu.make_async_copy(w_hbm.at[pl.ds(c * rb, rb), :],
                                     stage_ref.at[c % slots, :, pl.ds(0, n)], sem.at[c % slots])

    for c in range(min(slots - 1, rows // rb)):
        block(c).start()
    for c in range(rows // rb):
        if c + slots - 1 < rows // rb:
            block(c + slots - 1).start()
        block(c).wait()
        w_ref[c * rb:(c + 1) * rb, :] = stage_ref[c % slots, :, :n].astype(_BF16)


def _lane_block_transpose(xs):
    nb = len(xs)
    diag = _diagonal_gather(xs)
    moved = [w if d == 0 else pltpu.roll(w, d * SSM_GROUP, axis=1) for d, w in enumerate(diag)]
    back = _diagonal_gather(moved)
    return [back[(-b) % nb] for b in range(nb)]


def _diagonal_gather(arrs):
    assert len(arrs) == 8
    c = lax.broadcasted_iota(jnp.int32, (1, LANES), 1) // SSM_GROUP
    c0, c1, c2 = c & 1, (c >> 1) & 1, (c >> 2) & 1
    s1 = {(k0, hi): jnp.where((c0 ^ k0) == 1, arrs[2 * hi + 1], arrs[2 * hi])
          for k0 in range(2) for hi in range(4)}
    s2 = {}
    for k0 in range(2):
        for k1 in range(2):
            e1 = c1 ^ k1 ^ (c0 & k0)
            for top in range(2):
                s2[(k0, k1, top)] = jnp.where(e1 == 1, s1[(k0, 2 * top + 1)], s1[(k0, 2 * top)])
    out = []
    for k in range(8):
        k0, k1, k2 = k & 1, (k >> 1) & 1, (k >> 2) & 1
        carry0 = c0 & k0
        carry1 = (c1 & k1) | (c1 & carry0) | (k1 & carry0)
        e2 = c2 ^ k2 ^ carry1
        out.append(jnp.where(e2 == 1, s2[(k0, k1, 1)], s2[(k0, k1, 0)]))
    return out


def _inproj_kernel(x_ref, g_ref, w_hbm, cw_ref, cb_ref, yc_ref, ug_ref, zs_ref, carry_ref, slab_ref,
                   raw_ref, w_ref, sem, *, tiles_per_seq):
    tm = x_ref.shape[0]
    nb, sub = GROUPS_PER_BLOCK, SUBLANES

    @pl.when(pl.program_id(0) == 0)
    def _():
        _load_weight_as_bf16(w_hbm, w_ref, lambda slot: raw_ref.at[slot, pl.ds(0, w_hbm.shape[0])],
                             sem, COL_BLOCK)

    x = x_ref[...]
    ms = jnp.mean(x * x, axis=-1, keepdims=True)
    h = (x * lax.rsqrt(ms + EPS) * g_ref[...]).astype(_BF16)

    slabs_per_block = COL_BLOCK // LANES
    bases = (0, D_CONV, 2 * D_CONV, 3 * D_CONV, 4 * D_CONV + D_SSM)

    def project_u(blk):
        lo = blk * COL_BLOCK
        u = _dot(h, w_ref[:, 4 * D_CONV + lo:4 * D_CONV + lo + COL_BLOCK])
        for k in range(slabs_per_block):
            slab_ref[blk * slabs_per_block + k] = u[:, k * LANES:(k + 1) * LANES]

    def project_raw(blk, j):
        lo = bases[j] + blk * COL_BLOCK
        raw_ref[blk % 2, j] = _dot(h, w_ref[:, lo:lo + COL_BLOCK])

    def project(blk):
        return ([functools.partial(project_u, blk)]
                + [functools.partial(project_raw, blk, j) for j in range(len(bases))])

    def finish_gate(blk):
        sl = slice(blk * COL_BLOCK, (blk + 1) * COL_BLOCK)
        z_ssm = raw_ref[blk % 2, 4]
        zs_ref[:, sl] = (z_ssm * _sigmoid(z_ssm)).astype(_BF16)

    def finish_u(gv):
        rows = [slab_ref[gv, pl.ds(tl, tm // sub, stride=sub), :] for tl in range(sub)]
        for gl, o in enumerate(_lane_block_transpose(rows)):
            ug_ref[gv * nb + gl] = o

    def finish_conv(blk, half):
        sl = slice(blk * COL_BLOCK + half * LANES, blk * COL_BLOCK + (half + 1) * LANES)
        hl = slice(half * LANES, (half + 1) * LANES)
        b_gate, c_gate, v, z = (raw_ref[blk % 2, j, :, hl] for j in range(4))
        cv = c_gate * v
        ext = jnp.concatenate([carry_ref[:, sl], cv], axis=0)
        conv = (cb_ref[:, sl] + cw_ref[2:3, sl] * cv + cw_ref[1:2, sl] * ext[7:7 + tm]
                + cw_ref[0:1, sl] * ext[6:6 + tm])
        carry_ref[:, sl] = cv[tm - 8:]
        yc_ref[:, sl] = (b_gate * conv * (z * _sigmoid(z))).astype(_BF16)

    def finish(blk):
        return ([functools.partial(finish_gate, blk)]
                + [functools.partial(finish_u, blk * slabs_per_block + k) for k in range(slabs_per_block)]
                + [functools.partial(finish_conv, blk, half) for half in range(COL_BLOCK // LANES)])

    @pl.when(pl.program_id(0) % tiles_per_seq == 0)
    def _():
        carry_ref[...] = jnp.zeros_like(carry_ref)

    n_blocks = D_CONV // COL_BLOCK
    _interleave(project(0), [])
    for blk in range(1, n_blocks):
        _interleave(project(blk), finish(blk - 1))
    _interleave([], finish(n_blocks - 1))


def _inproj(x2d, norm_g, w_in, conv_w, conv_b, seq_len):
    n_tok = x2d.shape[0]
    tm = TOKEN_TILE
    const = lambda s: (0, 0)
    tile = lambda w: pl.BlockSpec((tm, w), lambda s: (s, 0))
    out_sds = jax.ShapeDtypeStruct((n_tok, D_CONV), _BF16)
    return pl.pallas_call(
        functools.partial(_inproj_kernel, tiles_per_seq=seq_len // tm),
        grid=(n_tok // tm,),
        in_specs=[
            tile(D_MODEL),
            pl.BlockSpec((1, D_MODEL), const),
            pl.BlockSpec(memory_space=pl.ANY),
            pl.BlockSpec((3, D_CONV), const),
            pl.BlockSpec((1, D_CONV), const),
        ],
        out_specs=[tile(D_CONV),
                   pl.BlockSpec((SSM_GROUPS, tm // SUBLANES, LANES), lambda s: (0, s, 0)),
                   tile(D_SSM)],
        out_shape=[out_sds,
                   jax.ShapeDtypeStruct((SSM_GROUPS, n_tok // SUBLANES, LANES), _F32),
                   out_sds],
        scratch_shapes=[pltpu.VMEM((8, D_CONV), _F32),
                        pltpu.VMEM((D_SSM // LANES, tm, LANES), _F32),
                        pltpu.VMEM((2, 5, tm, COL_BLOCK), _F32),
                        pltpu.VMEM((D_MODEL, N_IN), _BF16),
                        pltpu.SemaphoreType.DMA((2,))],
        compiler_params=pltpu.CompilerParams(dimension_semantics=("arbitrary",),
                                             vmem_limit_bytes=VMEM_LIMIT_BYTES),
        name="inproj_conv",
    )(x2d, norm_g, w_in.reshape(D_MODEL // tm, tm, N_IN), conv_w, conv_b)


def _complex_powers(ar, ai, exponent, nbits):
    pr = jnp.ones((exponent.shape[0], ar.shape[1]), _F32)
    pi = jnp.zeros_like(pr)
    cr, ci = ar, ai
    for b in range(nbits):
        bit = ((exponent >> b) & 1) == 1
        fr = jnp.where(bit, cr, 1.0)
        fi = jnp.where(bit, ci, 0.0)
        pr, pi = pr * fr - pi * fi, pr * fi + pi * fr
        cr, ci = cr * cr - ci * ci, 2.0 * cr * ci
    return pr, pi


def _re_im_sign():
    lane = lax.broadcasted_iota(jnp.int32, (1, LANES), 1)
    return jnp.where(lane < SSM_STATE, -1.0, 1.0).astype(_F32)


def _s5_prepare_group(g, rows_ref, mats_ref, kf_ref, ws_ref, wo_ref, a1_ref, a2_ref):
    t, th, jrows, h = S5_CHUNK, S5_TH, S5_POW_ROWS, SSM_GROUP
    sgn = _re_im_sign()
    nbits = max(1, (jrows - 1).bit_length())
    j = lax.broadcasted_iota(jnp.int32, (jrows, 1), 0)
    diag = (lax.broadcasted_iota(jnp.int32, (h, th), 0) == lax.broadcasted_iota(jnp.int32, (h, th), 1))
    lr, li, ldt, d_row = (rows_ref[g, :, 0:LANES], rows_ref[g, :, LANES:2 * LANES],
                          rows_ref[g, :, 2 * LANES:3 * LANES], rows_ref[g, :, 3 * LANES:])
    bt1, bt2, ct1, ct2 = (mats_ref[g, :, k * LANES:(k + 1) * LANES] for k in range(4))
    dt = jnp.exp(ldt)
    mag = jnp.exp(lr * dt)
    ar, ai = mag * jnp.cos(li * dt), mag * jnp.sin(li * dt)
    nr, ni = ar - 1.0, ai
    den = lr * lr + li * li
    qr = (nr * lr + ni * li) / den
    qi = (ni * lr - nr * li) / den
    b1 = qr * bt1 + (qi * sgn) * bt2
    b2 = (qr * sgn) * bt2 - qi * bt1
    c1 = ct1 * (-sgn)
    c2 = -ct2
    pr, pi = _complex_powers(ar, ai, j, nbits)
    rr, ri = _complex_powers(ar, ai, jnp.maximum(t - 1 - j, 0), nbits)
    cps = (pr[:, None, :] * c1[None] + pi[:, None, :] * c2[None]).reshape(jrows * h, LANES)
    cps = cps.astype(_BF16)
    b_hi = b1.astype(_BF16)
    b_lo = (b1 - b_hi.astype(_F32)).astype(_BF16)
    kf_ref[g] = _dot_nt(b_hi, cps[:th]) + _dot_nt(b_lo, cps[:th]) + jnp.where(diag, d_row, 0.0)
    wo_ref[g] = cps[h:h + th]
    ws_ref[g] = (rr[:t, None, :] * b1[None] + ri[:t, None, :] * b2[None]).reshape(th, LANES).astype(_BF16)
    a1_ref[g] = pr[t:t + 1]
    a2_ref[g] = pi[t:t + 1]


def _s5_kernel(ug_ref, rows_ref, mats_ref, yg_ref, kf_ref, ws_ref, wo_ref, a1_ref, a2_ref,
               mt_ref, xg_ref, yi_ref, sl_ref, slsw_ref, sin_ref, *, chunks_per_seq):
    nb = ug_ref.shape[0]
    t, th, h = S5_CHUNK, S5_TH, SSM_GROUP
    t_hi_n = t // SUBLANES
    n_chunks = ug_ref.shape[1] // t_hi_n
    n_seq = n_chunks // chunks_per_seq
    pitch = S5_SCAN_PITCH

    for g in range(nb):
        _s5_prepare_group(g, rows_ref, mats_ref, kf_ref, ws_ref, wo_ref, a1_ref, a2_ref)
        xg = jnp.concatenate([ug_ref[g, pl.ds(k, n_chunks, stride=t_hi_n), :]
                              for k in range(t_hi_n)], axis=1).astype(_BF16)
        xg_ref[g] = xg
        s = _dot(xg, ws_ref[g])
        s_sw = pltpu.roll(s, SSM_STATE, axis=1)
        for b in range(n_seq):
            rows = slice(b * chunks_per_seq, (b + 1) * chunks_per_seq)
            sl_ref[pl.ds(b * nb + g, chunks_per_seq, stride=pitch), :] = s[rows]
            slsw_ref[pl.ds(b * nb + g, chunks_per_seq, stride=pitch), :] = s_sw[rows]

    a1 = jnp.concatenate([a1_ref[g] for g in range(nb)] * n_seq, axis=0)
    a2 = jnp.concatenate([a2_ref[g] for g in range(nb)] * n_seq, axis=0) * _re_im_sign()
    s = jnp.zeros((n_seq * nb, LANES), _F32)
    s_sw = s
    for c in range(chunks_per_seq):
        sin_ref[c * pitch:c * pitch + n_seq * nb, :] = s
        s, s_sw = (a1 * s + a2 * s_sw + sl_ref[c * pitch:c * pitch + n_seq * nb, :],
                   a1 * s_sw - a2 * s + slsw_ref[c * pitch:c * pitch + n_seq * nb, :])

    for g in range(nb):
        zk = jnp.concatenate([jnp.zeros((h, th), _F32), kf_ref[g]], axis=1)
        for k in range(t):
            mt_ref[g, k * h:(k + 1) * h, :] = zk[:, th - k * h:2 * th - k * h].astype(_BF16)
        yi_ref[g] = _dot(xg_ref[g], mt_ref[g])

    for g in range(nb):
        s_in = jnp.concatenate([sin_ref[pl.ds(b * nb + g, chunks_per_seq, stride=pitch), :]
                                for b in range(n_seq)], axis=0).astype(_BF16)
        y = yi_ref[g] + _dot_nt(s_in, wo_ref[g])
        for k in range(t_hi_n):
            yg_ref[g, pl.ds(k, n_chunks, stride=t_hi_n), :] = y[:, k * LANES:(k + 1) * LANES]


def _s5_core(ug, rows, mats, chunks_per_seq):
    _, n_rows, _ = ug.shape
    h, th, nb = SSM_GROUP, S5_TH, GROUPS_PER_BLOCK
    group_rows = pl.BlockSpec((nb, n_rows, LANES), lambda i: (i, 0, 0))
    return pl.pallas_call(
        functools.partial(_s5_kernel, chunks_per_seq=chunks_per_seq),
        grid=(SSM_GROUPS // nb,),
        in_specs=[group_rows,
                  pl.BlockSpec((nb, 1, rows.shape[-1]), lambda i: (i, 0, 0)),
                  pl.BlockSpec((nb, h, mats.shape[-1]), lambda i: (i, 0, 0))],
        out_specs=group_rows,
        out_shape=jax.ShapeDtypeStruct(ug.shape, _F32),
        scratch_shapes=[pltpu.VMEM((nb, h, th), _F32),
                        pltpu.VMEM((nb, th, LANES), _BF16),
                        pltpu.VMEM((nb, th, LANES), _BF16),
                        pltpu.VMEM((nb, 1, LANES), _F32),
                        pltpu.VMEM((nb, 1, LANES), _F32),
                        pltpu.VMEM((nb, th, th), _BF16),
                        pltpu.VMEM((nb, n_rows * SUBLANES // S5_CHUNK, th), _BF16),
                        pltpu.VMEM((nb, n_rows * SUBLANES // S5_CHUNK, th), _F32)]
        + [pltpu.VMEM((chunks_per_seq * S5_SCAN_PITCH, LANES), _F32)] * 3,
        compiler_params=pltpu.CompilerParams(dimension_semantics=("arbitrary",),
                                             vmem_limit_bytes=VMEM_LIMIT_BYTES),
        name="s5_core",
    )(ug, rows, mats)


def _out_kernel(yc_ref, yg_ref, zs_ref, x_ref, wg_hbm, bg_ref, wo_hbm, gp_ref, o_ref, slab_ref,
                wg_ref, wo_ref, stage_ref, ys_ref, sem):
    tm = x_ref.shape[0]
    nb, sub = GROUPS_PER_BLOCK, SUBLANES
    rp = tm // OUT_ROW_PARTS
    slabs_per_block = COL_BLOCK // LANES

    @pl.when(pl.program_id(0) == 0)
    def _():
        _load_rows_as_bf16(wo_hbm, wo_ref, stage_ref, sem)
        _load_rows_as_bf16(wg_hbm, wg_ref, stage_ref, sem)

    def activation(part):
        rows = slice(part * rp, (part + 1) * rp)
        grows = slice(part * rp // sub, (part + 1) * rp // sub)
        acc = {"pre": bg_ref[...], "y": []}

        def block(blk):
            gvs = range(blk * slabs_per_block, (blk + 1) * slabs_per_block)
            for gv in gvs:
                tok = _lane_block_transpose([yg_ref[gv * nb + gl, grows, :] for gl in range(nb)])
                for tl in range(sub):
                    slab_ref[gv, pl.ds(part * rp + tl, rp // sub, stride=sub), :] = tok[tl]
            y = jax.nn.gelu(jnp.concatenate([slab_ref[gv, rows, :] for gv in gvs], axis=1))
            acc["y"].append(y)
            acc["pre"] = acc["pre"] + _dot(y.astype(_BF16), wg_ref[blk * COL_BLOCK:(blk + 1) * COL_BLOCK, :])

        def gate():
            y = jnp.concatenate(acc["y"], axis=1)
            ys_ref[part] = (y * _sigmoid(acc["pre"]) * zs_ref[rows, :].astype(_F32)).astype(_BF16)

        return [functools.partial(block, blk) for blk in range(D_SSM // COL_BLOCK)] + [gate]

    def projection(part):
        rows = slice(part * rp, (part + 1) * rp)

        def cols(c):
            sl = slice(c * OUT_COL_BLOCK, (c + 1) * OUT_COL_BLOCK)
            o_ref[rows, sl] = (_dot(yc_ref[rows, :], wo_ref[:D_CONV, sl])
                               + _dot(ys_ref[part], wo_ref[D_CONV:, sl]))

        return [functools.partial(cols, c) for c in range(D_MODEL // OUT_COL_BLOCK)]

    def norm_residual(part):
        rows = slice(part * rp, (part + 1) * rp)
        o = o_ref[rows, :]
        ms = jnp.mean(o * o, axis=-1, keepdims=True)
        o_ref[rows, :] = x_ref[rows, :] + o * lax.rsqrt(ms + EPS) * gp_ref[...]

    _interleave([], activation(0))
    for part in range(OUT_ROW_PARTS):
        beside = activation(part + 1) if part + 1 < OUT_ROW_PARTS else []
        if part > 0:
            beside = [functools.partial(norm_residual, part - 1)] + beside
        _interleave(projection(part), beside)
    norm_residual(OUT_ROW_PARTS - 1)


def _out_proj(yc, yg, zs, x2d, w_glu, b_glu, w_out, norm_g):
    n_tok = x2d.shape[0]
    tm = TOKEN_TILE
    const = lambda i: (0, 0)
    tile = lambda w: pl.BlockSpec((tm, w), lambda i: (i, 0))
    return pl.pallas_call(
        _out_kernel,
        grid=(n_tok // tm,),
        in_specs=[
            tile(D_CONV),
            pl.BlockSpec((SSM_GROUPS, tm // SUBLANES, LANES), lambda i: (0, i, 0)),
            tile(D_SSM), tile(D_MODEL),
            pl.BlockSpec(memory_space=pl.ANY),
            pl.BlockSpec((1, D_SSM), const),
            pl.BlockSpec(memory_space=pl.ANY),
            pl.BlockSpec((1, D_MODEL), const),
        ],
        out_specs=tile(D_MODEL),
        out_shape=jax.ShapeDtypeStruct((n_tok, D_MODEL), _F32),
        scratch_shapes=[pltpu.VMEM((D_SSM // LANES, tm, LANES), _F32),
                        pltpu.VMEM((D_SSM, D_SSM), _BF16),
                        pltpu.VMEM((D_CONV + D_SSM, D_MODEL), _BF16),
                        pltpu.VMEM((WEIGHT_STAGE_SLOTS, WEIGHT_STAGE_ROWS, D_MODEL), _F32),
                        pltpu.VMEM((OUT_ROW_PARTS, tm // OUT_ROW_PARTS, D_SSM), _BF16),
                        pltpu.SemaphoreType.DMA((WEIGHT_STAGE_SLOTS,))],
        compiler_params=pltpu.CompilerParams(dimension_semantics=("arbitrary",),
                                             vmem_limit_bytes=VMEM_LIMIT_BYTES),
        name="glu_outproj",
    )(yc, yg, zs, x2d, w_glu, b_glu, w_out, norm_g)


def kernel(x, norm_pre_g, w_in, conv_w, conv_b, ssm_a_re, ssm_a_im, ssm_log_dt, ssm_b_re, ssm_b_im,
           ssm_c_re, ssm_c_im, ssm_d, w_glu, b_glu, w_out, norm_post_g):
    bsz, seq_len, _ = x.shape
    n_tok = bsz * seq_len
    g, p, h, t = SSM_GROUPS, SSM_STATE, SSM_GROUP, S5_CHUNK
    assert seq_len % TOKEN_TILE == 0 and TOKEN_TILE % t == 0
    x2d = x.reshape(n_tok, D_MODEL)

    yc, ug, zs = _inproj(x2d, norm_pre_g.reshape(1, -1), w_in, conv_w, conv_b.reshape(1, -1), seq_len)

    brt, bit = ssm_b_re.transpose(0, 2, 1), ssm_b_im.transpose(0, 2, 1)
    a_re, a_im = ssm_a_re.reshape(g, 1, p), ssm_a_im.reshape(g, 1, p)
    rows = jnp.concatenate(
        [a_re, a_re, a_im, a_im, jnp.broadcast_to(ssm_log_dt.reshape(g, 1, 1), (g, 1, LANES)),
         ssm_d.reshape(g, 1, h), jnp.zeros((g, 1, S5_TH - h), _F32)], axis=-1)
    mats = jnp.concatenate([brt, bit, bit, brt, ssm_c_re, ssm_c_im, ssm_c_im, ssm_c_re], axis=-1)
    yg = _s5_core(ug, rows, mats, seq_len // t)

    out = _out_proj(yc, yg, zs, x2d, w_glu, b_glu.reshape(1, -1), w_out, norm_post_g.reshape(1, -1))
    return out.reshape(bsz, seq_len, D_MODEL)
```

```python
import functools

import jax
import jax.numpy as jnp
from jax import lax
from jax.experimental import pallas as pl
from jax.experimental.pallas import tpu as pltpu

D_MODEL = 2048
D_CONV = 1024
D_SSM = 1024
SSM_GROUP = 16
SSM_GROUPS = 64
SSM_STATE = 64
N_IN = 4 * D_CONV + 2 * D_SSM
EPS = 1e-6

S5_CHUNK = 32
S5_TH = S5_CHUNK * SSM_GROUP
S5_POW_ROWS = S5_CHUNK + 8
S5_SCAN_PITCH = 24
LANES = 128
SUBLANES = 8
GROUPS_PER_BLOCK = LANES // SSM_GROUP
TOKEN_TILE = 512
COL_BLOCK = 256
OUT_ROW_PARTS = 2
OUT_COL_BLOCK = 512
WEIGHT_STAGE_SLOTS = 4
WEIGHT_STAGE_ROWS = 128
VMEM_LIMIT_BYTES = 56 * 1024 * 1024

_F32 = jnp.float32
_BF16 = jnp.bfloat16


def _dot(a, b):
    return jnp.dot(a, b, preferred_element_type=_F32)


def _dot_nt(a, b):
    return lax.dot_general(a, b, (((1,), (1,)), ((), ())), preferred_element_type=_F32)


def _sigmoid(z):
    return 1.0 / (1.0 + jnp.exp(-z))


def _interleave(mxu_items, vpu_items):
    n = max(len(mxu_items), 1)
    done = 0
    for i, item in enumerate(mxu_items):
        item()
        upto = -(-len(vpu_items) * (i + 1) // n)
        for piece in vpu_items[done:upto]:
            piece()
        done = upto
    for piece in vpu_items[done:]:
        piece()


def _load_weight_as_bf16(w_hbm, w_ref, stage, sem, cols):
    a, r, n = w_hbm.shape

    def chunk(c, slot):
        return pltpu.make_async_copy(w_hbm.at[:, :, pl.ds(c * cols, cols)], stage(slot), sem.at[slot])

    chunk(0, 0).start()
    for c in range(n // cols):
        slot = c % 2
        if c + 1 < n // cols:
            chunk(c + 1, 1 - slot).start()
        chunk(c, slot).wait()
        for k in range(a):
            w_ref[k * r:(k + 1) * r, c * cols:(c + 1) * cols] = stage(slot)[k].astype(_BF16)


def _load_rows_as_bf16(w_hbm, w_ref, stage_ref, sem):
    slots, rb, _ = stage_ref.shape
    rows, n = w_hbm.shape

    def block(c):
        return pltpu.make_async_copy(w_hbm.at[pl.ds(c * rb, rb), :],
                                     stage_ref.at[c % slots, :, pl.ds(0, n)], sem.at[c % slots])

    for c in range(min(slots - 1, rows // rb)):
        block(c).start()
    for c in range(rows // rb):
        if c + slots - 1 < rows // rb:
            block(c + slots - 1).start()
        block(c).wait()
        w_ref[c * rb:(c + 1) * rb, :] = stage_ref[c % slots, :, :n].astype(_BF16)


def _lane_block_transpose(xs):
    nb = len(xs)
    diag = _diagonal_gather(xs)
    moved = [w if d == 0 else pltpu.roll(w, d * SSM_GROUP, axis=1) for d, w in enumerate(diag)]
    back = _diagonal_gather(moved)
    return [back[(-b) % nb] for b in range(nb)]


def _diagonal_gather(arrs):
    assert len(arrs) == 8
    c = lax.broadcasted_iota(jnp.int32, (1, LANES), 1) // SSM_GROUP
    c0, c1, c2 = c & 1, (c >> 1) & 1, (c >> 2) & 1
    s1 = {(k0, hi): jnp.where((c0 ^ k0) == 1, arrs[2 * hi + 1], arrs[2 * hi])
          for k0 in range(2) for hi in range(4)}
    s2 = {}
    for k0 in range(2):
        for k1 in range(2):
            e1 = c1 ^ k1 ^ (c0 & k0)
            for top in range(2):
                s2[(k0, k1, top)] = jnp.where(e1 == 1, s1[(k0, 2 * top + 1)], s1[(k0, 2 * top)])
    out = []
    for k in range(8):
        k0, k1, k2 = k & 1, (k >> 1) & 1, (k >> 2) & 1
        carry0 = c0 & k0
        carry1 = (c1 & k1) | (c1 & carry0) | (k1 & carry0)
        e2 = c2 ^ k2 ^ carry1
        out.append(jnp.where(e2 == 1, s2[(k0, k1, 1)], s2[(k0, k1, 0)]))
    return out


def _inproj_kernel(x_ref, g_ref, w_hbm, cw_ref, cb_ref, yc_ref, ug_ref, zs_ref, carry_ref, slab_ref,
                   raw_ref, w_ref, sem, *, tiles_per_seq):
    tm = x_ref.shape[0]
    nb, sub = GROUPS_PER_BLOCK, SUBLANES
    slabs_per_block = COL_BLOCK // LANES
    n_blocks = D_CONV // COL_BLOCK
    bases = (4 * D_CONV, 0, D_CONV, 2 * D_CONV, 3 * D_CONV, 4 * D_CONV + D_SSM)

    @pl.when(pl.program_id(0) == 0)
    def _():
        _load_weight_as_bf16(w_hbm, w_ref, lambda slot: raw_ref.at[slot, pl.ds(0, w_hbm.shape[0])],
                             sem, COL_BLOCK)

    @pl.when(pl.program_id(0) % tiles_per_seq == 0)
    def _():
        carry_ref[...] = jnp.zeros_like(carry_ref)

    x = x_ref[...]
    ms = jnp.mean(x * x, axis=-1, keepdims=True)
    h = (x * lax.rsqrt(ms + EPS) * g_ref[...]).astype(_BF16)

    def project_one(blk, j):
        lo = bases[j] + blk * COL_BLOCK
        res = _dot(h, w_ref[:, lo:lo + COL_BLOCK])
        if j == 0:
            for k in range(slabs_per_block):
                slab_ref[blk * slabs_per_block + k] = res[:, k * LANES:(k + 1) * LANES]
        else:
            raw_ref[blk % 2, j - 1] = res

    def project(blk):
        return [functools.partial(project_one, blk, j) for j in range(len(bases))]

    def finish_gate(blk):
        sl = slice(blk * COL_BLOCK, (blk + 1) * COL_BLOCK)
        z_ssm = raw_ref[blk % 2, 4]
        zs_ref[:, sl] = (z_ssm * _sigmoid(z_ssm)).astype(_BF16)

    def finish_u(gv):
        rows = [slab_ref[gv, pl.ds(tl, tm // sub, stride=sub), :] for tl in range(sub)]
        for gl, o in enumerate(_lane_block_transpose(rows)):
            ug_ref[gv * nb + gl] = o

    def finish_conv(blk, half):
        sl = slice(blk * COL_BLOCK + half * LANES, blk * COL_BLOCK + (half + 1) * LANES)
        hl = slice(half * LANES, (half + 1) * LANES)
        b_gate, c_gate, v, z = (raw_ref[blk % 2, j, :, hl] for j in range(4))
        cv = c_gate * v
        ext = jnp.concatenate([carry_ref[:, sl], cv], axis=0)
        conv = (cb_ref[:, sl] + cw_ref[2:3, sl] * cv + cw_ref[1:2, sl] * ext[7:7 + tm]
                + cw_ref[0:1, sl] * ext[6:6 + tm])
        carry_ref[:, sl] = cv[tm - 8:]
        yc_ref[:, sl] = (b_gate * conv * (z * _sigmoid(z))).astype(_BF16)

    def finish(blk):
        return ([functools.partial(finish_gate, blk)]
                + [functools.partial(finish_u, blk * slabs_per_block + k) for k in range(slabs_per_block)]
                + [functools.partial(finish_conv, blk, half) for half in range(COL_BLOCK // LANES)])

    _interleave(project(0), [])
    for blk in range(1, n_blocks):
        _interleave(project(blk), finish(blk - 1))
    _interleave([], finish(n_blocks - 1))


def _inproj(x2d, norm_g, w_in, conv_w, conv_b, seq_len):
    n_tok = x2d.shape[0]
    tm = TOKEN_TILE
    const = lambda s: (0, 0)
    tile = lambda w: pl.BlockSpec((tm, w), lambda s: (s, 0))
    out_sds = jax.ShapeDtypeStruct((n_tok, D_CONV), _BF16)
    return pl.pallas_call(
        functools.partial(_inproj_kernel, tiles_per_seq=seq_len // tm),
        grid=(n_tok // tm,),
        in_specs=[
            tile(D_MODEL),
            pl.BlockSpec((1, D_MODEL), const),
            pl.BlockSpec(memory_space=pl.ANY),
            pl.BlockSpec((3, D_CONV), const),
            pl.BlockSpec((1, D_CONV), const),
        ],
        out_specs=[tile(D_CONV),
                   pl.BlockSpec((SSM_GROUPS, tm // SUBLANES, LANES), lambda s: (0, s, 0)),
                   tile(D_SSM)],
        out_shape=[out_sds,
                   jax.ShapeDtypeStruct((SSM_GROUPS, n_tok // SUBLANES, LANES), _F32),
                   out_sds],
        scratch_shapes=[pltpu.VMEM((8, D_CONV), _F32),
                        pltpu.VMEM((D_SSM // LANES, tm, LANES), _F32),
                        pltpu.VMEM((2, 5, tm, COL_BLOCK), _F32),
                        pltpu.VMEM((D_MODEL, N_IN), _BF16),
                        pltpu.SemaphoreType.DMA((2,))],
        compiler_params=pltpu.CompilerParams(dimension_semantics=("arbitrary",),
                                             vmem_limit_bytes=VMEM_LIMIT_BYTES),
        name="inproj_conv",
    )(x2d, norm_g, w_in.reshape(D_MODEL // tm, tm, N_IN), conv_w, conv_b)


def _complex_powers(ar, ai, exponent, nbits):
    pr = jnp.ones((exponent.shape[0], ar.shape[1]), _F32)
    pi = jnp.zeros_like(pr)
    cr, ci = ar, ai
    for b in range(nbits):
        bit = ((exponent >> b) & 1) == 1
        fr = jnp.where(bit, cr, 1.0)
        fi = jnp.where(bit, ci, 0.0)
        pr, pi = pr * fr - pi * fi, pr * fi + pi * fr
        cr, ci = cr * cr - ci * ci, 2.0 * cr * ci
    return pr, pi


def _re_im_sign():
    lane = lax.broadcasted_iota(jnp.int32, (1, LANES), 1)
    return jnp.where(lane < SSM_STATE, -1.0, 1.0).astype(_F32)


def _s5_prepare_group(g, rows_ref, mats_ref, kf_ref, ws_ref, wo_ref, a1_ref, a2_ref):
    t, th, jrows, h = S5_CHUNK, S5_TH, S5_POW_ROWS, SSM_GROUP
    sgn = _re_im_sign()
    nbits = max(1, (jrows - 1).bit_length())
    j = lax.broadcasted_iota(jnp.int32, (jrows, 1), 0)
    diag = (lax.broadcasted_iota(jnp.int32, (h, th), 0) == lax.broadcasted_iota(jnp.int32, (h, th), 1))
    lr, li, ldt, d_row = (rows_ref[g, :, 0:LANES], rows_ref[g, :, LANES:2 * LANES],
                          rows_ref[g, :, 2 * LANES:3 * LANES], rows_ref[g, :, 3 * LANES:])
    bt1, bt2, ct1, ct2 = (mats_ref[g, :, k * LANES:(k + 1) * LANES] for k in range(4))
    dt = jnp.exp(ldt)
    mag = jnp.exp(lr * dt)
    ar, ai = mag * jnp.cos(li * dt), mag * jnp.sin(li * dt)
    nr, ni = ar - 1.0, ai
    den = lr * lr + li * li
    qr = (nr * lr + ni * li) / den
    qi = (ni * lr - nr * li) / den
    b1 = qr * bt1 + (qi * sgn) * bt2
    b2 = (qr * sgn) * bt2 - qi * bt1
    c1 = ct1 * (-sgn)
    c2 = -ct2
    pr, pi = _complex_powers(ar, ai, j, nbits)
    rr, ri = _complex_powers(ar, ai, jnp.maximum(t - 1 - j, 0), nbits)
    cps = (pr[:, None, :] * c1[None] + pi[:, None, :] * c2[None]).reshape(jrows * h, LANES)
    cps = cps.astype(_BF16)
    b_hi = b1.astype(_BF16)
    b_lo = (b1 - b_hi.astype(_F32)).astype(_BF16)
    kf_ref[g] = _dot_nt(b_hi, cps[:th]) + _dot_nt(b_lo, cps[:th]) + jnp.where(diag, d_row, 0.0)
    wo_ref[g] = cps[h:h + th]
    ws_ref[g] = (rr[:t, None, :] * b1[None] + ri[:t, None, :] * b2[None]).reshape(th, LANES).astype(_BF16)
    a1_ref[g] = pr[t:t + 1]
    a2_ref[g] = pi[t:t + 1]


def _s5_kernel(ug_ref, rows_ref, mats_ref, yg_ref, kf_ref, ws_ref, wo_ref, a1_ref, a2_ref,
               mt_ref, xg_ref, yi_ref, sl_ref, slsw_ref, sin_ref, *, chunks_per_seq):
    nb = ug_ref.shape[0]
    t, th, h = S5_CHUNK, S5_TH, SSM_GROUP
    t_hi_n = t // SUBLANES
    n_chunks = ug_ref.shape[1] // t_hi_n
    n_seq = n_chunks // chunks_per_seq
    pitch = S5_SCAN_PITCH

    for g in range(nb):
        _s5_prepare_group(g, rows_ref, mats_ref, kf_ref, ws_ref, wo_ref, a1_ref, a2_ref)
        xg = jnp.concatenate([ug_ref[g, pl.ds(k, n_chunks, stride=t_hi_n), :]
                              for k in range(t_hi_n)], axis=1).astype(_BF16)
        xg_ref[g] = xg
        s = _dot(xg, ws_ref[g])
        s_sw = pltpu.roll(s, SSM_STATE, axis=1)
        for b in range(n_seq):
            rows = slice(b * chunks_per_seq, (b + 1) * chunks_per_seq)
            sl_ref[pl.ds(b * nb + g, chunks_per_seq, stride=pitch), :] = s[rows]
            slsw_ref[pl.ds(b * nb + g, chunks_per_seq, stride=pitch), :] = s_sw[rows]

    a1 = jnp.concatenate([a1_ref[g] for g in range(nb)] * n_seq, axis=0)
    a2 = jnp.concatenate([a2_ref[g] for g in range(nb)] * n_seq, axis=0) * _re_im_sign()
    s = jnp.zeros((n_seq * nb, LANES), _F32)
    s_sw = s
    for c in range(chunks_per_seq):
        sin_ref[c * pitch:c * pitch + n_seq * nb, :] = s
        s, s_sw = (a1 * s + a2 * s_sw + sl_ref[c * pitch:c * pitch + n_seq * nb, :],
                   a1 * s_sw - a2 * s + slsw_ref[c * pitch:c * pitch + n_seq * nb, :])

    for g in range(nb):
        zk = jnp.concatenate([jnp.zeros((h, th), _F32), kf_ref[g]], axis=1)
        for k in range(t):
            mt_ref[g, k * h:(k + 1) * h, :] = zk[:, th - k * h:2 * th - k * h].astype(_BF16)
        yi_ref[g] = _dot(xg_ref[g], mt_ref[g])

    for g in range(nb):
        s_in = jnp.concatenate([sin_ref[pl.ds(b * nb + g, chunks_per_seq, stride=pitch), :]
                                for b in range(n_seq)], axis=0).astype(_BF16)
        y = yi_ref[g] + _dot_nt(s_in, wo_ref[g])
        for k in range(t_hi_n):
            yg_ref[g, pl.ds(k, n_chunks, stride=t_hi_n), :] = y[:, k * LANES:(k + 1) * LANES]


def _s5_core(ug, rows, mats, chunks_per_seq):
    _, n_rows, _ = ug.shape
    h, th, nb = SSM_GROUP, S5_TH, GROUPS_PER_BLOCK
    group_rows = pl.BlockSpec((nb, n_rows, LANES), lambda i: (i, 0, 0))
    return pl.pallas_call(
        functools.partial(_s5_kernel, chunks_per_seq=chunks_per_seq),
        grid=(SSM_GROUPS // nb,),
        in_specs=[group_rows,
                  pl.BlockSpec((nb, 1, rows.shape[-1]), lambda i: (i, 0, 0)),
                  pl.BlockSpec((nb, h, mats.shape[-1]), lambda i: (i, 0, 0))],
        out_specs=group_rows,
        out_shape=jax.ShapeDtypeStruct(ug.shape, _F32),
        scratch_shapes=[pltpu.VMEM((nb, h, th), _F32),
                        pltpu.VMEM((nb, th, LANES), _BF16),
                        pltpu.VMEM((nb, th, LANES), _BF16),
                        pltpu.VMEM((nb, 1, LANES), _F32),
                        pltpu.VMEM((nb, 1, LANES), _F32),
                        pltpu.VMEM((nb, th, th), _BF16),
                        pltpu.VMEM((nb, n_rows * SUBLANES // S5_CHUNK, th), _BF16),
                        pltpu.VMEM((nb, n_rows * SUBLANES // S5_CHUNK, th), _F32)]
        + [pltpu.VMEM((chunks_per_seq * S5_SCAN_PITCH, LANES), _F32)] * 3,
        compiler_params=pltpu.CompilerParams(dimension_semantics=("arbitrary",),
                                             vmem_limit_bytes=VMEM_LIMIT_BYTES),
        name="s5_core",
    )(ug, rows, mats)


def _out_kernel(yc_ref, yg_ref, zs_ref, x_ref, wg_hbm, bg_ref, wo_hbm, gp_ref, o_ref, slab_ref,
                wg_ref, wo_ref, stage_ref, ys_ref, sem):
    tm = x_ref.shape[0]
    nb, sub = GROUPS_PER_BLOCK, SUBLANES
    rp = tm // OUT_ROW_PARTS
    slabs_per_block = COL_BLOCK // LANES

    @pl.when(pl.program_id(0) == 0)
    def _():
        _load_rows_as_bf16(wo_hbm, wo_ref, stage_ref, sem)
        _load_rows_as_bf16(wg_hbm, wg_ref, stage_ref, sem)

    def activation(part):
        rows = slice(part * rp, (part + 1) * rp)
        grows = slice(part * rp // sub, (part + 1) * rp // sub)
        acc = {"pre": bg_ref[...], "y": []}

        def block(blk):
            gvs = range(blk * slabs_per_block, (blk + 1) * slabs_per_block)
            for gv in gvs:
                tok = _lane_block_transpose([yg_ref[gv * nb + gl, grows, :] for gl in range(nb)])
                for tl in range(sub):
                    slab_ref[gv, pl.ds(part * rp + tl, rp // sub, stride=sub), :] = tok[tl]
            y = jax.nn.gelu(jnp.concatenate([slab_ref[gv, rows, :] for gv in gvs], axis=1))
            acc["y"].append(y)
            acc["pre"] = acc["pre"] + _dot(y.astype(_BF16), wg_ref[blk * COL_BLOCK:(blk + 1) * COL_BLOCK, :])

        def gate():
            y = jnp.concatenate(acc["y"], axis=1)
            ys_ref[part] = (y * _sigmoid(acc["pre"]) * zs_ref[rows, :].astype(_F32)).astype(_BF16)

        return [functools.partial(block, blk) for blk in range(D_SSM // COL_BLOCK)] + [gate]

    def conv_projection(part):
        rows = slice(part * rp, (part + 1) * rp)

        def cols(c):
            sl = slice(c * OUT_COL_BLOCK, (c + 1) * OUT_COL_BLOCK)
            o_ref[rows, sl] = _dot(yc_ref[rows, :], wo_ref[:D_CONV, sl])

        return [functools.partial(cols, c) for c in range(D_MODEL // OUT_COL_BLOCK)]

    def ssm_projection(part):
        rows = slice(part * rp, (part + 1) * rp)

        def cols(c):
            sl = slice(c * OUT_COL_BLOCK, (c + 1) * OUT_COL_BLOCK)
            o_ref[rows, sl] += _dot(ys_ref[part], wo_ref[D_CONV:, sl])

        return [functools.partial(cols, c) for c in range(D_MODEL // OUT_COL_BLOCK)]

    def norm_residual(part):
        rows = slice(part * rp, (part + 1) * rp)
        o = o_ref[rows, :]
        ms = jnp.mean(o * o, axis=-1, keepdims=True)
        o_ref[rows, :] = x_ref[rows, :] + o * lax.rsqrt(ms + EPS) * gp_ref[...]

    for part in range(OUT_ROW_PARTS + 1):
        mxu, vpu = [], []
        if part > 0:
            mxu += ssm_projection(part - 1)
        if part > 1:
            vpu.append(functools.partial(norm_residual, part - 2))
        if part < OUT_ROW_PARTS:
            mxu += conv_projection(part)
            vpu += activation(part)
        _interleave(mxu, vpu)
    norm_residual(OUT_ROW_PARTS - 1)


def _out_proj(yc, yg, zs, x2d, w_glu, b_glu, w_out, norm_g):
    n_tok = x2d.shape[0]
    tm = TOKEN_TILE
    const = lambda i: (0, 0)
    tile = lambda w: pl.BlockSpec((tm, w), lambda i: (i, 0))
    return pl.pallas_call(
        _out_kernel,
        grid=(n_tok // tm,),
        in_specs=[
            tile(D_CONV),
            pl.BlockSpec((SSM_GROUPS, tm // SUBLANES, LANES), lambda i: (0, i, 0)),
            tile(D_SSM), tile(D_MODEL),
            pl.BlockSpec(memory_space=pl.ANY),
            pl.BlockSpec((1, D_SSM), const),
            pl.BlockSpec(memory_space=pl.ANY),
            pl.BlockSpec((1, D_MODEL), const),
        ],
        out_specs=tile(D_MODEL),
        out_shape=jax.ShapeDtypeStruct((n_tok, D_MODEL), _F32),
        scratch_shapes=[pltpu.VMEM((D_SSM // LANES, tm, LANES), _F32),
                        pltpu.VMEM((D_SSM, D_SSM), _BF16),
                        pltpu.VMEM((D_CONV + D_SSM, D_MODEL), _BF16),
                        pltpu.VMEM((WEIGHT_STAGE_SLOTS, WEIGHT_STAGE_ROWS, D_MODEL), _F32),
                        pltpu.VMEM((OUT_ROW_PARTS, tm // OUT_ROW_PARTS, D_SSM), _BF16),
                        pltpu.SemaphoreType.DMA((WEIGHT_STAGE_SLOTS,))],
        compiler_params=pltpu.CompilerParams(dimension_semantics=("arbitrary",),
                                             vmem_limit_bytes=VMEM_LIMIT_BYTES),
        name="glu_outproj",
    )(yc, yg, zs, x2d, w_glu, b_glu, w_out, norm_g)


def kernel(x, norm_pre_g, w_in, conv_w, conv_b, ssm_a_re, ssm_a_im, ssm_log_dt, ssm_b_re, ssm_b_im,
           ssm_c_re, ssm_c_im, ssm_d, w_glu, b_glu, w_out, norm_post_g):
    bsz, seq_len, _ = x.shape
    n_tok = bsz * seq_len
    g, p, h, t = SSM_GROUPS, SSM_STATE, SSM_GROUP, S5_CHUNK
    assert seq_len % TOKEN_TILE == 0 and TOKEN_TILE % t == 0
    x2d = x.reshape(n_tok, D_MODEL)

    yc, ug, zs = _inproj(x2d, norm_pre_g.reshape(1, -1), w_in, conv_w, conv_b.reshape(1, -1), seq_len)

    brt, bit = ssm_b_re.transpose(0, 2, 1), ssm_b_im.transpose(0, 2, 1)
    a_re, a_im = ssm_a_re.reshape(g, 1, p), ssm_a_im.reshape(g, 1, p)
    rows = jnp.concatenate(
        [a_re, a_re, a_im, a_im, jnp.broadcast_to(ssm_log_dt.reshape(g, 1, 1), (g, 1, LANES)),
         ssm_d.reshape(g, 1, h), jnp.zeros((g, 1, S5_TH - h), _F32)], axis=-1)
    mats = jnp.concatenate([brt, bit, bit, brt, ssm_c_re, ssm_c_im, ssm_c_im, ssm_c_re], axis=-1)
    yg = _s5_core(ug, rows, mats, seq_len // t)

    out = _out_proj(yc, yg, zs, x2d, w_glu, b_glu.reshape(1, -1), w_out, norm_post_g.reshape(1, -1))
    return out.reshape(bsz, seq_len, D_MODEL)
```

```python
import functools

import jax
import jax.numpy as jnp
from jax import lax
from jax.experimental import pallas as pl
from jax.experimental.pallas import tpu as pltpu

D_MODEL = 2048
D_CONV = 1024
D_SSM = 1024
SSM_GROUP = 16
SSM_GROUPS = 64
SSM_STATE = 64
N_IN = 4 * D_CONV + 2 * D_SSM
EPS = 1e-6

S5_CHUNK = 16
S5_TH = S5_CHUNK * SSM_GROUP
S5_POW_ROWS = S5_CHUNK + 8
S5_SCAN_PITCH = 24
LANES = 128
SUBLANES = 8
GROUPS_PER_BLOCK = LANES // SSM_GROUP
TOKEN_TILE = 512
COL_BLOCK = 256
OUT_ROW_PARTS = 2
OUT_COL_BLOCK = 512
WEIGHT_STAGE_SLOTS = 4
WEIGHT_STAGE_ROWS = 128
VMEM_LIMIT_BYTES = 56 * 1024 * 1024

_F32 = jnp.float32
_BF16 = jnp.bfloat16


def _dot(a, b):
    return jnp.dot(a, b, preferred_element_type=_F32)


def _dot_nt(a, b):
    return lax.dot_general(a, b, (((1,), (1,)), ((), ())), preferred_element_type=_F32)


def _sigmoid(z):
    return 1.0 / (1.0 + jnp.exp(-z))


def _interleave(mxu_items, vpu_items):
    n = max(len(mxu_items), 1)
    done = 0
    for i, item in enumerate(mxu_items):
        item()
        upto = -(-len(vpu_items) * (i + 1) // n)
        for piece in vpu_items[done:upto]:
            piece()
        done = upto
    for piece in vpu_items[done:]:
        piece()


def _load_weight_as_bf16(w_hbm, w_ref, stage, sem, cols):
    a, r, n = w_hbm.shape

    def chunk(c, slot):
        return pltpu.make_async_copy(w_hbm.at[:, :, pl.ds(c * cols, cols)], stage(slot), sem.at[slot])

    chunk(0, 0).start()
    for c in range(n // cols):
        slot = c % 2
        if c + 1 < n // cols:
            chunk(c + 1, 1 - slot).start()
        chunk(c, slot).wait()
        for k in range(a):
            w_ref[k * r:(k + 1) * r, c * cols:(c + 1) * cols] = stage(slot)[k].astype(_BF16)


def _load_rows_as_bf16(w_hbm, w_ref, stage_ref, sem):
    slots, rb, _ = stage_ref.shape
    rows, n = w_hbm.shape

    def block(c):
        return pltpu.make_async_copy(w_hbm.at[pl.ds(c * rb, rb), :],
                                     stage_ref.at[c % slots, :, pl.ds(0, n)], sem.at[c % slots])

    for c in range(min(slots - 1, rows // rb)):
        block(c).start()
    for c in range(rows // rb):
        if c + slots - 1 < rows // rb:
            block(c + slots - 1).start()
        block(c).wait()
        w_ref[c * rb:(c + 1) * rb, :] = stage_ref[c % slots, :, :n].astype(_BF16)


def _lane_block_transpose(xs):
    nb = len(xs)
    diag = _diagonal_gather(xs)
    moved = [w if d == 0 else pltpu.roll(w, d * SSM_GROUP, axis=1) for d, w in enumerate(diag)]
    back = _diagonal_gather(moved)
    return [back[(-b) % nb] for b in range(nb)]


def _diagonal_gather(arrs):
    assert len(arrs) == 8
    c = lax.broadcasted_iota(jnp.int32, (1, LANES), 1) // SSM_GROUP
    c0, c1, c2 = c & 1, (c >> 1) & 1, (c >> 2) & 1
    s1 = {(k0, hi): jnp.where((c0 ^ k0) == 1, arrs[2 * hi + 1], arrs[2 * hi])
          for k0 in range(2) for hi in range(4)}
    s2 = {}
    for k0 in range(2):
        for k1 in range(2):
            e1 = c1 ^ k1 ^ (c0 & k0)
            for top in range(2):
                s2[(k0, k1, top)] = jnp.where(e1 == 1, s1[(k0, 2 * top + 1)], s1[(k0, 2 * top)])
    out = []
    for k in range(8):
        k0, k1, k2 = k & 1, (k >> 1) & 1, (k >> 2) & 1
        carry0 = c0 & k0
        carry1 = (c1 & k1) | (c1 & carry0) | (k1 & carry0)
        e2 = c2 ^ k2 ^ carry1
        out.append(jnp.where(e2 == 1, s2[(k0, k1, 1)], s2[(k0, k1, 0)]))
    return out


def _inproj_kernel(x_ref, g_ref, w_hbm, cw_ref, cb_ref, yc_ref, ug_ref, zs_ref, carry_ref, slab_ref,
                   raw_ref, w_ref, sem, *, tiles_per_seq):
    tm = x_ref.shape[0]
    nb, sub = GROUPS_PER_BLOCK, SUBLANES
    slabs_per_block = COL_BLOCK // LANES
    n_blocks = D_CONV // COL_BLOCK
    bases = (4 * D_CONV, 0, D_CONV, 2 * D_CONV, 3 * D_CONV, 4 * D_CONV + D_SSM)

    @pl.when(pl.program_id(0) == 0)
    def _():
        _load_weight_as_bf16(w_hbm, w_ref, lambda slot: raw_ref.at[slot, pl.ds(0, w_hbm.shape[0])],
                             sem, COL_BLOCK)

    @pl.when(pl.program_id(0) % tiles_per_seq == 0)
    def _():
        carry_ref[...] = jnp.zeros_like(carry_ref)

    x = x_ref[...]
    ms = jnp.mean(x * x, axis=-1, keepdims=True)
    h = (x * lax.rsqrt(ms + EPS) * g_ref[...]).astype(_BF16)

    def project_one(blk, j):
        lo = bases[j] + blk * COL_BLOCK
        res = _dot(h, w_ref[:, lo:lo + COL_BLOCK])
        if j == 0:
            for k in range(slabs_per_block):
                slab_ref[blk * slabs_per_block + k] = res[:, k * LANES:(k + 1) * LANES]
        else:
            raw_ref[blk % 2, j - 1] = res

    def project(blk):
        return [functools.partial(project_one, blk, j) for j in range(len(bases))]

    def finish_gate(blk):
        sl = slice(blk * COL_BLOCK, (blk + 1) * COL_BLOCK)
        z_ssm = raw_ref[blk % 2, 4]
        zs_ref[:, sl] = (z_ssm * _sigmoid(z_ssm)).astype(_BF16)

    def finish_u(gv):
        rows = [slab_ref[gv, pl.ds(tl, tm // sub, stride=sub), :] for tl in range(sub)]
        for gl, o in enumerate(_lane_block_transpose(rows)):
            ug_ref[gv * nb + gl] = o

    def finish_conv(blk, half):
        sl = slice(blk * COL_BLOCK + half * LANES, blk * COL_BLOCK + (half + 1) * LANES)
        hl = slice(half * LANES, (half + 1) * LANES)
        b_gate, c_gate, v, z = (raw_ref[blk % 2, j, :, hl] for j in range(4))
        cv = c_gate * v
        ext = jnp.concatenate([carry_ref[:, sl], cv], axis=0)
        conv = (cb_ref[:, sl] + cw_ref[2:3, sl] * cv + cw_ref[1:2, sl] * ext[7:7 + tm]
                + cw_ref[0:1, sl] * ext[6:6 + tm])
        carry_ref[:, sl] = cv[tm - 8:]
        yc_ref[:, sl] = (b_gate * conv * (z * _sigmoid(z))).astype(_BF16)

    def finish(blk):
        return ([functools.partial(finish_gate, blk)]
                + [functools.partial(finish_u, blk * slabs_per_block + k) for k in range(slabs_per_block)]
                + [functools.partial(finish_conv, blk, half) for half in range(COL_BLOCK // LANES)])

    _interleave(project(0), [])
    for blk in range(1, n_blocks):
        _interleave(project(blk), finish(blk - 1))
    _interleave([], finish(n_blocks - 1))


def _inproj(x2d, norm_g, w_in, conv_w, conv_b, seq_len):
    n_tok = x2d.shape[0]
    tm = TOKEN_TILE
    const = lambda s: (0, 0)
    tile = lambda w: pl.BlockSpec((tm, w), lambda s: (s, 0))
    out_sds = jax.ShapeDtypeStruct((n_tok, D_CONV), _BF16)
    return pl.pallas_call(
        functools.partial(_inproj_kernel, tiles_per_seq=seq_len // tm),
        grid=(n_tok // tm,),
        in_specs=[
            tile(D_MODEL),
            pl.BlockSpec((1, D_MODEL), const),
            pl.BlockSpec(memory_space=pl.ANY),
            pl.BlockSpec((3, D_CONV), const),
            pl.BlockSpec((1, D_CONV), const),
        ],
        out_specs=[tile(D_CONV),
                   pl.BlockSpec((SSM_GROUPS, tm // SUBLANES, LANES), lambda s: (0, s, 0)),
                   tile(D_SSM)],
        out_shape=[out_sds,
                   jax.ShapeDtypeStruct((SSM_GROUPS, n_tok // SUBLANES, LANES), _F32),
                   out_sds],
        scratch_shapes=[pltpu.VMEM((8, D_CONV), _F32),
                        pltpu.VMEM((D_SSM // LANES, tm, LANES), _F32),
                        pltpu.VMEM((2, 5, tm, COL_BLOCK), _F32),
                        pltpu.VMEM((D_MODEL, N_IN), _BF16),
                        pltpu.SemaphoreType.DMA((2,))],
        compiler_params=pltpu.CompilerParams(dimension_semantics=("arbitrary",),
                                             vmem_limit_bytes=VMEM_LIMIT_BYTES),
        name="inproj_conv",
    )(x2d, norm_g, w_in.reshape(D_MODEL // tm, tm, N_IN), conv_w, conv_b)


def _complex_powers(ar, ai, exponent, nbits):
    pr = jnp.ones((exponent.shape[0], ar.shape[1]), _F32)
    pi = jnp.zeros_like(pr)
    cr, ci = ar, ai
    for b in range(nbits):
        bit = ((exponent >> b) & 1) == 1
        fr = jnp.where(bit, cr, 1.0)
        fi = jnp.where(bit, ci, 0.0)
        pr, pi = pr * fr - pi * fi, pr * fi + pi * fr
        cr, ci = cr * cr - ci * ci, 2.0 * cr * ci
    return pr, pi


def _re_im_sign():
    lane = lax.broadcasted_iota(jnp.int32, (1, LANES), 1)
    return jnp.where(lane < SSM_STATE, -1.0, 1.0).astype(_F32)


def _s5_prepare_group(g, rows_ref, mats_ref, kf_ref, ws_ref, wo_ref, a1_ref, a2_ref):
    t, th, jrows, h = S5_CHUNK, S5_TH, S5_POW_ROWS, SSM_GROUP
    sgn = _re_im_sign()
    nbits = max(1, (jrows - 1).bit_length())
    j = lax.broadcasted_iota(jnp.int32, (jrows, 1), 0)
    diag = (lax.broadcasted_iota(jnp.int32, (h, th), 0) == lax.broadcasted_iota(jnp.int32, (h, th), 1))
    lr, li, ldt, d_row = (rows_ref[g, :, 0:LANES], rows_ref[g, :, LANES:2 * LANES],
                          rows_ref[g, :, 2 * LANES:3 * LANES], rows_ref[g, :, 3 * LANES:])
    bt1, bt2, ct1, ct2 = (mats_ref[g, :, k * LANES:(k + 1) * LANES] for k in range(4))
    dt = jnp.exp(ldt)
    mag = jnp.exp(lr * dt)
    ar, ai = mag * jnp.cos(li * dt), mag * jnp.sin(li * dt)
    nr, ni = ar - 1.0, ai
    den = lr * lr + li * li
    qr = (nr * lr + ni * li) / den
    qi = (ni * lr - nr * li) / den
    b1 = qr * bt1 + (qi * sgn) * bt2
    b2 = (qr * sgn) * bt2 - qi * bt1
    c1 = ct1 * (-sgn)
    c2 = -ct2
    pr, pi = _complex_powers(ar, ai, j, nbits)
    rr, ri = _complex_powers(ar, ai, jnp.maximum(t - 1 - j, 0), nbits)
    cps = (pr[:, None, :] * c1[None] + pi[:, None, :] * c2[None]).reshape(jrows * h, LANES)
    cps = cps.astype(_BF16)
    b_hi = b1.astype(_BF16)
    b_lo = (b1 - b_hi.astype(_F32)).astype(_BF16)
    kf_ref[g] = _dot_nt(b_hi, cps[:th]) + _dot_nt(b_lo, cps[:th]) + jnp.where(diag, d_row, 0.0)
    wo_ref[g] = cps[h:h + th]
    ws_ref[g] = (rr[:t, None, :] * b1[None] + ri[:t, None, :] * b2[None]).reshape(th, LANES).astype(_BF16)
    a1_ref[g] = pr[t:t + 1]
    a2_ref[g] = pi[t:t + 1]


def _s5_kernel(ug_ref, rows_ref, mats_ref, yg_ref, kf_ref, ws_ref, wo_ref, a1_ref, a2_ref,
               mt_ref, xg_ref, yi_ref, sl_ref, slsw_ref, sin_ref, *, chunks_per_seq):
    nb = ug_ref.shape[0]
    t, th, h = S5_CHUNK, S5_TH, SSM_GROUP
    t_hi_n = t // SUBLANES
    n_chunks = ug_ref.shape[1] // t_hi_n
    n_seq = n_chunks // chunks_per_seq
    pitch = S5_SCAN_PITCH

    for g in range(nb):
        _s5_prepare_group(g, rows_ref, mats_ref, kf_ref, ws_ref, wo_ref, a1_ref, a2_ref)
        xg = jnp.concatenate([ug_ref[g, pl.ds(k, n_chunks, stride=t_hi_n), :]
                              for k in range(t_hi_n)], axis=1).astype(_BF16)
        xg_ref[g] = xg
        s = _dot(xg, ws_ref[g])
        s_sw = pltpu.roll(s, SSM_STATE, axis=1)
        for b in range(n_seq):
            rows = slice(b * chunks_per_seq, (b + 1) * chunks_per_seq)
            sl_ref[pl.ds(b * nb + g, chunks_per_seq, stride=pitch), :] = s[rows]
            slsw_ref[pl.ds(b * nb + g, chunks_per_seq, stride=pitch), :] = s_sw[rows]

    a1 = jnp.concatenate([a1_ref[g] for g in range(nb)] * n_seq, axis=0)
    a2 = jnp.concatenate([a2_ref[g] for g in range(nb)] * n_seq, axis=0) * _re_im_sign()
    s = jnp.zeros((n_seq * nb, LANES), _F32)
    s_sw = s
    for c in range(chunks_per_seq):
        sin_ref[c * pitch:c * pitch + n_seq * nb, :] = s
        s, s_sw = (a1 * s + a2 * s_sw + sl_ref[c * pitch:c * pitch + n_seq * nb, :],
                   a1 * s_sw - a2 * s + slsw_ref[c * pitch:c * pitch + n_seq * nb, :])

    for g in range(nb):
        zk = jnp.concatenate([jnp.zeros((h, th), _F32), kf_ref[g]], axis=1)
        for k in range(t):
            mt_ref[g, k * h:(k + 1) * h, :] = zk[:, th - k * h:2 * th - k * h].astype(_BF16)
        yi_ref[g] = _dot(xg_ref[g], mt_ref[g])

    for g in range(nb):
        s_in = jnp.concatenate([sin_ref[pl.ds(b * nb + g, chunks_per_seq, stride=pitch), :]
                                for b in range(n_seq)], axis=0).astype(_BF16)
        y = yi_ref[g] + _dot_nt(s_in, wo_ref[g])
        for k in range(t_hi_n):
            yg_ref[g, pl.ds(k, n_chunks, stride=t_hi_n), :] = y[:, k * LANES:(k + 1) * LANES]


def _s5_core(ug, rows, mats, chunks_per_seq):
    _, n_rows, _ = ug.shape
    h, th, nb = SSM_GROUP, S5_TH, GROUPS_PER_BLOCK
    group_rows = pl.BlockSpec((nb, n_rows, LANES), lambda i: (i, 0, 0))
    return pl.pallas_call(
        functools.partial(_s5_kernel, chunks_per_seq=chunks_per_seq),
        grid=(SSM_GROUPS // nb,),
        in_specs=[group_rows,
                  pl.BlockSpec((nb, 1, rows.shape[-1]), lambda i: (i, 0, 0)),
                  pl.BlockSpec((nb, h, mats.shape[-1]), lambda i: (i, 0, 0))],
        out_specs=group_rows,
        out_shape=jax.ShapeDtypeStruct(ug.shape, _F32),
        scratch_shapes=[pltpu.VMEM((nb, h, th), _F32),
                        pltpu.VMEM((nb, th, LANES), _BF16),
                        pltpu.VMEM((nb, th, LANES), _BF16),
                        pltpu.VMEM((nb, 1, LANES), _F32),
                        pltpu.VMEM((nb, 1, LANES), _F32),
                        pltpu.VMEM((nb, th, th), _BF16),
                        pltpu.VMEM((nb, n_rows * SUBLANES // S5_CHUNK, th), _BF16),
                        pltpu.VMEM((nb, n_rows * SUBLANES // S5_CHUNK, th), _F32)]
        + [pltpu.VMEM((chunks_per_seq * S5_SCAN_PITCH, LANES), _F32)] * 3,
        compiler_params=pltpu.CompilerParams(dimension_semantics=("arbitrary",),
                                             vmem_limit_bytes=VMEM_LIMIT_BYTES),
        name="s5_core",
    )(ug, rows, mats)


def _out_kernel(yc_ref, yg_ref, zs_ref, x_ref, wg_hbm, bg_ref, wo_hbm, gp_ref, o_ref, slab_ref,
                wg_ref, wo_ref, stage_ref, ys_ref, sem):
    tm = x_ref.shape[0]
    nb, sub = GROUPS_PER_BLOCK, SUBLANES
    rp = tm // OUT_ROW_PARTS
    slabs_per_block = COL_BLOCK // LANES

    @pl.when(pl.program_id(0) == 0)
    def _():
        _load_rows_as_bf16(wo_hbm, wo_ref, stage_ref, sem)
        _load_rows_as_bf16(wg_hbm, wg_ref, stage_ref, sem)

    def activation(part):
        rows = slice(part * rp, (part + 1) * rp)
        grows = slice(part * rp // sub, (part + 1) * rp // sub)
        acc = {"pre": bg_ref[...], "y": []}

        def block(blk):
            gvs = range(blk * slabs_per_block, (blk + 1) * slabs_per_block)
            for gv in gvs:
                tok = _lane_block_transpose([yg_ref[gv * nb + gl, grows, :] for gl in range(nb)])
                for tl in range(sub):
                    slab_ref[gv, pl.ds(part * rp + tl, rp // sub, stride=sub), :] = tok[tl]
            y = jax.nn.gelu(jnp.concatenate([slab_ref[gv, rows, :] for gv in gvs], axis=1))
            acc["y"].append(y)
            acc["pre"] = acc["pre"] + _dot(y.astype(_BF16), wg_ref[blk * COL_BLOCK:(blk + 1) * COL_BLOCK, :])

        def gate():
            y = jnp.concatenate(acc["y"], axis=1)
            ys_ref[part] = (y * _sigmoid(acc["pre"]) * zs_ref[rows, :].astype(_F32)).astype(_BF16)

        return [functools.partial(block, blk) for blk in range(D_SSM // COL_BLOCK)] + [gate]

    def conv_projection(part):
        rows = slice(part * rp, (part + 1) * rp)

        def cols(c):
            sl = slice(c * OUT_COL_BLOCK, (c + 1) * OUT_COL_BLOCK)
            o_ref[rows, sl] = _dot(yc_ref[rows, :], wo_ref[:D_CONV, sl])

        return [functools.partial(cols, c) for c in range(D_MODEL // OUT_COL_BLOCK)]

    def ssm_projection(part):
        rows = slice(part * rp, (part + 1) * rp)

        def cols(c):
            sl = slice(c * OUT_COL_BLOCK, (c + 1) * OUT_COL_BLOCK)
            o_ref[rows, sl] += _dot(ys_ref[part], wo_ref[D_CONV:, sl])

        return [functools.partial(cols, c) for c in range(D_MODEL // OUT_COL_BLOCK)]

    def norm_residual(part):
        rows = slice(part * rp, (part + 1) * rp)
        o = o_ref[rows, :]
        ms = jnp.mean(o * o, axis=-1, keepdims=True)
        o_ref[rows, :] = x_ref[rows, :] + o * lax.rsqrt(ms + EPS) * gp_ref[...]

    for part in range(OUT_ROW_PARTS + 1):
        mxu, vpu = [], []
        if part > 0:
            mxu += ssm_projection(part - 1)
        if part > 1:
            vpu.append(functools.partial(norm_residual, part - 2))
        if part < OUT_ROW_PARTS:
            mxu += conv_projection(part)
            vpu += activation(part)
        _interleave(mxu, vpu)
    norm_residual(OUT_ROW_PARTS - 1)


def _out_proj(yc, yg, zs, x2d, w_glu, b_glu, w_out, norm_g):
    n_tok = x2d.shape[0]
    tm = TOKEN_TILE
    const = lambda i: (0, 0)
    tile = lambda w: pl.BlockSpec((tm, w), lambda i: (i, 0))
    return pl.pallas_call(
        _out_kernel,
        grid=(n_tok // tm,),
        in_specs=[
            tile(D_CONV),
            pl.BlockSpec((SSM_GROUPS, tm // SUBLANES, LANES), lambda i: (0, i, 0)),
            tile(D_SSM), tile(D_MODEL),
            pl.BlockSpec(memory_space=pl.ANY),
            pl.BlockSpec((1, D_SSM), const),
            pl.BlockSpec(memory_space=pl.ANY),
            pl.BlockSpec((1, D_MODEL), const),
        ],
        out_specs=tile(D_MODEL),
        out_shape=jax.ShapeDtypeStruct((n_tok, D_MODEL), _F32),
        scratch_shapes=[pltpu.VMEM((D_SSM // LANES, tm, LANES), _F32),
                        pltpu.VMEM((D_SSM, D_SSM), _BF16),
                        pltpu.VMEM((D_CONV + D_SSM, D_MODEL), _BF16),
                        pltpu.VMEM((WEIGHT_STAGE_SLOTS, WEIGHT_STAGE_ROWS, D_MODEL), _F32),
                        pltpu.VMEM((OUT_ROW_PARTS, tm // OUT_ROW_PARTS, D_SSM), _BF16),
                        pltpu.SemaphoreType.DMA((WEIGHT_STAGE_SLOTS,))],
        compiler_params=pltpu.CompilerParams(dimension_semantics=("arbitrary",),
                                             vmem_limit_bytes=VMEM_LIMIT_BYTES),
        name="glu_outproj",
    )(yc, yg, zs, x2d, w_glu, b_glu, w_out, norm_g)


def kernel(x, norm_pre_g, w_in, conv_w, conv_b, ssm_a_re, ssm_a_im, ssm_log_dt, ssm_b_re, ssm_b_im,
           ssm_c_re, ssm_c_im, ssm_d, w_glu, b_glu, w_out, norm_post_g):
    bsz, seq_len, _ = x.shape
    n_tok = bsz * seq_len
    g, p, h, t = SSM_GROUPS, SSM_STATE, SSM_GROUP, S5_CHUNK
    assert seq_len % TOKEN_TILE == 0 and TOKEN_TILE % t == 0
    x2d = x.reshape(n_tok, D_MODEL)

    yc, ug, zs = _inproj(x2d, norm_pre_g.reshape(1, -1), w_in, conv_w, conv_b.reshape(1, -1), seq_len)

    brt, bit = ssm_b_re.transpose(0, 2, 1), ssm_b_im.transpose(0, 2, 1)
    a_re, a_im = ssm_a_re.reshape(g, 1, p), ssm_a_im.reshape(g, 1, p)
    rows = jnp.concatenate(
        [a_re, a_re, a_im, a_im, jnp.broadcast_to(ssm_log_dt.reshape(g, 1, 1), (g, 1, LANES)),
         ssm_d.reshape(g, 1, h), jnp.zeros((g, 1, S5_TH - h), _F32)], axis=-1)
    mats = jnp.concatenate([brt, bit, bit, brt, ssm_c_re, ssm_c_im, ssm_c_im, ssm_c_re], axis=-1)
    yg = _s5_core(ug, rows, mats, seq_len // t)

    out = _out_proj(yc, yg, zs, x2d, w_glu, b_glu.reshape(1, -1), w_out, norm_post_g.reshape(1, -1))
    return out.reshape(bsz, seq_len, D_MODEL)
```

```python
import functools

import jax
import jax.numpy as jnp
from jax import lax
from jax.experimental import pallas as pl
from jax.experimental.pallas import tpu as pltpu

D_MODEL = 2048
D_CONV = 1024
D_SSM = 1024
SSM_GROUP = 16
SSM_GROUPS = 64
SSM_STATE = 64
N_IN = 4 * D_CONV + 2 * D_SSM
EPS = 1e-6

S5_CHUNK = 16
S5_TH = S5_CHUNK * SSM_GROUP
S5_POW_ROWS = S5_CHUNK + 8
S5_SCAN_PITCH = 24
LANES = 128
SUBLANES = 8
GROUPS_PER_BLOCK = LANES // SSM_GROUP
TOKEN_TILE = 512
COL_BLOCK = 256
OUT_ROW_PARTS = 2
OUT_COL_BLOCK = 512
VMEM_LIMIT_BYTES = 56 * 1024 * 1024

_F32 = jnp.float32
_BF16 = jnp.bfloat16


def _dot(a, b):
    return jnp.dot(a, b, preferred_element_type=_F32)


def _dot_nt(a, b):
    return lax.dot_general(a, b, (((1,), (1,)), ((), ())), preferred_element_type=_F32)


def _sigmoid(z):
    return 1.0 / (1.0 + jnp.exp(-z))


def _interleave(mxu_items, vpu_items):
    n = max(len(mxu_items), 1)
    done = 0
    for i, item in enumerate(mxu_items):
        item()
        upto = -(-len(vpu_items) * (i + 1) // n)
        for piece in vpu_items[done:upto]:
            piece()
        done = upto
    for piece in vpu_items[done:]:
        piece()


def _load_weight_as_bf16(w_hbm, w_ref, stage, sem, cols):
    a, r, n = w_hbm.shape

    def chunk(c, slot):
        return pltpu.make_async_copy(w_hbm.at[:, :, pl.ds(c * cols, cols)], stage(slot), sem.at[slot])

    chunk(0, 0).start()
    for c in range(n // cols):
        slot = c % 2
        if c + 1 < n // cols:
            chunk(c + 1, 1 - slot).start()
        chunk(c, slot).wait()
        for k in range(a):
            w_ref[k * r:(k + 1) * r, c * cols:(c + 1) * cols] = stage(slot)[k].astype(_BF16)


def _lane_block_transpose(xs):
    nb = len(xs)
    diag = _diagonal_gather(xs)
    moved = [w if d == 0 else pltpu.roll(w, d * SSM_GROUP, axis=1) for d, w in enumerate(diag)]
    back = _diagonal_gather(moved)
    return [back[(-b) % nb] for b in range(nb)]


def _diagonal_gather(arrs):
    assert len(arrs) == 8
    c = lax.broadcasted_iota(jnp.int32, (1, LANES), 1) // SSM_GROUP
    c0, c1, c2 = c & 1, (c >> 1) & 1, (c >> 2) & 1
    s1 = {(k0, hi): jnp.where((c0 ^ k0) == 1, arrs[2 * hi + 1], arrs[2 * hi])
          for k0 in range(2) for hi in range(4)}
    s2 = {}
    for k0 in range(2):
        for k1 in range(2):
            e1 = c1 ^ k1 ^ (c0 & k0)
            for top in range(2):
                s2[(k0, k1, top)] = jnp.where(e1 == 1, s1[(k0, 2 * top + 1)], s1[(k0, 2 * top)])
    out = []
    for k in range(8):
        k0, k1, k2 = k & 1, (k >> 1) & 1, (k >> 2) & 1
        carry0 = c0 & k0
        carry1 = (c1 & k1) | (c1 & carry0) | (k1 & carry0)
        e2 = c2 ^ k2 ^ carry1
        out.append(jnp.where(e2 == 1, s2[(k0, k1, 1)], s2[(k0, k1, 0)]))
    return out


def _inproj_kernel(x_ref, g_ref, w_hbm, cw_ref, cb_ref, yc_ref, ug_ref, zs_ref, carry_ref, slab_ref,
                   raw_ref, w_ref, sem, *, tiles_per_seq):
    tm = x_ref.shape[0]
    nb, sub = GROUPS_PER_BLOCK, SUBLANES
    slabs_per_block = COL_BLOCK // LANES
    n_blocks = D_CONV // COL_BLOCK
    bases = (4 * D_CONV, 0, D_CONV, 2 * D_CONV, 3 * D_CONV, 4 * D_CONV + D_SSM)

    @pl.when(pl.program_id(0) == 0)
    def _():
        _load_weight_as_bf16(w_hbm, w_ref, lambda slot: raw_ref.at[slot, pl.ds(0, w_hbm.shape[0])],
                             sem, COL_BLOCK)

    @pl.when(pl.program_id(0) % tiles_per_seq == 0)
    def _():
        carry_ref[...] = jnp.zeros_like(carry_ref)

    x = x_ref[...]
    ms = jnp.mean(x * x, axis=-1, keepdims=True)
    h = (x * lax.rsqrt(ms + EPS) * g_ref[...]).astype(_BF16)

    def project_one(blk, j):
        lo = bases[j] + blk * COL_BLOCK
        res = _dot(h, w_ref[:, lo:lo + COL_BLOCK])
        if j == 0:
            for k in range(slabs_per_block):
                slab_ref[blk * slabs_per_block + k] = res[:, k * LANES:(k + 1) * LANES]
        else:
            raw_ref[blk % 2, j - 1] = res

    def project(blk):
        return [functools.partial(project_one, blk, j) for j in range(len(bases))]

    def finish_gate(blk):
        sl = slice(blk * COL_BLOCK, (blk + 1) * COL_BLOCK)
        z_ssm = raw_ref[blk % 2, 4]
        zs_ref[:, sl] = (z_ssm * _sigmoid(z_ssm)).astype(_BF16)

    def finish_u(gv):
        rows = [slab_ref[gv, pl.ds(tl, tm // sub, stride=sub), :] for tl in range(sub)]
        for gl, o in enumerate(_lane_block_transpose(rows)):
            ug_ref[gv * nb + gl] = o

    def finish_conv(blk, half):
        sl = slice(blk * COL_BLOCK + half * LANES, blk * COL_BLOCK + (half + 1) * LANES)
        hl = slice(half * LANES, (half + 1) * LANES)
        b_gate, c_gate, v, z = (raw_ref[blk % 2, j, :, hl] for j in range(4))
        cv = c_gate * v
        ext = jnp.concatenate([carry_ref[:, sl], cv], axis=0)
        conv = (cb_ref[:, sl] + cw_ref[2:3, sl] * cv + cw_ref[1:2, sl] * ext[sub - 1:sub - 1 + tm]
                + cw_ref[0:1, sl] * ext[sub - 2:sub - 2 + tm])
        carry_ref[:, sl] = cv[tm - sub:]
        yc_ref[:, sl] = (b_gate * conv * (z * _sigmoid(z))).astype(_BF16)

    def finish(blk):
        return ([functools.partial(finish_gate, blk)]
                + [functools.partial(finish_u, blk * slabs_per_block + k) for k in range(slabs_per_block)]
                + [functools.partial(finish_conv, blk, half) for half in range(COL_BLOCK // LANES)])

    _interleave(project(0), [])
    for blk in range(1, n_blocks):
        _interleave(project(blk), finish(blk - 1))
    _interleave([], finish(n_blocks - 1))


def _inproj(x2d, norm_g, w_in, conv_w, conv_b, seq_len):
    n_tok = x2d.shape[0]
    tm = TOKEN_TILE
    const = lambda s: (0, 0)
    tile = lambda w: pl.BlockSpec((tm, w), lambda s: (s, 0))
    out_sds = jax.ShapeDtypeStruct((n_tok, D_CONV), _BF16)
    return pl.pallas_call(
        functools.partial(_inproj_kernel, tiles_per_seq=seq_len // tm),
        grid=(n_tok // tm,),
        in_specs=[
            tile(D_MODEL),
            pl.BlockSpec((1, D_MODEL), const),
            pl.BlockSpec(memory_space=pl.ANY),
            pl.BlockSpec((3, D_CONV), const),
            pl.BlockSpec((1, D_CONV), const),
        ],
        out_specs=[tile(D_CONV),
                   pl.BlockSpec((SSM_GROUPS, tm // SUBLANES, LANES), lambda s: (0, s, 0)),
                   tile(D_SSM)],
        out_shape=[out_sds,
                   jax.ShapeDtypeStruct((SSM_GROUPS, n_tok // SUBLANES, LANES), _F32),
                   out_sds],
        scratch_shapes=[pltpu.VMEM((SUBLANES, D_CONV), _F32),
                        pltpu.VMEM((D_SSM // LANES, tm, LANES), _F32),
                        pltpu.VMEM((2, 5, tm, COL_BLOCK), _F32),
                        pltpu.VMEM((D_MODEL, N_IN), _BF16),
                        pltpu.SemaphoreType.DMA((2,))],
        compiler_params=pltpu.CompilerParams(dimension_semantics=("arbitrary",),
                                             vmem_limit_bytes=VMEM_LIMIT_BYTES),
        name="inproj_conv",
    )(x2d, norm_g, w_in.reshape(D_MODEL // tm, tm, N_IN), conv_w, conv_b)


def _complex_powers(ar, ai, exponent, nbits):
    pr = jnp.ones((exponent.shape[0], ar.shape[1]), _F32)
    pi = jnp.zeros_like(pr)
    cr, ci = ar, ai
    for b in range(nbits):
        bit = ((exponent >> b) & 1) == 1
        fr = jnp.where(bit, cr, 1.0)
        fi = jnp.where(bit, ci, 0.0)
        pr, pi = pr * fr - pi * fi, pr * fi + pi * fr
        cr, ci = cr * cr - ci * ci, 2.0 * cr * ci
    return pr, pi


def _re_im_sign():
    lane = lax.broadcasted_iota(jnp.int32, (1, LANES), 1)
    return jnp.where(lane < SSM_STATE, -1.0, 1.0).astype(_F32)


def _s5_prepare_group(g, rows_ref, mats_ref, kf_ref, ws_ref, wo_ref, a1_ref, a2_ref):
    t, th, jrows, h = S5_CHUNK, S5_TH, S5_POW_ROWS, SSM_GROUP
    sgn = _re_im_sign()
    nbits = max(1, (jrows - 1).bit_length())
    j = lax.broadcasted_iota(jnp.int32, (jrows, 1), 0)
    diag = (lax.broadcasted_iota(jnp.int32, (h, th), 0) == lax.broadcasted_iota(jnp.int32, (h, th), 1))
    lr, li, ldt, d_row = (rows_ref[g, :, 0:LANES], rows_ref[g, :, LANES:2 * LANES],
                          rows_ref[g, :, 2 * LANES:3 * LANES], rows_ref[g, :, 3 * LANES:])
    bt1, bt2, ct1, ct2 = (mats_ref[g, :, k * LANES:(k + 1) * LANES] for k in range(4))
    dt = jnp.exp(ldt)
    mag = jnp.exp(lr * dt)
    ar, ai = mag * jnp.cos(li * dt), mag * jnp.sin(li * dt)
    nr, ni = ar - 1.0, ai
    den = lr * lr + li * li
    qr = (nr * lr + ni * li) / den
    qi = (ni * lr - nr * li) / den
    b1 = qr * bt1 + (qi * sgn) * bt2
    b2 = (qr * sgn) * bt2 - qi * bt1
    c1 = ct1 * (-sgn)
    c2 = -ct2
    pr, pi = _complex_powers(ar, ai, j, nbits)
    rr, ri = _complex_powers(ar, ai, jnp.maximum(t - 1 - j, 0), nbits)
    cps = (pr[:, None, :] * c1[None] + pi[:, None, :] * c2[None]).reshape(jrows * h, LANES)
    cps = cps.astype(_BF16)
    b_hi = b1.astype(_BF16)
    b_lo = (b1 - b_hi.astype(_F32)).astype(_BF16)
    kf_ref[g] = _dot_nt(b_hi, cps[:th]) + _dot_nt(b_lo, cps[:th]) + jnp.where(diag, d_row, 0.0)
    wo_ref[g] = cps[h:h + th]
    ws_ref[g] = (rr[:t, None, :] * b1[None] + ri[:t, None, :] * b2[None]).reshape(th, LANES).astype(_BF16)
    a1_ref[g] = pr[t:t + 1]
    a2_ref[g] = pi[t:t + 1]


def _s5_kernel(ug_ref, rows_ref, mats_ref, wglu_ref, wout_ref, yg_ref, wglu_bf_ref, wout_bf_ref,
               kf_ref, ws_ref, wo_ref, a1_ref, a2_ref,
               mt_ref, xg_ref, yi_ref, sl_ref, slsw_ref, sin_ref, *, chunks_per_seq):
    nb = ug_ref.shape[0]
    t, th, h = S5_CHUNK, S5_TH, SSM_GROUP
    t_hi_n = t // SUBLANES
    n_chunks = ug_ref.shape[1] // t_hi_n
    n_seq = n_chunks // chunks_per_seq
    pitch = S5_SCAN_PITCH
    assert n_seq * nb <= pitch

    wglu_bf_ref[...] = wglu_ref[...].astype(_BF16)
    wout_bf_ref[...] = wout_ref[...].astype(_BF16)

    for g in range(nb):
        _s5_prepare_group(g, rows_ref, mats_ref, kf_ref, ws_ref, wo_ref, a1_ref, a2_ref)
        xg = jnp.concatenate([ug_ref[g, pl.ds(k, n_chunks, stride=t_hi_n), :]
                              for k in range(t_hi_n)], axis=1).astype(_BF16)
        xg_ref[g] = xg
        s = _dot(xg, ws_ref[g])
        s_sw = pltpu.roll(s, SSM_STATE, axis=1)
        for b in range(n_seq):
            rows = slice(b * chunks_per_seq, (b + 1) * chunks_per_seq)
            sl_ref[pl.ds(b * nb + g, chunks_per_seq, stride=pitch), :] = s[rows]
            slsw_ref[pl.ds(b * nb + g, chunks_per_seq, stride=pitch), :] = s_sw[rows]

    a1 = jnp.concatenate([a1_ref[g] for g in range(nb)] * n_seq, axis=0)
    a2 = jnp.concatenate([a2_ref[g] for g in range(nb)] * n_seq, axis=0) * _re_im_sign()
    s = jnp.zeros((n_seq * nb, LANES), _F32)
    s_sw = s
    for c in range(chunks_per_seq):
        sin_ref[c * pitch:c * pitch + n_seq * nb, :] = s
        s, s_sw = (a1 * s + a2 * s_sw + sl_ref[c * pitch:c * pitch + n_seq * nb, :],
                   a1 * s_sw - a2 * s + slsw_ref[c * pitch:c * pitch + n_seq * nb, :])

    for g in range(nb):
        zk = jnp.concatenate([jnp.zeros((h, th), _F32), kf_ref[g]], axis=1)
        for k in range(t):
            mt_ref[g, k * h:(k + 1) * h, :] = zk[:, th - k * h:2 * th - k * h].astype(_BF16)
        yi_ref[g] = _dot(xg_ref[g], mt_ref[g])

    for g in range(nb):
        s_in = jnp.concatenate([sin_ref[pl.ds(b * nb + g, chunks_per_seq, stride=pitch), :]
                                for b in range(n_seq)], axis=0).astype(_BF16)
        y = yi_ref[g] + _dot_nt(s_in, wo_ref[g])
        for k in range(t_hi_n):
            yg_ref[g, pl.ds(k, n_chunks, stride=t_hi_n), :] = y[:, k * LANES:(k + 1) * LANES]


def _s5_core(ug, rows, mats, w_glu, w_out, chunks_per_seq):
    _, n_rows, _ = ug.shape
    h, th, nb = SSM_GROUP, S5_TH, GROUPS_PER_BLOCK
    steps = SSM_GROUPS // nb
    group_rows = pl.BlockSpec((nb, n_rows, LANES), lambda i: (i, 0, 0))
    row_block = lambda w: pl.BlockSpec((w.shape[0] // steps, w.shape[1]), lambda i: (i, 0))
    return pl.pallas_call(
        functools.partial(_s5_kernel, chunks_per_seq=chunks_per_seq),
        grid=(steps,),
        in_specs=[group_rows,
                  pl.BlockSpec((nb, 1, rows.shape[-1]), lambda i: (i, 0, 0)),
                  pl.BlockSpec((nb, h, mats.shape[-1]), lambda i: (i, 0, 0)),
                  row_block(w_glu), row_block(w_out)],
        out_specs=[group_rows, row_block(w_glu), row_block(w_out)],
        out_shape=[jax.ShapeDtypeStruct(ug.shape, _F32),
                   jax.ShapeDtypeStruct(w_glu.shape, _BF16),
                   jax.ShapeDtypeStruct(w_out.shape, _BF16)],
        scratch_shapes=[pltpu.VMEM((nb, h, th), _F32),
                        pltpu.VMEM((nb, th, LANES), _BF16),
                        pltpu.VMEM((nb, th, LANES), _BF16),
                        pltpu.VMEM((nb, 1, LANES), _F32),
                        pltpu.VMEM((nb, 1, LANES), _F32),
                        pltpu.VMEM((nb, th, th), _BF16),
                        pltpu.VMEM((nb, n_rows * SUBLANES // S5_CHUNK, th), _BF16),
                        pltpu.VMEM((nb, n_rows * SUBLANES // S5_CHUNK, th), _F32)]
        + [pltpu.VMEM((chunks_per_seq * S5_SCAN_PITCH, LANES), _F32)] * 3,
        compiler_params=pltpu.CompilerParams(dimension_semantics=("arbitrary",),
                                             vmem_limit_bytes=VMEM_LIMIT_BYTES),
        name="s5_core",
    )(ug, rows, mats, w_glu, w_out)


def _out_kernel(yc_ref, yg_ref, zs_ref, x_ref, wg_ref, bg_ref, wo_ref, gp_ref, o_ref, slab_ref, ys_ref):
    tm = x_ref.shape[0]
    nb, sub = GROUPS_PER_BLOCK, SUBLANES
    rp = tm // OUT_ROW_PARTS
    slabs_per_block = COL_BLOCK // LANES

    def activation(part):
        rows = slice(part * rp, (part + 1) * rp)
        grows = slice(part * rp // sub, (part + 1) * rp // sub)
        acc = {"pre": bg_ref[...], "y": []}

        def block(blk):
            gvs = range(blk * slabs_per_block, (blk + 1) * slabs_per_block)
            for gv in gvs:
                tok = _lane_block_transpose([yg_ref[gv * nb + gl, grows, :] for gl in range(nb)])
                for tl in range(sub):
                    slab_ref[gv, pl.ds(part * rp + tl, rp // sub, stride=sub), :] = tok[tl]
            y = jax.nn.gelu(jnp.concatenate([slab_ref[gv, rows, :] for gv in gvs], axis=1))
            acc["y"].append(y)
            acc["pre"] = acc["pre"] + _dot(y.astype(_BF16), wg_ref[blk * COL_BLOCK:(blk + 1) * COL_BLOCK, :])

        def gate():
            y = jnp.concatenate(acc["y"], axis=1)
            ys_ref[part] = (y * _sigmoid(acc["pre"]) * zs_ref[rows, :].astype(_F32)).astype(_BF16)

        return [functools.partial(block, blk) for blk in range(D_SSM // COL_BLOCK)] + [gate]

    def conv_projection(part):
        rows = slice(part * rp, (part + 1) * rp)

        def cols(c):
            sl = slice(c * OUT_COL_BLOCK, (c + 1) * OUT_COL_BLOCK)
            o_ref[rows, sl] = _dot(yc_ref[rows, :], wo_ref[:D_CONV, sl])

        return [functools.partial(cols, c) for c in range(D_MODEL // OUT_COL_BLOCK)]

    def ssm_projection(part):
        rows = slice(part * rp, (part + 1) * rp)

        def cols(c):
            sl = slice(c * OUT_COL_BLOCK, (c + 1) * OUT_COL_BLOCK)
            o_ref[rows, sl] += _dot(ys_ref[part], wo_ref[D_CONV:, sl])

        return [functools.partial(cols, c) for c in range(D_MODEL // OUT_COL_BLOCK)]

    def norm_residual(part):
        rows = slice(part * rp, (part + 1) * rp)
        o = o_ref[rows, :]
        ms = jnp.mean(o * o, axis=-1, keepdims=True)
        o_ref[rows, :] = x_ref[rows, :] + o * lax.rsqrt(ms + EPS) * gp_ref[...]

    for part in range(OUT_ROW_PARTS + 1):
        mxu, vpu = [], []
        if part > 0:
            mxu += ssm_projection(part - 1)
        if part > 1:
            vpu.append(functools.partial(norm_residual, part - 2))
        if part < OUT_ROW_PARTS:
            mxu += conv_projection(part)
            vpu += activation(part)
        _interleave(mxu, vpu)
    norm_residual(OUT_ROW_PARTS - 1)


def _out_proj(yc, yg, zs, x2d, w_glu_bf16, b_glu, w_out_bf16, norm_g):
    n_tok = x2d.shape[0]
    tm = TOKEN_TILE
    const = lambda i: (0, 0)
    tile = lambda w: pl.BlockSpec((tm, w), lambda i: (i, 0))
    return pl.pallas_call(
        _out_kernel,
        grid=(n_tok // tm,),
        in_specs=[
            tile(D_CONV),
            pl.BlockSpec((SSM_GROUPS, tm // SUBLANES, LANES), lambda i: (0, i, 0)),
            tile(D_SSM), tile(D_MODEL),
            pl.BlockSpec((D_SSM, D_SSM), const, pipeline_mode=pl.Buffered(1)),
            pl.BlockSpec((1, D_SSM), const),
            pl.BlockSpec((D_CONV + D_SSM, D_MODEL), const, pipeline_mode=pl.Buffered(1)),
            pl.BlockSpec((1, D_MODEL), const),
        ],
        out_specs=tile(D_MODEL),
        out_shape=jax.ShapeDtypeStruct((n_tok, D_MODEL), _F32),
        scratch_shapes=[pltpu.VMEM((D_SSM // LANES, tm, LANES), _F32),
                        pltpu.VMEM((OUT_ROW_PARTS, tm // OUT_ROW_PARTS, D_SSM), _BF16)],
        compiler_params=pltpu.CompilerParams(dimension_semantics=("arbitrary",),
                                             vmem_limit_bytes=VMEM_LIMIT_BYTES),
        name="glu_outproj",
    )(yc, yg, zs, x2d, w_glu_bf16, b_glu, w_out_bf16, norm_g)


def kernel(x, norm_pre_g, w_in, conv_w, conv_b, ssm_a_re, ssm_a_im, ssm_log_dt, ssm_b_re, ssm_b_im,
           ssm_c_re, ssm_c_im, ssm_d, w_glu, b_glu, w_out, norm_post_g):
    bsz, seq_len, _ = x.shape
    n_tok = bsz * seq_len
    g, p, h, t = SSM_GROUPS, SSM_STATE, SSM_GROUP, S5_CHUNK
    assert seq_len % TOKEN_TILE == 0 and TOKEN_TILE % t == 0
    x2d = x.reshape(n_tok, D_MODEL)

    yc, ug, zs = _inproj(x2d, norm_pre_g.reshape(1, -1), w_in, conv_w, conv_b.reshape(1, -1), seq_len)

    brt, bit = ssm_b_re.transpose(0, 2, 1), ssm_b_im.transpose(0, 2, 1)
    a_re, a_im = ssm_a_re.reshape(g, 1, p), ssm_a_im.reshape(g, 1, p)
    rows = jnp.concatenate(
        [a_re, a_re, a_im, a_im, jnp.broadcast_to(ssm_log_dt.reshape(g, 1, 1), (g, 1, LANES)),
         ssm_d.reshape(g, 1, h), jnp.zeros((g, 1, S5_TH - h), _F32)], axis=-1)
    mats = jnp.concatenate([brt, bit, bit, brt, ssm_c_re, ssm_c_im, ssm_c_im, ssm_c_re], axis=-1)
    yg, w_glu_bf16, w_out_bf16 = _s5_core(ug, rows, mats, w_glu, w_out, seq_len // t)

    out = _out_proj(yc, yg, zs, x2d, w_glu_bf16, b_glu.reshape(1, -1), w_out_bf16,
                    norm_post_g.reshape(1, -1))
    return out.reshape(bsz, seq_len, D_MODEL)
```

```python
import functools

import jax
import jax.numpy as jnp
from jax import lax
from jax.experimental import pallas as pl
from jax.experimental.pallas import tpu as pltpu

D_MODEL = 2048
D_CONV = 1024
D_SSM = 1024
SSM_GROUP = 16
SSM_GROUPS = 64
SSM_STATE = 64
N_IN = 4 * D_CONV + 2 * D_SSM
EPS = 1e-6

S5_CHUNK = 16
S5_TH = S5_CHUNK * SSM_GROUP
S5_POW_ROWS = S5_CHUNK + 8
S5_SCAN_PITCH = 24
LANES = 128
SUBLANES = 8
GROUPS_PER_BLOCK = LANES // SSM_GROUP
TOKEN_TILE = 512
COL_BLOCK = 256
OUT_ROW_PARTS = 2
OUT_COL_BLOCK = 512
VMEM_LIMIT_BYTES = 56 * 1024 * 1024

_F32 = jnp.float32
_BF16 = jnp.bfloat16


def _dot(a, b):
    return jnp.dot(a, b, preferred_element_type=_F32)


def _dot_nt(a, b):
    return lax.dot_general(a, b, (((1,), (1,)), ((), ())), preferred_element_type=_F32)


def _sigmoid(z):
    return 1.0 / (1.0 + jnp.exp(-z))


def _interleave(mxu_items, vpu_items):
    n = max(len(mxu_items), 1)
    done = 0
    for i, item in enumerate(mxu_items):
        item()
        upto = -(-len(vpu_items) * (i + 1) // n)
        for piece in vpu_items[done:upto]:
            piece()
        done = upto
    for piece in vpu_items[done:]:
        piece()


def _load_weight_as_bf16(w_hbm, w_ref, stage, sem, cols):
    a, r, n = w_hbm.shape

    def chunk(c, slot):
        return pltpu.make_async_copy(w_hbm.at[:, :, pl.ds(c * cols, cols)], stage(slot), sem.at[slot])

    chunk(0, 0).start()
    for c in range(n // cols):
        slot = c % 2
        if c + 1 < n // cols:
            chunk(c + 1, 1 - slot).start()
        chunk(c, slot).wait()
        for k in range(a):
            w_ref[k * r:(k + 1) * r, c * cols:(c + 1) * cols] = stage(slot)[k].astype(_BF16)


def _lane_block_transpose(xs):
    nb = len(xs)
    diag = _diagonal_gather(xs)
    moved = [w if d == 0 else pltpu.roll(w, d * SSM_GROUP, axis=1) for d, w in enumerate(diag)]
    back = _diagonal_gather(moved)
    return [back[(-b) % nb] for b in range(nb)]


def _diagonal_gather(arrs):
    assert len(arrs) == 8
    c = lax.broadcasted_iota(jnp.int32, (1, LANES), 1) // SSM_GROUP
    c0, c1, c2 = c & 1, (c >> 1) & 1, (c >> 2) & 1
    s1 = {(k0, hi): jnp.where((c0 ^ k0) == 1, arrs[2 * hi + 1], arrs[2 * hi])
          for k0 in range(2) for hi in range(4)}
    s2 = {}
    for k0 in range(2):
        for k1 in range(2):
            e1 = c1 ^ k1 ^ (c0 & k0)
            for top in range(2):
                s2[(k0, k1, top)] = jnp.where(e1 == 1, s1[(k0, 2 * top + 1)], s1[(k0, 2 * top)])
    out = []
    for k in range(8):
        k0, k1, k2 = k & 1, (k >> 1) & 1, (k >> 2) & 1
        carry0 = c0 & k0
        carry1 = (c1 & k1) | (c1 & carry0) | (k1 & carry0)
        e2 = c2 ^ k2 ^ carry1
        out.append(jnp.where(e2 == 1, s2[(k0, k1, 1)], s2[(k0, k1, 0)]))
    return out


def _inproj_kernel(x_ref, g_ref, w_hbm, cw_ref, cb_ref, wglu_ref, wout_ref,
                   yc_ref, ug_ref, zs_ref, wglu_bf_ref, wout_bf_ref,
                   carry_ref, slab_ref, raw_ref, w_ref, sem, *, tiles_per_seq):
    tm = x_ref.shape[0]
    nb, sub = GROUPS_PER_BLOCK, SUBLANES
    slabs_per_block = COL_BLOCK // LANES
    n_blocks = D_CONV // COL_BLOCK
    bases = (4 * D_CONV, 0, D_CONV, 2 * D_CONV, 3 * D_CONV, 4 * D_CONV + D_SSM)

    @pl.when(pl.program_id(0) == 0)
    def _():
        _load_weight_as_bf16(w_hbm, w_ref, lambda slot: raw_ref.at[slot, pl.ds(0, w_hbm.shape[0])],
                             sem, COL_BLOCK)

    @pl.when(pl.program_id(0) % tiles_per_seq == 0)
    def _():
        carry_ref[...] = jnp.zeros_like(carry_ref)

    x = x_ref[...]
    ms = jnp.mean(x * x, axis=-1, keepdims=True)
    h = (x * lax.rsqrt(ms + EPS) * g_ref[...]).astype(_BF16)

    wglu_bf_ref[...] = wglu_ref[...].astype(_BF16)
    wout_bf_ref[...] = wout_ref[...].astype(_BF16)

    def project_one(blk, j):
        lo = bases[j] + blk * COL_BLOCK
        res = _dot(h, w_ref[:, lo:lo + COL_BLOCK])
        if j == 0:
            for k in range(slabs_per_block):
                slab_ref[blk * slabs_per_block + k] = res[:, k * LANES:(k + 1) * LANES]
        else:
            raw_ref[blk % 2, j - 1] = res

    def project(blk):
        return [functools.partial(project_one, blk, j) for j in range(len(bases))]

    def finish_gate(blk):
        sl = slice(blk * COL_BLOCK, (blk + 1) * COL_BLOCK)
        z_ssm = raw_ref[blk % 2, 4]
        zs_ref[:, sl] = (z_ssm * _sigmoid(z_ssm)).astype(_BF16)

    def finish_u(gv):
        rows = [slab_ref[gv, pl.ds(tl, tm // sub, stride=sub), :] for tl in range(sub)]
        for gl, o in enumerate(_lane_block_transpose(rows)):
            ug_ref[gv * nb + gl] = o

    def finish_conv(blk, half):
        sl = slice(blk * COL_BLOCK + half * LANES, blk * COL_BLOCK + (half + 1) * LANES)
        hl = slice(half * LANES, (half + 1) * LANES)
        b_gate, c_gate, v, z = (raw_ref[blk % 2, j, :, hl] for j in range(4))
        cv = c_gate * v
        ext = jnp.concatenate([carry_ref[:, sl], cv], axis=0)
        conv = (cb_ref[:, sl] + cw_ref[2:3, sl] * cv + cw_ref[1:2, sl] * ext[sub - 1:sub - 1 + tm]
                + cw_ref[0:1, sl] * ext[sub - 2:sub - 2 + tm])
        carry_ref[:, sl] = cv[tm - sub:]
        yc_ref[:, sl] = (b_gate * conv * (z * _sigmoid(z))).astype(_BF16)

    def finish(blk):
        return ([functools.partial(finish_gate, blk)]
                + [functools.partial(finish_u, blk * slabs_per_block + k) for k in range(slabs_per_block)]
                + [functools.partial(finish_conv, blk, half) for half in range(COL_BLOCK // LANES)])

    _interleave(project(0), [])
    for blk in range(1, n_blocks):
        _interleave(project(blk), finish(blk - 1))
    _interleave([], finish(n_blocks - 1))


def _inproj(x2d, norm_g, w_in, conv_w, conv_b, w_glu, w_out, seq_len):
    n_tok = x2d.shape[0]
    tm = TOKEN_TILE
    steps = n_tok // tm
    const = lambda s: (0, 0)
    tile = lambda w: pl.BlockSpec((tm, w), lambda s: (s, 0))
    row_block = lambda w: pl.BlockSpec((w.shape[0] // steps, w.shape[1]), lambda s: (s, 0))
    out_sds = jax.ShapeDtypeStruct((n_tok, D_CONV), _BF16)
    return pl.pallas_call(
        functools.partial(_inproj_kernel, tiles_per_seq=seq_len // tm),
        grid=(steps,),
        in_specs=[
            tile(D_MODEL),
            pl.BlockSpec((1, D_MODEL), const),
            pl.BlockSpec(memory_space=pl.ANY),
            pl.BlockSpec((3, D_CONV), const),
            pl.BlockSpec((1, D_CONV), const),
            row_block(w_glu), row_block(w_out),
        ],
        out_specs=[tile(D_CONV),
                   pl.BlockSpec((SSM_GROUPS, tm // SUBLANES, LANES), lambda s: (0, s, 0)),
                   tile(D_SSM),
                   row_block(w_glu), row_block(w_out)],
        out_shape=[out_sds,
                   jax.ShapeDtypeStruct((SSM_GROUPS, n_tok // SUBLANES, LANES), _F32),
                   out_sds,
                   jax.ShapeDtypeStruct(w_glu.shape, _BF16),
                   jax.ShapeDtypeStruct(w_out.shape, _BF16)],
        scratch_shapes=[pltpu.VMEM((SUBLANES, D_CONV), _F32),
                        pltpu.VMEM((D_SSM // LANES, tm, LANES), _F32),
                        pltpu.VMEM((2, 5, tm, COL_BLOCK), _F32),
                        pltpu.VMEM((D_MODEL, N_IN), _BF16),
                        pltpu.SemaphoreType.DMA((2,))],
        compiler_params=pltpu.CompilerParams(dimension_semantics=("arbitrary",),
                                             vmem_limit_bytes=VMEM_LIMIT_BYTES),
        name="inproj_conv",
    )(x2d, norm_g, w_in.reshape(D_MODEL // tm, tm, N_IN), conv_w, conv_b, w_glu, w_out)


def _complex_powers(ar, ai, exponent, nbits):
    pr = jnp.ones((exponent.shape[0], ar.shape[1]), _F32)
    pi = jnp.zeros_like(pr)
    cr, ci = ar, ai
    for b in range(nbits):
        bit = ((exponent >> b) & 1) == 1
        fr = jnp.where(bit, cr, 1.0)
        fi = jnp.where(bit, ci, 0.0)
        pr, pi = pr * fr - pi * fi, pr * fi + pi * fr
        cr, ci = cr * cr - ci * ci, 2.0 * cr * ci
    return pr, pi


def _re_im_sign():
    lane = lax.broadcasted_iota(jnp.int32, (1, LANES), 1)
    return jnp.where(lane < SSM_STATE, -1.0, 1.0).astype(_F32)


def _s5_prepare_group(g, rows_ref, mats_ref, kf_ref, ws_ref, wo_ref, a1_ref, a2_ref):
    t, th, jrows, h = S5_CHUNK, S5_TH, S5_POW_ROWS, SSM_GROUP
    sgn = _re_im_sign()
    nbits = max(1, (jrows - 1).bit_length())
    j = lax.broadcasted_iota(jnp.int32, (jrows, 1), 0)
    diag = (lax.broadcasted_iota(jnp.int32, (h, th), 0) == lax.broadcasted_iota(jnp.int32, (h, th), 1))
    lr, li, ldt, d_row = (rows_ref[g, :, 0:LANES], rows_ref[g, :, LANES:2 * LANES],
                          rows_ref[g, :, 2 * LANES:3 * LANES], rows_ref[g, :, 3 * LANES:])
    bt1, bt2, ct1, ct2 = (mats_ref[g, :, k * LANES:(k + 1) * LANES] for k in range(4))
    dt = jnp.exp(ldt)
    mag = jnp.exp(lr * dt)
    ar, ai = mag * jnp.cos(li * dt), mag * jnp.sin(li * dt)
    nr, ni = ar - 1.0, ai
    den = lr * lr + li * li
    qr = (nr * lr + ni * li) / den
    qi = (ni * lr - nr * li) / den
    b1 = qr * bt1 + (qi * sgn) * bt2
    b2 = (qr * sgn) * bt2 - qi * bt1
    c1 = ct1 * (-sgn)
    c2 = -ct2
    pr, pi = _complex_powers(ar, ai, j, nbits)
    rr, ri = _complex_powers(ar, ai, jnp.maximum(t - 1 - j, 0), nbits)
    cps = (pr[:, None, :] * c1[None] + pi[:, None, :] * c2[None]).reshape(jrows * h, LANES)
    cps = cps.astype(_BF16)
    b_hi = b1.astype(_BF16)
    b_lo = (b1 - b_hi.astype(_F32)).astype(_BF16)
    kf_ref[g] = _dot_nt(b_hi, cps[:th]) + _dot_nt(b_lo, cps[:th]) + jnp.where(diag, d_row, 0.0)
    wo_ref[g] = cps[h:h + th]
    ws_ref[g] = (rr[:t, None, :] * b1[None] + ri[:t, None, :] * b2[None]).reshape(th, LANES).astype(_BF16)
    a1_ref[g] = pr[t:t + 1]
    a2_ref[g] = pi[t:t + 1]


def _s5_kernel(ug_ref, rows_ref, mats_ref, yg_ref, kf_ref, ws_ref, wo_ref, a1_ref, a2_ref,
               mt_ref, xg_ref, yi_ref, sl_ref, slsw_ref, sin_ref, *, chunks_per_seq):
    nb = ug_ref.shape[0]
    t, th, h = S5_CHUNK, S5_TH, SSM_GROUP
    t_hi_n = t // SUBLANES
    n_chunks = ug_ref.shape[1] // t_hi_n
    n_seq = n_chunks // chunks_per_seq
    pitch = S5_SCAN_PITCH
    assert n_seq * nb <= pitch

    for g in range(nb):
        _s5_prepare_group(g, rows_ref, mats_ref, kf_ref, ws_ref, wo_ref, a1_ref, a2_ref)
        xg = jnp.concatenate([ug_ref[g, pl.ds(k, n_chunks, stride=t_hi_n), :]
                              for k in range(t_hi_n)], axis=1).astype(_BF16)
        xg_ref[g] = xg
        s = _dot(xg, ws_ref[g])
        s_sw = pltpu.roll(s, SSM_STATE, axis=1)
        for b in range(n_seq):
            rows = slice(b * chunks_per_seq, (b + 1) * chunks_per_seq)
            sl_ref[pl.ds(b * nb + g, chunks_per_seq, stride=pitch), :] = s[rows]
            slsw_ref[pl.ds(b * nb + g, chunks_per_seq, stride=pitch), :] = s_sw[rows]

    a1 = jnp.concatenate([a1_ref[g] for g in range(nb)] * n_seq, axis=0)
    a2 = jnp.concatenate([a2_ref[g] for g in range(nb)] * n_seq, axis=0) * _re_im_sign()
    s = jnp.zeros((n_seq * nb, LANES), _F32)
    s_sw = s
    for c in range(chunks_per_seq):
        sin_ref[c * pitch:c * pitch + n_seq * nb, :] = s
        s, s_sw = (a1 * s + a2 * s_sw + sl_ref[c * pitch:c * pitch + n_seq * nb, :],
                   a1 * s_sw - a2 * s + slsw_ref[c * pitch:c * pitch + n_seq * nb, :])

    for g in range(nb):
        zk = jnp.concatenate([jnp.zeros((h, th), _F32), kf_ref[g]], axis=1)
        for k in range(t):
            mt_ref[g, k * h:(k + 1) * h, :] = zk[:, th - k * h:2 * th - k * h].astype(_BF16)
        yi_ref[g] = _dot(xg_ref[g], mt_ref[g])

    for g in range(nb):
        s_in = jnp.concatenate([sin_ref[pl.ds(b * nb + g, chunks_per_seq, stride=pitch), :]
                                for b in range(n_seq)], axis=0).astype(_BF16)
        y = yi_ref[g] + _dot_nt(s_in, wo_ref[g])
        for k in range(t_hi_n):
            yg_ref[g, pl.ds(k, n_chunks, stride=t_hi_n), :] = y[:, k * LANES:(k + 1) * LANES]


def _s5_core(ug, rows, mats, chunks_per_seq):
    _, n_rows, _ = ug.shape
    h, th, nb = SSM_GROUP, S5_TH, GROUPS_PER_BLOCK
    group_rows = pl.BlockSpec((nb, n_rows, LANES), lambda i: (i, 0, 0))
    return pl.pallas_call(
        functools.partial(_s5_kernel, chunks_per_seq=chunks_per_seq),
        grid=(SSM_GROUPS // nb,),
        in_specs=[group_rows,
                  pl.BlockSpec((nb, 1, rows.shape[-1]), lambda i: (i, 0, 0)),
                  pl.BlockSpec((nb, h, mats.shape[-1]), lambda i: (i, 0, 0))],
        out_specs=group_rows,
        out_shape=jax.ShapeDtypeStruct(ug.shape, _F32),
        scratch_shapes=[pltpu.VMEM((nb, h, th), _F32),
                        pltpu.VMEM((nb, th, LANES), _BF16),
                        pltpu.VMEM((nb, th, LANES), _BF16),
                        pltpu.VMEM((nb, 1, LANES), _F32),
                        pltpu.VMEM((nb, 1, LANES), _F32),
                        pltpu.VMEM((nb, th, th), _BF16),
                        pltpu.VMEM((nb, n_rows * SUBLANES // S5_CHUNK, th), _BF16),
                        pltpu.VMEM((nb, n_rows * SUBLANES // S5_CHUNK, th), _F32)]
        + [pltpu.VMEM((chunks_per_seq * S5_SCAN_PITCH, LANES), _F32)] * 3,
        compiler_params=pltpu.CompilerParams(dimension_semantics=("arbitrary",),
                                             vmem_limit_bytes=VMEM_LIMIT_BYTES),
        name="s5_core",
    )(ug, rows, mats)


def _out_kernel(yc_ref, yg_ref, zs_ref, x_ref, wg_ref, bg_ref, wo_ref, gp_ref, o_ref, slab_ref, ys_ref):
    tm = x_ref.shape[0]
    nb, sub = GROUPS_PER_BLOCK, SUBLANES
    rp = tm // OUT_ROW_PARTS
    slabs_per_block = COL_BLOCK // LANES

    def activation(part):
        rows = slice(part * rp, (part + 1) * rp)
        grows = slice(part * rp // sub, (part + 1) * rp // sub)
        acc = {"pre": bg_ref[...], "y": []}

        def block(blk):
            gvs = range(blk * slabs_per_block, (blk + 1) * slabs_per_block)
            for gv in gvs:
                tok = _lane_block_transpose([yg_ref[gv * nb + gl, grows, :] for gl in range(nb)])
                for tl in range(sub):
                    slab_ref[gv, pl.ds(part * rp + tl, rp // sub, stride=sub), :] = tok[tl]
            y = jax.nn.gelu(jnp.concatenate([slab_ref[gv, rows, :] for gv in gvs], axis=1))
            acc["y"].append(y)
            acc["pre"] = acc["pre"] + _dot(y.astype(_BF16), wg_ref[blk * COL_BLOCK:(blk + 1) * COL_BLOCK, :])

        def gate():
            y = jnp.concatenate(acc["y"], axis=1)
            ys_ref[part] = (y * _sigmoid(acc["pre"]) * zs_ref[rows, :].astype(_F32)).astype(_BF16)

        return [functools.partial(block, blk) for blk in range(D_SSM // COL_BLOCK)] + [gate]

    def conv_projection(part):
        rows = slice(part * rp, (part + 1) * rp)

        def cols(c):
            sl = slice(c * OUT_COL_BLOCK, (c + 1) * OUT_COL_BLOCK)
            o_ref[rows, sl] = _dot(yc_ref[rows, :], wo_ref[:D_CONV, sl])

        return [functools.partial(cols, c) for c in range(D_MODEL // OUT_COL_BLOCK)]

    def ssm_projection(part):
        rows = slice(part * rp, (part + 1) * rp)

        def cols(c):
            sl = slice(c * OUT_COL_BLOCK, (c + 1) * OUT_COL_BLOCK)
            o_ref[rows, sl] += _dot(ys_ref[part], wo_ref[D_CONV:, sl])

        return [functools.partial(cols, c) for c in range(D_MODEL // OUT_COL_BLOCK)]

    def norm_residual(part):
        rows = slice(part * rp, (part + 1) * rp)
        o = o_ref[rows, :]
        ms = jnp.mean(o * o, axis=-1, keepdims=True)
        o_ref[rows, :] = x_ref[rows, :] + o * lax.rsqrt(ms + EPS) * gp_ref[...]

    for part in range(OUT_ROW_PARTS + 1):
        mxu, vpu = [], []
        if part > 0:
            mxu += ssm_projection(part - 1)
        if part > 1:
            vpu.append(functools.partial(norm_residual, part - 2))
        if part < OUT_ROW_PARTS:
            mxu += conv_projection(part)
            vpu += activation(part)
        _interleave(mxu, vpu)
    norm_residual(OUT_ROW_PARTS - 1)


def _out_proj(yc, yg, zs, x2d, w_glu_bf16, b_glu, w_out_bf16, norm_g):
    n_tok = x2d.shape[0]
    tm = TOKEN_TILE
    const = lambda i: (0, 0)
    tile = lambda w: pl.BlockSpec((tm, w), lambda i: (i, 0))
    return pl.pallas_call(
        _out_kernel,
        grid=(n_tok // tm,),
        in_specs=[
            tile(D_CONV),
            pl.BlockSpec((SSM_GROUPS, tm // SUBLANES, LANES), lambda i: (0, i, 0)),
            tile(D_SSM), tile(D_MODEL),
            pl.BlockSpec((D_SSM, D_SSM), const, pipeline_mode=pl.Buffered(1)),
            pl.BlockSpec((1, D_SSM), const),
            pl.BlockSpec((D_CONV + D_SSM, D_MODEL), const, pipeline_mode=pl.Buffered(1)),
            pl.BlockSpec((1, D_MODEL), const),
        ],
        out_specs=tile(D_MODEL),
        out_shape=jax.ShapeDtypeStruct((n_tok, D_MODEL), _F32),
        scratch_shapes=[pltpu.VMEM((D_SSM // LANES, tm, LANES), _F32),
                        pltpu.VMEM((OUT_ROW_PARTS, tm // OUT_ROW_PARTS, D_SSM), _BF16)],
        compiler_params=pltpu.CompilerParams(dimension_semantics=("arbitrary",),
                                             vmem_limit_bytes=VMEM_LIMIT_BYTES),
        name="glu_outproj",
    )(yc, yg, zs, x2d, w_glu_bf16, b_glu, w_out_bf16, norm_g)


def kernel(x, norm_pre_g, w_in, conv_w, conv_b, ssm_a_re, ssm_a_im, ssm_log_dt, ssm_b_re, ssm_b_im,
           ssm_c_re, ssm_c_im, ssm_d, w_glu, b_glu, w_out, norm_post_g):
    bsz, seq_len, _ = x.shape
    n_tok = bsz * seq_len
    g, p, h, t = SSM_GROUPS, SSM_STATE, SSM_GROUP, S5_CHUNK
    assert seq_len % TOKEN_TILE == 0 and TOKEN_TILE % t == 0
    x2d = x.reshape(n_tok, D_MODEL)

    yc, ug, zs, w_glu_bf16, w_out_bf16 = _inproj(
        x2d, norm_pre_g.reshape(1, -1), w_in, conv_w, conv_b.reshape(1, -1), w_glu, w_out, seq_len)

    brt, bit = ssm_b_re.transpose(0, 2, 1), ssm_b_im.transpose(0, 2, 1)
    a_re, a_im = ssm_a_re.reshape(g, 1, p), ssm_a_im.reshape(g, 1, p)
    rows = jnp.concatenate(
        [a_re, a_re, a_im, a_im, jnp.broadcast_to(ssm_log_dt.reshape(g, 1, 1), (g, 1, LANES)),
         ssm_d.reshape(g, 1, h), jnp.zeros((g, 1, S5_TH - h), _F32)], axis=-1)
    mats = jnp.concatenate([brt, bit, bit, brt, ssm_c_re, ssm_c_im, ssm_c_im, ssm_c_re], axis=-1)
    yg = _s5_core(ug, rows, mats, seq_len // t)

    out = _out_proj(yc, yg, zs, x2d, w_glu_bf16, b_glu.reshape(1, -1), w_out_bf16,
                    norm_post_g.reshape(1, -1))
    return out.reshape(bsz, seq_len, D_MODEL)
```

```python
import functools

import jax
import jax.numpy as jnp
from jax import lax
from jax.experimental import pallas as pl
from jax.experimental.pallas import tpu as pltpu

D_MODEL = 2048
D_CONV = 1024
D_SSM = 1024
SSM_GROUP = 16
SSM_GROUPS = 64
SSM_STATE = 64
N_IN = 4 * D_CONV + 2 * D_SSM
EPS = 1e-6

S5_CHUNK = 16
S5_TH = S5_CHUNK * SSM_GROUP
S5_POW_ROWS = S5_CHUNK + 8
S5_SCAN_PITCH = 24
S5_SCAN_RADIX = 4
LANES = 128
SUBLANES = 8
GROUPS_PER_BLOCK = LANES // SSM_GROUP
TOKEN_TILE = 512
COL_BLOCK = 256
OUT_ROW_PARTS = 2
OUT_COL_BLOCK = 512
VMEM_LIMIT_BYTES = 56 * 1024 * 1024

_F32 = jnp.float32
_BF16 = jnp.bfloat16


def _dot(a, b):
    return jnp.dot(a, b, preferred_element_type=_F32)


def _dot_nt(a, b):
    return lax.dot_general(a, b, (((1,), (1,)), ((), ())), preferred_element_type=_F32)


def _sigmoid(z):
    return 1.0 / (1.0 + jnp.exp(-z))


def _interleave(mxu_items, vpu_items):
    n = max(len(mxu_items), 1)
    done = 0
    for i, item in enumerate(mxu_items):
        item()
        upto = -(-len(vpu_items) * (i + 1) // n)
        for piece in vpu_items[done:upto]:
            piece()
        done = upto
    for piece in vpu_items[done:]:
        piece()


def _load_weight_as_bf16(w_hbm, w_ref, stage, sem, cols):
    a, r, n = w_hbm.shape

    def chunk(c, slot):
        return pltpu.make_async_copy(w_hbm.at[:, :, pl.ds(c * cols, cols)], stage(slot), sem.at[slot])

    chunk(0, 0).start()
    for c in range(n // cols):
        slot = c % 2
        if c + 1 < n // cols:
            chunk(c + 1, 1 - slot).start()
        chunk(c, slot).wait()
        for k in range(a):
            w_ref[k * r:(k + 1) * r, c * cols:(c + 1) * cols] = stage(slot)[k].astype(_BF16)


def _lane_block_transpose(xs):
    nb = len(xs)
    diag = _diagonal_gather(xs)
    moved = [w if d == 0 else pltpu.roll(w, d * SSM_GROUP, axis=1) for d, w in enumerate(diag)]
    back = _diagonal_gather(moved)
    return [back[(-b) % nb] for b in range(nb)]


def _diagonal_gather(arrs):
    assert len(arrs) == 8
    c = lax.broadcasted_iota(jnp.int32, (1, LANES), 1) // SSM_GROUP
    c0, c1, c2 = c & 1, (c >> 1) & 1, (c >> 2) & 1
    s1 = {(k0, hi): jnp.where((c0 ^ k0) == 1, arrs[2 * hi + 1], arrs[2 * hi])
          for k0 in range(2) for hi in range(4)}
    s2 = {}
    for k0 in range(2):
        for k1 in range(2):
            e1 = c1 ^ k1 ^ (c0 & k0)
            for top in range(2):
                s2[(k0, k1, top)] = jnp.where(e1 == 1, s1[(k0, 2 * top + 1)], s1[(k0, 2 * top)])
    out = []
    for k in range(8):
        k0, k1, k2 = k & 1, (k >> 1) & 1, (k >> 2) & 1
        carry0 = c0 & k0
        carry1 = (c1 & k1) | (c1 & carry0) | (k1 & carry0)
        e2 = c2 ^ k2 ^ carry1
        out.append(jnp.where(e2 == 1, s2[(k0, k1, 1)], s2[(k0, k1, 0)]))
    return out


def _inproj_kernel(x_ref, g_ref, w_hbm, cw_ref, cb_ref, wglu_ref, wout_ref,
                   yc_ref, ug_ref, zs_ref, wglu_bf_ref, wout_bf_ref,
                   carry_ref, slab_ref, raw_ref, w_ref, sem, *, tiles_per_seq):
    tm = x_ref.shape[0]
    nb, sub = GROUPS_PER_BLOCK, SUBLANES
    slabs_per_block = COL_BLOCK // LANES
    n_blocks = D_CONV // COL_BLOCK
    bases = (4 * D_CONV, 0, D_CONV, 2 * D_CONV, 3 * D_CONV, 4 * D_CONV + D_SSM)

    @pl.when(pl.program_id(0) == 0)
    def _():
        _load_weight_as_bf16(w_hbm, w_ref, lambda slot: raw_ref.at[slot, pl.ds(0, w_hbm.shape[0])],
                             sem, COL_BLOCK)

    @pl.when(pl.program_id(0) % tiles_per_seq == 0)
    def _():
        carry_ref[...] = jnp.zeros_like(carry_ref)

    x = x_ref[...]
    ms = jnp.mean(x * x, axis=-1, keepdims=True)
    h = (x * lax.rsqrt(ms + EPS) * g_ref[...]).astype(_BF16)

    wglu_bf_ref[...] = wglu_ref[...].astype(_BF16)
    wout_bf_ref[...] = wout_ref[...].astype(_BF16)

    def project_one(blk, j):
        lo = bases[j] + blk * COL_BLOCK
        res = _dot(h, w_ref[:, lo:lo + COL_BLOCK])
        if j == 0:
            for k in range(slabs_per_block):
                slab_ref[blk * slabs_per_block + k] = res[:, k * LANES:(k + 1) * LANES]
        else:
            raw_ref[blk % 2, j - 1] = res

    def project(blk):
        return [functools.partial(project_one, blk, j) for j in range(len(bases))]

    def finish_gate(blk):
        sl = slice(blk * COL_BLOCK, (blk + 1) * COL_BLOCK)
        z_ssm = raw_ref[blk % 2, 4]
        zs_ref[:, sl] = (z_ssm * _sigmoid(z_ssm)).astype(_BF16)

    def finish_u(gv):
        rows = [slab_ref[gv, pl.ds(tl, tm // sub, stride=sub), :] for tl in range(sub)]
        for gl, o in enumerate(_lane_block_transpose(rows)):
            ug_ref[gv * nb + gl] = o

    def finish_conv(blk, half):
        sl = slice(blk * COL_BLOCK + half * LANES, blk * COL_BLOCK + (half + 1) * LANES)
        hl = slice(half * LANES, (half + 1) * LANES)
        b_gate, c_gate, v, z = (raw_ref[blk % 2, j, :, hl] for j in range(4))
        cv = c_gate * v
        ext = jnp.concatenate([carry_ref[:, sl], cv], axis=0)
        conv = (cb_ref[:, sl] + cw_ref[2:3, sl] * cv + cw_ref[1:2, sl] * ext[sub - 1:sub - 1 + tm]
                + cw_ref[0:1, sl] * ext[sub - 2:sub - 2 + tm])
        carry_ref[:, sl] = cv[tm - sub:]
        yc_ref[:, sl] = (b_gate * conv * (z * _sigmoid(z))).astype(_BF16)

    def finish(blk):
        return ([functools.partial(finish_gate, blk)]
                + [functools.partial(finish_u, blk * slabs_per_block + k) for k in range(slabs_per_block)]
                + [functools.partial(finish_conv, blk, half) for half in range(COL_BLOCK // LANES)])

    _interleave(project(0), [])
    for blk in range(1, n_blocks):
        _interleave(project(blk), finish(blk - 1))
    _interleave([], finish(n_blocks - 1))


def _inproj(x2d, norm_g, w_in, conv_w, conv_b, w_glu, w_out, seq_len):
    n_tok = x2d.shape[0]
    tm = TOKEN_TILE
    steps = n_tok // tm
    const = lambda s: (0, 0)
    tile = lambda w: pl.BlockSpec((tm, w), lambda s: (s, 0))
    row_block = lambda w: pl.BlockSpec((w.shape[0] // steps, w.shape[1]), lambda s: (s, 0))
    out_sds = jax.ShapeDtypeStruct((n_tok, D_CONV), _BF16)
    return pl.pallas_call(
        functools.partial(_inproj_kernel, tiles_per_seq=seq_len // tm),
        grid=(steps,),
        in_specs=[
            tile(D_MODEL),
            pl.BlockSpec((1, D_MODEL), const),
            pl.BlockSpec(memory_space=pl.ANY),
            pl.BlockSpec((3, D_CONV), const),
            pl.BlockSpec((1, D_CONV), const),
            row_block(w_glu), row_block(w_out),
        ],
        out_specs=[tile(D_CONV),
                   pl.BlockSpec((SSM_GROUPS, tm // SUBLANES, LANES), lambda s: (0, s, 0)),
                   tile(D_SSM),
                   row_block(w_glu), row_block(w_out)],
        out_shape=[out_sds,
                   jax.ShapeDtypeStruct((SSM_GROUPS, n_tok // SUBLANES, LANES), _F32),
                   out_sds,
                   jax.ShapeDtypeStruct(w_glu.shape, _BF16),
                   jax.ShapeDtypeStruct(w_out.shape, _BF16)],
        scratch_shapes=[pltpu.VMEM((SUBLANES, D_CONV), _F32),
                        pltpu.VMEM((D_SSM // LANES, tm, LANES), _F32),
                        pltpu.VMEM((2, 5, tm, COL_BLOCK), _F32),
                        pltpu.VMEM((D_MODEL, N_IN), _BF16),
                        pltpu.SemaphoreType.DMA((2,))],
        compiler_params=pltpu.CompilerParams(dimension_semantics=("arbitrary",),
                                             vmem_limit_bytes=VMEM_LIMIT_BYTES),
        name="inproj_conv",
    )(x2d, norm_g, w_in.reshape(D_MODEL // tm, tm, N_IN), conv_w, conv_b, w_glu, w_out)


def _complex_powers(ar, ai, exponent, nbits):
    pr = jnp.ones((exponent.shape[0], ar.shape[1]), _F32)
    pi = jnp.zeros_like(pr)
    cr, ci = ar, ai
    for b in range(nbits):
        bit = ((exponent >> b) & 1) == 1
        fr = jnp.where(bit, cr, 1.0)
        fi = jnp.where(bit, ci, 0.0)
        pr, pi = pr * fr - pi * fi, pr * fi + pi * fr
        cr, ci = cr * cr - ci * ci, 2.0 * cr * ci
    return pr, pi


def _re_im_sign():
    lane = lax.broadcasted_iota(jnp.int32, (1, LANES), 1)
    return jnp.where(lane < SSM_STATE, -1.0, 1.0).astype(_F32)


def _s5_prepare_group(g, rows_ref, mats_ref, kf_ref, ws_ref, wo_ref, a1_ref, a2_ref):
    t, th, jrows, h = S5_CHUNK, S5_TH, S5_POW_ROWS, SSM_GROUP
    sgn = _re_im_sign()
    nbits = max(1, (jrows - 1).bit_length())
    j = lax.broadcasted_iota(jnp.int32, (jrows, 1), 0)
    diag = (lax.broadcasted_iota(jnp.int32, (h, th), 0) == lax.broadcasted_iota(jnp.int32, (h, th), 1))
    lr, li, ldt, d_row = (rows_ref[g, :, 0:LANES], rows_ref[g, :, LANES:2 * LANES],
                          rows_ref[g, :, 2 * LANES:3 * LANES], rows_ref[g, :, 3 * LANES:])
    bt1, bt2, ct1, ct2 = (mats_ref[g, :, k * LANES:(k + 1) * LANES] for k in range(4))
    dt = jnp.exp(ldt)
    mag = jnp.exp(lr * dt)
    ar, ai = mag * jnp.cos(li * dt), mag * jnp.sin(li * dt)
    nr, ni = ar - 1.0, ai
    den = lr * lr + li * li
    qr = (nr * lr + ni * li) / den
    qi = (ni * lr - nr * li) / den
    b1 = qr * bt1 + (qi * sgn) * bt2
    b2 = (qr * sgn) * bt2 - qi * bt1
    c1 = ct1 * (-sgn)
    c2 = -ct2
    pr, pi = _complex_powers(ar, ai, j, nbits)
    rr, ri = _complex_powers(ar, ai, jnp.maximum(t - 1 - j, 0), nbits)
    cps = (pr[:, None, :] * c1[None] + pi[:, None, :] * c2[None]).reshape(jrows * h, LANES)
    cps = cps.astype(_BF16)
    b_hi = b1.astype(_BF16)
    b_lo = (b1 - b_hi.astype(_F32)).astype(_BF16)
    kf_ref[g] = _dot_nt(b_hi, cps[:th]) + _dot_nt(b_lo, cps[:th]) + jnp.where(diag, d_row, 0.0)
    wo_ref[g] = cps[h:h + th]
    ws_ref[g] = (rr[:t, None, :] * b1[None] + ri[:t, None, :] * b2[None]).reshape(th, LANES).astype(_BF16)
    a1_ref[g] = pr[t:t + 1]
    a2_ref[g] = pi[t:t + 1]


def _s5_kernel(ug_ref, rows_ref, mats_ref, yg_ref, kf_ref, ws_ref, wo_ref, a1_ref, a2_ref,
               mt_ref, xg_ref, yi_ref, sl_ref, slsw_ref, sin_ref, *, chunks_per_seq):
    nb = ug_ref.shape[0]
    t, th, h = S5_CHUNK, S5_TH, SSM_GROUP
    t_hi_n = t // SUBLANES
    n_chunks = ug_ref.shape[1] // t_hi_n
    n_seq = n_chunks // chunks_per_seq
    pitch = S5_SCAN_PITCH
    assert n_seq * nb <= pitch

    for g in range(nb):
        _s5_prepare_group(g, rows_ref, mats_ref, kf_ref, ws_ref, wo_ref, a1_ref, a2_ref)
        xg = jnp.concatenate([ug_ref[g, pl.ds(k, n_chunks, stride=t_hi_n), :]
                              for k in range(t_hi_n)], axis=1).astype(_BF16)
        xg_ref[g] = xg
        s = _dot(xg, ws_ref[g])
        s_sw = pltpu.roll(s, SSM_STATE, axis=1)
        for b in range(n_seq):
            rows = slice(b * chunks_per_seq, (b + 1) * chunks_per_seq)
            sl_ref[pl.ds(b * nb + g, chunks_per_seq, stride=pitch), :] = s[rows]
            slsw_ref[pl.ds(b * nb + g, chunks_per_seq, stride=pitch), :] = s_sw[rows]

    a1 = jnp.concatenate([a1_ref[g] for g in range(nb)] * n_seq, axis=0)
    a2 = jnp.concatenate([a2_ref[g] for g in range(nb)] * n_seq, axis=0) * _re_im_sign()
    n_rows = n_seq * nb

    def times(a, z):
        return a[0] * z[0] + a[1] * z[1], a[0] * z[1] - a[1] * z[0]

    def plus(z, w):
        return z[0] + w[0], z[1] + w[1]

    def local(c):
        return (sl_ref[c * pitch:c * pitch + n_rows, :], slsw_ref[c * pitch:c * pitch + n_rows, :])

    radix = S5_SCAN_RADIX
    powers = [(a1, a2)]
    for _ in range(radix - 1):
        pr, pi_signed = powers[-1]
        sgn = _re_im_sign()
        ar, ai, qr, qi = a1, a2 * sgn, pr, pi_signed * sgn
        powers.append((qr * ar - qi * ai, (qr * ai + qi * ar) * sgn))
    s = (jnp.zeros((n_rows, LANES), _F32), jnp.zeros((n_rows, LANES), _F32))
    for c0 in range(0, chunks_per_seq, radix):
        partial = None
        for j in range(radix):
            if j == 0:
                entering = s[0]
            else:
                entering = powers[j - 1][0] * s[0] + powers[j - 1][1] * s[1] + partial[0]
            sin_ref[(c0 + j) * pitch:(c0 + j) * pitch + n_rows, :] = entering
            partial = local(c0 + j) if partial is None else plus(times(powers[0], partial), local(c0 + j))
        s = plus(times(powers[radix - 1], s), partial)

    for g in range(nb):
        zk = jnp.concatenate([jnp.zeros((h, th), _F32), kf_ref[g]], axis=1)
        for k in range(t):
            mt_ref[g, k * h:(k + 1) * h, :] = zk[:, th - k * h:2 * th - k * h].astype(_BF16)
        yi_ref[g] = _dot(xg_ref[g], mt_ref[g])

    for g in range(nb):
        s_in = jnp.concatenate([sin_ref[pl.ds(b * nb + g, chunks_per_seq, stride=pitch), :]
                                for b in range(n_seq)], axis=0).astype(_BF16)
        y = yi_ref[g] + _dot_nt(s_in, wo_ref[g])
        for k in range(t_hi_n):
            yg_ref[g, pl.ds(k, n_chunks, stride=t_hi_n), :] = y[:, k * LANES:(k + 1) * LANES]


def _s5_core(ug, rows, mats, chunks_per_seq):
    _, n_rows, _ = ug.shape
    h, th, nb = SSM_GROUP, S5_TH, GROUPS_PER_BLOCK
    group_rows = pl.BlockSpec((nb, n_rows, LANES), lambda i: (i, 0, 0))
    return pl.pallas_call(
        functools.partial(_s5_kernel, chunks_per_seq=chunks_per_seq),
        grid=(SSM_GROUPS // nb,),
        in_specs=[group_rows,
                  pl.BlockSpec((nb, 1, rows.shape[-1]), lambda i: (i, 0, 0)),
                  pl.BlockSpec((nb, h, mats.shape[-1]), lambda i: (i, 0, 0))],
        out_specs=group_rows,
        out_shape=jax.ShapeDtypeStruct(ug.shape, _F32),
        scratch_shapes=[pltpu.VMEM((nb, h, th), _F32),
                        pltpu.VMEM((nb, th, LANES), _BF16),
                        pltpu.VMEM((nb, th, LANES), _BF16),
                        pltpu.VMEM((nb, 1, LANES), _F32),
                        pltpu.VMEM((nb, 1, LANES), _F32),
                        pltpu.VMEM((nb, th, th), _BF16),
                        pltpu.VMEM((nb, n_rows * SUBLANES // S5_CHUNK, th), _BF16),
                        pltpu.VMEM((nb, n_rows * SUBLANES // S5_CHUNK, th), _F32)]
        + [pltpu.VMEM((chunks_per_seq * S5_SCAN_PITCH, LANES), _F32)] * 3,
        compiler_params=pltpu.CompilerParams(dimension_semantics=("arbitrary",),
                                             vmem_limit_bytes=VMEM_LIMIT_BYTES),
        name="s5_core",
    )(ug, rows, mats)


def _out_kernel(yc_ref, yg_ref, zs_ref, x_ref, wg_ref, bg_ref, wo_ref, gp_ref, o_ref, slab_ref, ys_ref):
    tm = x_ref.shape[0]
    nb, sub = GROUPS_PER_BLOCK, SUBLANES
    rp = tm // OUT_ROW_PARTS
    slabs_per_block = COL_BLOCK // LANES

    def activation(part):
        rows = slice(part * rp, (part + 1) * rp)
        grows = slice(part * rp // sub, (part + 1) * rp // sub)
        acc = {"pre": bg_ref[...], "y": []}

        def block(blk):
            gvs = range(blk * slabs_per_block, (blk + 1) * slabs_per_block)
            for gv in gvs:
                tok = _lane_block_transpose([yg_ref[gv * nb + gl, grows, :] for gl in range(nb)])
                for tl in range(sub):
                    slab_ref[gv, pl.ds(part * rp + tl, rp // sub, stride=sub), :] = tok[tl]
            y = jax.nn.gelu(jnp.concatenate([slab_ref[gv, rows, :] for gv in gvs], axis=1))
            acc["y"].append(y)
            acc["pre"] = acc["pre"] + _dot(y.astype(_BF16), wg_ref[blk * COL_BLOCK:(blk + 1) * COL_BLOCK, :])

        def gate():
            y = jnp.concatenate(acc["y"], axis=1)
            ys_ref[part] = (y * _sigmoid(acc["pre"]) * zs_ref[rows, :].astype(_F32)).astype(_BF16)

        return [functools.partial(block, blk) for blk in range(D_SSM // COL_BLOCK)] + [gate]

    def conv_projection(part):
        rows = slice(part * rp, (part + 1) * rp)

        def cols(c):
            sl = slice(c * OUT_COL_BLOCK, (c + 1) * OUT_COL_BLOCK)
            o_ref[rows, sl] = _dot(yc_ref[rows, :], wo_ref[:D_CONV, sl])

        return [functools.partial(cols, c) for c in range(D_MODEL // OUT_COL_BLOCK)]

    def ssm_projection(part):
        rows = slice(part * rp, (part + 1) * rp)

        def cols(c):
            sl = slice(c * OUT_COL_BLOCK, (c + 1) * OUT_COL_BLOCK)
            o_ref[rows, sl] += _dot(ys_ref[part], wo_ref[D_CONV:, sl])

        return [functools.partial(cols, c) for c in range(D_MODEL // OUT_COL_BLOCK)]

    def norm_residual(part):
        rows = slice(part * rp, (part + 1) * rp)
        o = o_ref[rows, :]
        ms = jnp.mean(o * o, axis=-1, keepdims=True)
        o_ref[rows, :] = x_ref[rows, :] + o * lax.rsqrt(ms + EPS) * gp_ref[...]

    for part in range(OUT_ROW_PARTS + 1):
        mxu, vpu = [], []
        if part > 0:
            mxu += ssm_projection(part - 1)
        if part > 1:
            vpu.append(functools.partial(norm_residual, part - 2))
        if part < OUT_ROW_PARTS:
            mxu += conv_projection(part)
            vpu += activation(part)
        _interleave(mxu, vpu)
    norm_residual(OUT_ROW_PARTS - 1)


def _out_proj(yc, yg, zs, x2d, w_glu_bf16, b_glu, w_out_bf16, norm_g):
    n_tok = x2d.shape[0]
    tm = TOKEN_TILE
    const = lambda i: (0, 0)
    tile = lambda w: pl.BlockSpec((tm, w), lambda i: (i, 0))
    return pl.pallas_call(
        _out_kernel,
        grid=(n_tok // tm,),
        in_specs=[
            tile(D_CONV),
            pl.BlockSpec((SSM_GROUPS, tm // SUBLANES, LANES), lambda i: (0, i, 0)),
            tile(D_SSM), tile(D_MODEL),
            pl.BlockSpec((D_SSM, D_SSM), const, pipeline_mode=pl.Buffered(1)),
            pl.BlockSpec((1, D_SSM), const),
            pl.BlockSpec((D_CONV + D_SSM, D_MODEL), const, pipeline_mode=pl.Buffered(1)),
            pl.BlockSpec((1, D_MODEL), const),
        ],
        out_specs=tile(D_MODEL),
        out_shape=jax.ShapeDtypeStruct((n_tok, D_MODEL), _F32),
        scratch_shapes=[pltpu.VMEM((D_SSM // LANES, tm, LANES), _F32),
                        pltpu.VMEM((OUT_ROW_PARTS, tm // OUT_ROW_PARTS, D_SSM), _BF16)],
        compiler_params=pltpu.CompilerParams(dimension_semantics=("arbitrary",),
                                             vmem_limit_bytes=VMEM_LIMIT_BYTES),
        name="glu_outproj",
    )(yc, yg, zs, x2d, w_glu_bf16, b_glu, w_out_bf16, norm_g)


def kernel(x, norm_pre_g, w_in, conv_w, conv_b, ssm_a_re, ssm_a_im, ssm_log_dt, ssm_b_re, ssm_b_im,
           ssm_c_re, ssm_c_im, ssm_d, w_glu, b_glu, w_out, norm_post_g):
    bsz, seq_len, _ = x.shape
    n_tok = bsz * seq_len
    g, p, h, t = SSM_GROUPS, SSM_STATE, SSM_GROUP, S5_CHUNK
    assert seq_len % TOKEN_TILE == 0 and TOKEN_TILE % t == 0
    x2d = x.reshape(n_tok, D_MODEL)

    yc, ug, zs, w_glu_bf16, w_out_bf16 = _inproj(
        x2d, norm_pre_g.reshape(1, -1), w_in, conv_w, conv_b.reshape(1, -1), w_glu, w_out, seq_len)

    brt, bit = ssm_b_re.transpose(0, 2, 1), ssm_b_im.transpose(0, 2, 1)
    a_re, a_im = ssm_a_re.reshape(g, 1, p), ssm_a_im.reshape(g, 1, p)
    rows = jnp.concatenate(
        [a_re, a_re, a_im, a_im, jnp.broadcast_to(ssm_log_dt.reshape(g, 1, 1), (g, 1, LANES)),
         ssm_d.reshape(g, 1, h), jnp.zeros((g, 1, S5_TH - h), _F32)], axis=-1)
    mats = jnp.concatenate([brt, bit, bit, brt, ssm_c_re, ssm_c_im, ssm_c_im, ssm_c_re], axis=-1)
    yg = _s5_core(ug, rows, mats, seq_len // t)

    out = _out_proj(yc, yg, zs, x2d, w_glu_bf16, b_glu.reshape(1, -1), w_out_bf16,
                    norm_post_g.reshape(1, -1))
    return out.reshape(bsz, seq_len, D_MODEL)
```

```python
import functools

import jax
import jax.numpy as jnp
from jax import lax
from jax.experimental import pallas as pl
from jax.experimental.pallas import tpu as pltpu

D_MODEL = 2048
D_CONV = 1024
D_SSM = 1024
SSM_GROUP = 16
SSM_GROUPS = 64
SSM_STATE = 64
N_IN = 4 * D_CONV + 2 * D_SSM
EPS = 1e-6

S5_CHUNK = 16
S5_TH = S5_CHUNK * SSM_GROUP
S5_POW_ROWS = S5_CHUNK + 8
S5_GROUPS_PER_STEP = 16
S5_SCAN_PITCH = 40
S5_SCAN_RADIX = 4
LANES = 128
SUBLANES = 8
GROUPS_PER_BLOCK = LANES // SSM_GROUP
TOKEN_TILE = 512
COL_BLOCK = 256
OUT_ROW_PARTS = 2
OUT_COL_BLOCK = 512
VMEM_LIMIT_BYTES = 56 * 1024 * 1024
S5_VMEM_LIMIT_BYTES = 60 * 1024 * 1024

_F32 = jnp.float32
_BF16 = jnp.bfloat16


def _dot(a, b):
    return jnp.dot(a, b, preferred_element_type=_F32)


def _dot_nt(a, b):
    return lax.dot_general(a, b, (((1,), (1,)), ((), ())), preferred_element_type=_F32)


def _sigmoid(z):
    return 1.0 / (1.0 + jnp.exp(-z))


def _interleave(mxu_items, vpu_items):
    n = max(len(mxu_items), 1)
    done = 0
    for i, item in enumerate(mxu_items):
        item()
        upto = -(-len(vpu_items) * (i + 1) // n)
        for piece in vpu_items[done:upto]:
            piece()
        done = upto
    for piece in vpu_items[done:]:
        piece()


def _load_weight_as_bf16(w_hbm, w_ref, stage, sem, cols):
    a, r, n = w_hbm.shape

    def chunk(c, slot):
        return pltpu.make_async_copy(w_hbm.at[:, :, pl.ds(c * cols, cols)], stage(slot), sem.at[slot])

    chunk(0, 0).start()
    for c in range(n // cols):
        slot = c % 2
        if c + 1 < n // cols:
            chunk(c + 1, 1 - slot).start()
        chunk(c, slot).wait()
        for k in range(a):
            w_ref[k * r:(k + 1) * r, c * cols:(c + 1) * cols] = stage(slot)[k].astype(_BF16)


def _lane_block_transpose(xs):
    nb = len(xs)
    diag = _diagonal_gather(xs)
    moved = [w if d == 0 else pltpu.roll(w, d * SSM_GROUP, axis=1) for d, w in enumerate(diag)]
    back = _diagonal_gather(moved)
    return [back[(-b) % nb] for b in range(nb)]


def _diagonal_gather(arrs):
    assert len(arrs) == 8
    c = lax.broadcasted_iota(jnp.int32, (1, LANES), 1) // SSM_GROUP
    c0, c1, c2 = c & 1, (c >> 1) & 1, (c >> 2) & 1
    s1 = {(k0, hi): jnp.where((c0 ^ k0) == 1, arrs[2 * hi + 1], arrs[2 * hi])
          for k0 in range(2) for hi in range(4)}
    s2 = {}
    for k0 in range(2):
        for k1 in range(2):
            e1 = c1 ^ k1 ^ (c0 & k0)
            for top in range(2):
                s2[(k0, k1, top)] = jnp.where(e1 == 1, s1[(k0, 2 * top + 1)], s1[(k0, 2 * top)])
    out = []
    for k in range(8):
        k0, k1, k2 = k & 1, (k >> 1) & 1, (k >> 2) & 1
        carry0 = c0 & k0
        carry1 = (c1 & k1) | (c1 & carry0) | (k1 & carry0)
        e2 = c2 ^ k2 ^ carry1
        out.append(jnp.where(e2 == 1, s2[(k0, k1, 1)], s2[(k0, k1, 0)]))
    return out


def _inproj_kernel(x_ref, g_ref, w_hbm, cw_ref, cb_ref, wglu_ref, wout_ref,
                   yc_ref, ug_ref, zs_ref, wglu_bf_ref, wout_bf_ref,
                   carry_ref, slab_ref, raw_ref, w_ref, sem, *, tiles_per_seq):
    tm = x_ref.shape[0]
    nb, sub = GROUPS_PER_BLOCK, SUBLANES
    slabs_per_block = COL_BLOCK // LANES
    n_blocks = D_CONV // COL_BLOCK
    bases = (4 * D_CONV, 0, D_CONV, 2 * D_CONV, 3 * D_CONV, 4 * D_CONV + D_SSM)

    @pl.when(pl.program_id(0) == 0)
    def _():
        _load_weight_as_bf16(w_hbm, w_ref, lambda slot: raw_ref.at[slot, pl.ds(0, w_hbm.shape[0])],
                             sem, COL_BLOCK)

    @pl.when(pl.program_id(0) % tiles_per_seq == 0)
    def _():
        carry_ref[...] = jnp.zeros_like(carry_ref)

    x = x_ref[...]
    ms = jnp.mean(x * x, axis=-1, keepdims=True)
    h = (x * lax.rsqrt(ms + EPS) * g_ref[...]).astype(_BF16)

    wglu_bf_ref[...] = wglu_ref[...].astype(_BF16)
    wout_bf_ref[...] = wout_ref[...].astype(_BF16)

    def project_one(blk, j):
        lo = bases[j] + blk * COL_BLOCK
        res = _dot(h, w_ref[:, lo:lo + COL_BLOCK])
        if j == 0:
            for k in range(slabs_per_block):
                slab_ref[blk * slabs_per_block + k] = res[:, k * LANES:(k + 1) * LANES]
        else:
            raw_ref[blk % 2, j - 1] = res

    def project(blk):
        return [functools.partial(project_one, blk, j) for j in range(len(bases))]

    def finish_gate(blk):
        sl = slice(blk * COL_BLOCK, (blk + 1) * COL_BLOCK)
        z_ssm = raw_ref[blk % 2, 4]
        zs_ref[:, sl] = (z_ssm * _sigmoid(z_ssm)).astype(_BF16)

    def finish_u(gv):
        rows = [slab_ref[gv, pl.ds(tl, tm // sub, stride=sub), :] for tl in range(sub)]
        for gl, o in enumerate(_lane_block_transpose(rows)):
            ug_ref[gv * nb + gl] = o

    def finish_conv(blk, half):
        sl = slice(blk * COL_BLOCK + half * LANES, blk * COL_BLOCK + (half + 1) * LANES)
        hl = slice(half * LANES, (half + 1) * LANES)
        b_gate, c_gate, v, z = (raw_ref[blk % 2, j, :, hl] for j in range(4))
        cv = c_gate * v
        ext = jnp.concatenate([carry_ref[:, sl], cv], axis=0)
        conv = (cb_ref[:, sl] + cw_ref[2:3, sl] * cv + cw_ref[1:2, sl] * ext[sub - 1:sub - 1 + tm]
                + cw_ref[0:1, sl] * ext[sub - 2:sub - 2 + tm])
        carry_ref[:, sl] = cv[tm - sub:]
        yc_ref[:, sl] = (b_gate * conv * (z * _sigmoid(z))).astype(_BF16)

    def finish(blk):
        return ([functools.partial(finish_gate, blk)]
                + [functools.partial(finish_u, blk * slabs_per_block + k) for k in range(slabs_per_block)]
                + [functools.partial(finish_conv, blk, half) for half in range(COL_BLOCK // LANES)])

    _interleave(project(0), [])
    for blk in range(1, n_blocks):
        _interleave(project(blk), finish(blk - 1))
    _interleave([], finish(n_blocks - 1))


def _inproj(x2d, norm_g, w_in, conv_w, conv_b, w_glu, w_out, seq_len):
    n_tok = x2d.shape[0]
    tm = TOKEN_TILE
    steps = n_tok // tm
    const = lambda s: (0, 0)
    tile = lambda w: pl.BlockSpec((tm, w), lambda s: (s, 0))
    row_block = lambda w: pl.BlockSpec((w.shape[0] // steps, w.shape[1]), lambda s: (s, 0))
    out_sds = jax.ShapeDtypeStruct((n_tok, D_CONV), _BF16)
    return pl.pallas_call(
        functools.partial(_inproj_kernel, tiles_per_seq=seq_len // tm),
        grid=(steps,),
        in_specs=[
            tile(D_MODEL),
            pl.BlockSpec((1, D_MODEL), const),
            pl.BlockSpec(memory_space=pl.ANY),
            pl.BlockSpec((3, D_CONV), const),
            pl.BlockSpec((1, D_CONV), const),
            row_block(w_glu), row_block(w_out),
        ],
        out_specs=[tile(D_CONV),
                   pl.BlockSpec((SSM_GROUPS, tm // SUBLANES, LANES), lambda s: (0, s, 0)),
                   tile(D_SSM),
                   row_block(w_glu), row_block(w_out)],
        out_shape=[out_sds,
                   jax.ShapeDtypeStruct((SSM_GROUPS, n_tok // SUBLANES, LANES), _F32),
                   out_sds,
                   jax.ShapeDtypeStruct(w_glu.shape, _BF16),
                   jax.ShapeDtypeStruct(w_out.shape, _BF16)],
        scratch_shapes=[pltpu.VMEM((SUBLANES, D_CONV), _F32),
                        pltpu.VMEM((D_SSM // LANES, tm, LANES), _F32),
                        pltpu.VMEM((2, 5, tm, COL_BLOCK), _F32),
                        pltpu.VMEM((D_MODEL, N_IN), _BF16),
                        pltpu.SemaphoreType.DMA((2,))],
        compiler_params=pltpu.CompilerParams(dimension_semantics=("arbitrary",),
                                             vmem_limit_bytes=VMEM_LIMIT_BYTES),
        name="inproj_conv",
    )(x2d, norm_g, w_in.reshape(D_MODEL // tm, tm, N_IN), conv_w, conv_b, w_glu, w_out)


def _complex_powers(ar, ai, exponent, nbits):
    pr = jnp.ones((exponent.shape[0], ar.shape[1]), _F32)
    pi = jnp.zeros_like(pr)
    cr, ci = ar, ai
    for b in range(nbits):
        bit = ((exponent >> b) & 1) == 1
        fr = jnp.where(bit, cr, 1.0)
        fi = jnp.where(bit, ci, 0.0)
        pr, pi = pr * fr - pi * fi, pr * fi + pi * fr
        cr, ci = cr * cr - ci * ci, 2.0 * cr * ci
    return pr, pi


def _re_im_sign():
    lane = lax.broadcasted_iota(jnp.int32, (1, LANES), 1)
    return jnp.where(lane < SSM_STATE, -1.0, 1.0).astype(_F32)


def _s5_prepare_group(g, rows_ref, mats_ref, kf_ref, ws_ref, wo_ref, a1_ref, a2_ref):
    t, th, jrows, h = S5_CHUNK, S5_TH, S5_POW_ROWS, SSM_GROUP
    sgn = _re_im_sign()
    nbits = max(1, (jrows - 1).bit_length())
    j = lax.broadcasted_iota(jnp.int32, (jrows, 1), 0)
    diag = (lax.broadcasted_iota(jnp.int32, (h, th), 0) == lax.broadcasted_iota(jnp.int32, (h, th), 1))
    lr, li, ldt, d_row = (rows_ref[g, :, 0:LANES], rows_ref[g, :, LANES:2 * LANES],
                          rows_ref[g, :, 2 * LANES:3 * LANES], rows_ref[g, :, 3 * LANES:])
    bt1, bt2, ct1, ct2 = (mats_ref[g, :, k * LANES:(k + 1) * LANES] for k in range(4))
    dt = jnp.exp(ldt)
    mag = jnp.exp(lr * dt)
    ar, ai = mag * jnp.cos(li * dt), mag * jnp.sin(li * dt)
    nr, ni = ar - 1.0, ai
    den = lr * lr + li * li
    qr = (nr * lr + ni * li) / den
    qi = (ni * lr - nr * li) / den
    b1 = qr * bt1 + (qi * sgn) * bt2
    b2 = (qr * sgn) * bt2 - qi * bt1
    c1 = ct1 * (-sgn)
    c2 = -ct2
    pr, pi = _complex_powers(ar, ai, j, nbits)
    rr, ri = _complex_powers(ar, ai, jnp.maximum(t - 1 - j, 0), nbits)
    cps = (pr[:, None, :] * c1[None] + pi[:, None, :] * c2[None]).reshape(jrows * h, LANES)
    cps = cps.astype(_BF16)
    b_hi = b1.astype(_BF16)
    b_lo = (b1 - b_hi.astype(_F32)).astype(_BF16)
    kf_ref[g] = _dot_nt(b_hi, cps[:th]) + _dot_nt(b_lo, cps[:th]) + jnp.where(diag, d_row, 0.0)
    wo_ref[g] = cps[h:h + th]
    ws_ref[g] = (rr[:t, None, :] * b1[None] + ri[:t, None, :] * b2[None]).reshape(th, LANES).astype(_BF16)
    a1_ref[g] = pr[t:t + 1]
    a2_ref[g] = pi[t:t + 1]


def _s5_kernel(ug_ref, rows_ref, mats_ref, yg_ref, kf_ref, ws_ref, wo_ref, a1_ref, a2_ref,
               mt_ref, xg_ref, sl_ref, slsw_ref, sin_ref, *, chunks_per_seq):
    nb = ug_ref.shape[0]
    t, th, h = S5_CHUNK, S5_TH, SSM_GROUP
    t_hi_n = t // SUBLANES
    n_chunks = ug_ref.shape[1] // t_hi_n
    n_seq = n_chunks // chunks_per_seq
    pitch = S5_SCAN_PITCH
    assert n_seq * nb <= pitch

    for g in range(nb):
        _s5_prepare_group(g, rows_ref, mats_ref, kf_ref, ws_ref, wo_ref, a1_ref, a2_ref)
        xg = jnp.concatenate([ug_ref[g, pl.ds(k, n_chunks, stride=t_hi_n), :]
                              for k in range(t_hi_n)], axis=1).astype(_BF16)
        xg_ref[g] = xg
        s = _dot(xg, ws_ref[g])
        s_sw = pltpu.roll(s, SSM_STATE, axis=1)
        for b in range(n_seq):
            rows = slice(b * chunks_per_seq, (b + 1) * chunks_per_seq)
            sl_ref[pl.ds(b * nb + g, chunks_per_seq, stride=pitch), :] = s[rows]
            slsw_ref[pl.ds(b * nb + g, chunks_per_seq, stride=pitch), :] = s_sw[rows]

    a1 = jnp.concatenate([a1_ref[g] for g in range(nb)] * n_seq, axis=0)
    a2 = jnp.concatenate([a2_ref[g] for g in range(nb)] * n_seq, axis=0) * _re_im_sign()
    n_rows = n_seq * nb

    def times(a, z):
        return a[0] * z[0] + a[1] * z[1], a[0] * z[1] - a[1] * z[0]

    def plus(z, w):
        return z[0] + w[0], z[1] + w[1]

    def local(c):
        return (sl_ref[c * pitch:c * pitch + n_rows, :], slsw_ref[c * pitch:c * pitch + n_rows, :])

    radix = S5_SCAN_RADIX
    powers = [(a1, a2)]
    for _ in range(radix - 1):
        pr, pi_signed = powers[-1]
        sgn = _re_im_sign()
        ar, ai, qr, qi = a1, a2 * sgn, pr, pi_signed * sgn
        powers.append((qr * ar - qi * ai, (qr * ai + qi * ar) * sgn))
    s = (jnp.zeros((n_rows, LANES), _F32), jnp.zeros((n_rows, LANES), _F32))
    for c0 in range(0, chunks_per_seq, radix):
        partial = None
        for j in range(radix):
            if j == 0:
                entering = s[0]
            else:
                entering = powers[j - 1][0] * s[0] + powers[j - 1][1] * s[1] + partial[0]
            sin_ref[(c0 + j) * pitch:(c0 + j) * pitch + n_rows, :] = entering
            partial = local(c0 + j) if partial is None else plus(times(powers[0], partial), local(c0 + j))
        s = plus(times(powers[radix - 1], s), partial)

    for g in range(nb):
        zk = jnp.concatenate([jnp.zeros((h, th), _F32), kf_ref[g]], axis=1)
        for k in range(t):
            mt_ref[g, k * h:(k + 1) * h, :] = zk[:, th - k * h:2 * th - k * h].astype(_BF16)

    for g in range(nb):
        s_in = jnp.concatenate([sin_ref[pl.ds(b * nb + g, chunks_per_seq, stride=pitch), :]
                                for b in range(n_seq)], axis=0).astype(_BF16)
        y = _dot(xg_ref[g], mt_ref[g]) + _dot_nt(s_in, wo_ref[g])
        for k in range(t_hi_n):
            yg_ref[g, pl.ds(k, n_chunks, stride=t_hi_n), :] = y[:, k * LANES:(k + 1) * LANES]


def _s5_core(ug, rows, mats, chunks_per_seq):
    _, n_rows, _ = ug.shape
    h, th, nb = SSM_GROUP, S5_TH, S5_GROUPS_PER_STEP
    group_rows = pl.BlockSpec((nb, n_rows, LANES), lambda i: (i, 0, 0))
    return pl.pallas_call(
        functools.partial(_s5_kernel, chunks_per_seq=chunks_per_seq),
        grid=(SSM_GROUPS // nb,),
        in_specs=[group_rows,
                  pl.BlockSpec((nb, 1, rows.shape[-1]), lambda i: (i, 0, 0)),
                  pl.BlockSpec((nb, h, mats.shape[-1]), lambda i: (i, 0, 0))],
        out_specs=group_rows,
        out_shape=jax.ShapeDtypeStruct(ug.shape, _F32),
        scratch_shapes=[pltpu.VMEM((nb, h, th), _F32),
                        pltpu.VMEM((nb, th, LANES), _BF16),
                        pltpu.VMEM((nb, th, LANES), _BF16),
                        pltpu.VMEM((nb, 1, LANES), _F32),
                        pltpu.VMEM((nb, 1, LANES), _F32),
                        pltpu.VMEM((nb, th, th), _BF16),
                        pltpu.VMEM((nb, n_rows * SUBLANES // S5_CHUNK, th), _BF16)]
        + [pltpu.VMEM((chunks_per_seq * S5_SCAN_PITCH, LANES), _F32)] * 3,
        compiler_params=pltpu.CompilerParams(dimension_semantics=("arbitrary",),
                                             vmem_limit_bytes=S5_VMEM_LIMIT_BYTES),
        name="s5_core",
    )(ug, rows, mats)


def _out_kernel(yc_ref, yg_ref, zs_ref, x_ref, wg_ref, bg_ref, wo_ref, gp_ref, o_ref, slab_ref, ys_ref):
    tm = x_ref.shape[0]
    nb, sub = GROUPS_PER_BLOCK, SUBLANES
    rp = tm // OUT_ROW_PARTS
    slabs_per_block = COL_BLOCK // LANES

    def activation(part):
        rows = slice(part * rp, (part + 1) * rp)
        grows = slice(part * rp // sub, (part + 1) * rp // sub)
        acc = {"pre": bg_ref[...], "y": []}

        def block(blk):
            gvs = range(blk * slabs_per_block, (blk + 1) * slabs_per_block)
            for gv in gvs:
                tok = _lane_block_transpose([yg_ref[gv * nb + gl, grows, :] for gl in range(nb)])
                for tl in range(sub):
                    slab_ref[gv, pl.ds(part * rp + tl, rp // sub, stride=sub), :] = tok[tl]
            y = jax.nn.gelu(jnp.concatenate([slab_ref[gv, rows, :] for gv in gvs], axis=1))
            acc["y"].append(y)
            acc["pre"] = acc["pre"] + _dot(y.astype(_BF16), wg_ref[blk * COL_BLOCK:(blk + 1) * COL_BLOCK, :])

        def gate():
            y = jnp.concatenate(acc["y"], axis=1)
            ys_ref[part] = (y * _sigmoid(acc["pre"]) * zs_ref[rows, :].astype(_F32)).astype(_BF16)

        return [functools.partial(block, blk) for blk in range(D_SSM // COL_BLOCK)] + [gate]

    def conv_projection(part):
        rows = slice(part * rp, (part + 1) * rp)

        def cols(c):
            sl = slice(c * OUT_COL_BLOCK, (c + 1) * OUT_COL_BLOCK)
            o_ref[rows, sl] = _dot(yc_ref[rows, :], wo_ref[:D_CONV, sl])

        return [functools.partial(cols, c) for c in range(D_MODEL // OUT_COL_BLOCK)]

    def ssm_projection(part):
        rows = slice(part * rp, (part + 1) * rp)

        def cols(c):
            sl = slice(c * OUT_COL_BLOCK, (c + 1) * OUT_COL_BLOCK)
            o_ref[rows, sl] += _dot(ys_ref[part], wo_ref[D_CONV:, sl])

        return [functools.partial(cols, c) for c in range(D_MODEL // OUT_COL_BLOCK)]

    def norm_residual(part):
        rows = slice(part * rp, (part + 1) * rp)
        o = o_ref[rows, :]
        ms = jnp.mean(o * o, axis=-1, keepdims=True)
        o_ref[rows, :] = x_ref[rows, :] + o * lax.rsqrt(ms + EPS) * gp_ref[...]

    for part in range(OUT_ROW_PARTS + 1):
        mxu, vpu = [], []
        if part > 0:
            mxu += ssm_projection(part - 1)
        if part > 1:
            vpu.append(functools.partial(norm_residual, part - 2))
        if part < OUT_ROW_PARTS:
            mxu += conv_projection(part)
            vpu += activation(part)
        _interleave(mxu, vpu)
    norm_residual(OUT_ROW_PARTS - 1)


def _out_proj(yc, yg, zs, x2d, w_glu_bf16, b_glu, w_out_bf16, norm_g):
    n_tok = x2d.shape[0]
    tm = TOKEN_TILE
    const = lambda i: (0, 0)
    tile = lambda w: pl.BlockSpec((tm, w), lambda i: (i, 0))
    return pl.pallas_call(
        _out_kernel,
        grid=(n_tok // tm,),
        in_specs=[
            tile(D_CONV),
            pl.BlockSpec((SSM_GROUPS, tm // SUBLANES, LANES), lambda i: (0, i, 0)),
            tile(D_SSM), tile(D_MODEL),
            pl.BlockSpec((D_SSM, D_SSM), const, pipeline_mode=pl.Buffered(1)),
            pl.BlockSpec((1, D_SSM), const),
            pl.BlockSpec((D_CONV + D_SSM, D_MODEL), const, pipeline_mode=pl.Buffered(1)),
            pl.BlockSpec((1, D_MODEL), const),
        ],
        out_specs=tile(D_MODEL),
        out_shape=jax.ShapeDtypeStruct((n_tok, D_MODEL), _F32),
        scratch_shapes=[pltpu.VMEM((D_SSM // LANES, tm, LANES), _F32),
                        pltpu.VMEM((OUT_ROW_PARTS, tm // OUT_ROW_PARTS, D_SSM), _BF16)],
        compiler_params=pltpu.CompilerParams(dimension_semantics=("arbitrary",),
                                             vmem_limit_bytes=VMEM_LIMIT_BYTES),
        name="glu_outproj",
    )(yc, yg, zs, x2d, w_glu_bf16, b_glu, w_out_bf16, norm_g)


def kernel(x, norm_pre_g, w_in, conv_w, conv_b, ssm_a_re, ssm_a_im, ssm_log_dt, ssm_b_re, ssm_b_im,
           ssm_c_re, ssm_c_im, ssm_d, w_glu, b_glu, w_out, norm_post_g):
    bsz, seq_len, _ = x.shape
    n_tok = bsz * seq_len
    g, p, h, t = SSM_GROUPS, SSM_STATE, SSM_GROUP, S5_CHUNK
    assert seq_len % TOKEN_TILE == 0 and TOKEN_TILE % t == 0
    x2d = x.reshape(n_tok, D_MODEL)

    yc, ug, zs, w_glu_bf16, w_out_bf16 = _inproj(
        x2d, norm_pre_g.reshape(1, -1), w_in, conv_w, conv_b.reshape(1, -1), w_glu, w_out, seq_len)

    brt, bit = ssm_b_re.transpose(0, 2, 1), ssm_b_im.transpose(0, 2, 1)
    a_re, a_im = ssm_a_re.reshape(g, 1, p), ssm_a_im.reshape(g, 1, p)
    rows = jnp.concatenate(
        [a_re, a_re, a_im, a_im, jnp.broadcast_to(ssm_log_dt.reshape(g, 1, 1), (g, 1, LANES)),
         ssm_d.reshape(g, 1, h), jnp.zeros((g, 1, S5_TH - h), _F32)], axis=-1)
    mats = jnp.concatenate([brt, bit, bit, brt, ssm_c_re, ssm_c_im, ssm_c_im, ssm_c_re], axis=-1)
    yg = _s5_core(ug, rows, mats, seq_len // t)

    out = _out_proj(yc, yg, zs, x2d, w_glu_bf16, b_glu.reshape(1, -1), w_out_bf16,
                    norm_post_g.reshape(1, -1))
    return out.reshape(bsz, seq_len, D_MODEL)
```

```python
import functools

import jax
import jax.numpy as jnp
from jax import lax
from jax.experimental import pallas as pl
from jax.experimental.pallas import tpu as pltpu

D_MODEL = 2048
D_CONV = 1024
D_SSM = 1024
SSM_GROUP = 16
SSM_GROUPS = 64
SSM_STATE = 64
N_IN = 4 * D_CONV + 2 * D_SSM
EPS = 1e-6

S5_CHUNK = 16
S5_TH = S5_CHUNK * SSM_GROUP
S5_POW_ROWS = S5_CHUNK + 8
S5_SCAN_PITCH = 24
S5_SCAN_RADIX = 4
LANES = 128
SUBLANES = 8
GROUPS_PER_BLOCK = LANES // SSM_GROUP
TOKEN_TILE = 512
COL_BLOCK = 256
OUT_ROW_PARTS = 2
OUT_COL_BLOCK = 512
VMEM_LIMIT_BYTES = 56 * 1024 * 1024

_F32 = jnp.float32
_BF16 = jnp.bfloat16


def _dot(a, b):
    return jnp.dot(a, b, preferred_element_type=_F32)


def _dot_nt(a, b):
    return lax.dot_general(a, b, (((1,), (1,)), ((), ())), preferred_element_type=_F32)


def _sigmoid(z):
    return 1.0 / (1.0 + jnp.exp(-z))


def _interleave(mxu_items, vpu_items):
    n = max(len(mxu_items), 1)
    done = 0
    for i, item in enumerate(mxu_items):
        item()
        upto = -(-len(vpu_items) * (i + 1) // n)
        for piece in vpu_items[done:upto]:
            piece()
        done = upto
    for piece in vpu_items[done:]:
        piece()


def _load_weight_as_bf16(w_hbm, w_ref, stage, sem, cols):
    a, r, n = w_hbm.shape

    def chunk(c, slot):
        return pltpu.make_async_copy(w_hbm.at[:, :, pl.ds(c * cols, cols)], stage(slot), sem.at[slot])

    chunk(0, 0).start()
    for c in range(n // cols):
        slot = c % 2
        if c + 1 < n // cols:
            chunk(c + 1, 1 - slot).start()
        chunk(c, slot).wait()
        for k in range(a):
            w_ref[k * r:(k + 1) * r, c * cols:(c + 1) * cols] = stage(slot)[k].astype(_BF16)


def _lane_block_transpose(xs):
    nb = len(xs)
    diag = _diagonal_gather(xs)
    moved = [w if d == 0 else pltpu.roll(w, d * SSM_GROUP, axis=1) for d, w in enumerate(diag)]
    back = _diagonal_gather(moved)
    return [back[(-b) % nb] for b in range(nb)]


def _diagonal_gather(arrs):
    assert len(arrs) == 8
    c = lax.broadcasted_iota(jnp.int32, (1, LANES), 1) // SSM_GROUP
    c0, c1, c2 = c & 1, (c >> 1) & 1, (c >> 2) & 1
    s1 = {(k0, hi): jnp.where((c0 ^ k0) == 1, arrs[2 * hi + 1], arrs[2 * hi])
          for k0 in range(2) for hi in range(4)}
    s2 = {}
    for k0 in range(2):
        for k1 in range(2):
            e1 = c1 ^ k1 ^ (c0 & k0)
            for top in range(2):
                s2[(k0, k1, top)] = jnp.where(e1 == 1, s1[(k0, 2 * top + 1)], s1[(k0, 2 * top)])
    out = []
    for k in range(8):
        k0, k1, k2 = k & 1, (k >> 1) & 1, (k >> 2) & 1
        carry0 = c0 & k0
        carry1 = (c1 & k1) | (c1 & carry0) | (k1 & carry0)
        e2 = c2 ^ k2 ^ carry1
        out.append(jnp.where(e2 == 1, s2[(k0, k1, 1)], s2[(k0, k1, 0)]))
    return out


def _inproj_kernel(x_ref, g_ref, w_hbm, cw_ref, cb_ref, wglu_ref, wout_ref,
                   yc_ref, ug_ref, zs_ref, wglu_bf_ref, wout_bf_ref,
                   carry_ref, slab_ref, raw_ref, w_ref, sem, *, tiles_per_seq):
    tm = x_ref.shape[0]
    nb, sub = GROUPS_PER_BLOCK, SUBLANES
    slabs_per_block = COL_BLOCK // LANES
    n_blocks = D_CONV // COL_BLOCK
    bases = (4 * D_CONV, 0, D_CONV, 2 * D_CONV, 3 * D_CONV, 4 * D_CONV + D_SSM)

    @pl.when(pl.program_id(0) == 0)
    def _():
        _load_weight_as_bf16(w_hbm, w_ref, lambda slot: raw_ref.at[slot, pl.ds(0, w_hbm.shape[0])],
                             sem, COL_BLOCK)

    @pl.when(pl.program_id(0) % tiles_per_seq == 0)
    def _():
        carry_ref[...] = jnp.zeros_like(carry_ref)

    x = x_ref[...]
    ms = jnp.mean(x * x, axis=-1, keepdims=True)
    h = (x * lax.rsqrt(ms + EPS) * g_ref[...]).astype(_BF16)

    wglu_bf_ref[...] = wglu_ref[...].astype(_BF16)
    wout_bf_ref[...] = wout_ref[...].astype(_BF16)

    def project_one(blk, j):
        lo = bases[j] + blk * COL_BLOCK
        res = _dot(h, w_ref[:, lo:lo + COL_BLOCK])
        if j == 0:
            for k in range(slabs_per_block):
                slab_ref[blk * slabs_per_block + k] = res[:, k * LANES:(k + 1) * LANES]
        else:
            raw_ref[blk % 2, j - 1] = res

    def project(blk):
        return [functools.partial(project_one, blk, j) for j in range(len(bases))]

    def finish_gate(blk):
        sl = slice(blk * COL_BLOCK, (blk + 1) * COL_BLOCK)
        z_ssm = raw_ref[blk % 2, 4]
        zs_ref[:, sl] = (z_ssm * _sigmoid(z_ssm)).astype(_BF16)

    def finish_u(gv):
        rows = [slab_ref[gv, pl.ds(tl, tm // sub, stride=sub), :] for tl in range(sub)]
        for gl, o in enumerate(_lane_block_transpose(rows)):
            ug_ref[gv * nb + gl] = o

    def finish_conv(blk, half):
        sl = slice(blk * COL_BLOCK + half * LANES, blk * COL_BLOCK + (half + 1) * LANES)
        hl = slice(half * LANES, (half + 1) * LANES)
        b_gate, c_gate, v, z = (raw_ref[blk % 2, j, :, hl] for j in range(4))
        cv = c_gate * v
        ext = jnp.concatenate([carry_ref[:, sl], cv], axis=0)
        conv = (cb_ref[:, sl] + cw_ref[2:3, sl] * cv + cw_ref[1:2, sl] * ext[sub - 1:sub - 1 + tm]
                + cw_ref[0:1, sl] * ext[sub - 2:sub - 2 + tm])
        carry_ref[:, sl] = cv[tm - sub:]
        yc_ref[:, sl] = (b_gate * conv * (z * _sigmoid(z))).astype(_BF16)

    def finish(blk):
        return ([functools.partial(finish_gate, blk)]
                + [functools.partial(finish_u, blk * slabs_per_block + k) for k in range(slabs_per_block)]
                + [functools.partial(finish_conv, blk, half) for half in range(COL_BLOCK // LANES)])

    _interleave(project(0), [])
    for blk in range(1, n_blocks):
        _interleave(project(blk), finish(blk - 1))
    _interleave([], finish(n_blocks - 1))


def _inproj(x2d, norm_g, w_in, conv_w, conv_b, w_glu, w_out, seq_len):
    n_tok = x2d.shape[0]
    tm = TOKEN_TILE
    steps = n_tok // tm
    const = lambda s: (0, 0)
    tile = lambda w: pl.BlockSpec((tm, w), lambda s: (s, 0))
    row_block = lambda w: pl.BlockSpec((w.shape[0] // steps, w.shape[1]), lambda s: (s, 0))
    out_sds = jax.ShapeDtypeStruct((n_tok, D_CONV), _BF16)
    return pl.pallas_call(
        functools.partial(_inproj_kernel, tiles_per_seq=seq_len // tm),
        grid=(steps,),
        in_specs=[
            tile(D_MODEL),
            pl.BlockSpec((1, D_MODEL), const),
            pl.BlockSpec(memory_space=pl.ANY),
            pl.BlockSpec((3, D_CONV), const),
            pl.BlockSpec((1, D_CONV), const),
            row_block(w_glu), row_block(w_out),
        ],
        out_specs=[tile(D_CONV),
                   pl.BlockSpec((SSM_GROUPS, tm // SUBLANES, LANES), lambda s: (0, s, 0)),
                   tile(D_SSM),
                   row_block(w_glu), row_block(w_out)],
        out_shape=[out_sds,
                   jax.ShapeDtypeStruct((SSM_GROUPS, n_tok // SUBLANES, LANES), _F32),
                   out_sds,
                   jax.ShapeDtypeStruct(w_glu.shape, _BF16),
                   jax.ShapeDtypeStruct(w_out.shape, _BF16)],
        scratch_shapes=[pltpu.VMEM((SUBLANES, D_CONV), _F32),
                        pltpu.VMEM((D_SSM // LANES, tm, LANES), _F32),
                        pltpu.VMEM((2, 5, tm, COL_BLOCK), _F32),
                        pltpu.VMEM((D_MODEL, N_IN), _BF16),
                        pltpu.SemaphoreType.DMA((2,))],
        compiler_params=pltpu.CompilerParams(dimension_semantics=("arbitrary",),
                                             vmem_limit_bytes=VMEM_LIMIT_BYTES),
        name="inproj_conv",
    )(x2d, norm_g, w_in.reshape(D_MODEL // tm, tm, N_IN), conv_w, conv_b, w_glu, w_out)


def _complex_powers(ar, ai, exponent, nbits):
    pr = jnp.ones((exponent.shape[0], ar.shape[1]), _F32)
    pi = jnp.zeros_like(pr)
    cr, ci = ar, ai
    for b in range(nbits):
        bit = ((exponent >> b) & 1) == 1
        fr = jnp.where(bit, cr, 1.0)
        fi = jnp.where(bit, ci, 0.0)
        pr, pi = pr * fr - pi * fi, pr * fi + pi * fr
        cr, ci = cr * cr - ci * ci, 2.0 * cr * ci
    return pr, pi


def _re_im_sign():
    lane = lax.broadcasted_iota(jnp.int32, (1, LANES), 1)
    return jnp.where(lane < SSM_STATE, -1.0, 1.0).astype(_F32)


def _s5_prepare_group(g, group, are_ref, aim_ref, ldt_ref, d_ref, mats_ref,
                      kf_ref, ws_ref, wo_ref, a1_ref, a2_ref):
    t, th, jrows, h = S5_CHUNK, S5_TH, S5_POW_ROWS, SSM_GROUP
    sgn = _re_im_sign()
    nbits = max(1, (jrows - 1).bit_length())
    j = lax.broadcasted_iota(jnp.int32, (jrows, 1), 0)
    diag = (lax.broadcasted_iota(jnp.int32, (h, th), 0) == lax.broadcasted_iota(jnp.int32, (h, th), 1))
    a_re, a_im = are_ref[group], aim_ref[group]
    lr = jnp.concatenate([a_re, a_re], axis=1)
    li = jnp.concatenate([a_im, a_im], axis=1)
    ldt = jnp.broadcast_to(ldt_ref[group], (1, LANES))
    d_row = jnp.concatenate([d_ref[group], jnp.zeros((1, th - h), _F32)], axis=1)
    bt1, bt2, ct1, ct2 = (mats_ref[g, :, k * LANES:(k + 1) * LANES] for k in range(4))
    dt = jnp.exp(ldt)
    mag = jnp.exp(lr * dt)
    ar, ai = mag * jnp.cos(li * dt), mag * jnp.sin(li * dt)
    nr, ni = ar - 1.0, ai
    den = lr * lr + li * li
    qr = (nr * lr + ni * li) / den
    qi = (ni * lr - nr * li) / den
    b1 = qr * bt1 + (qi * sgn) * bt2
    b2 = (qr * sgn) * bt2 - qi * bt1
    c1 = ct1 * (-sgn)
    c2 = -ct2
    pr, pi = _complex_powers(ar, ai, j, nbits)
    rr, ri = _complex_powers(ar, ai, jnp.maximum(t - 1 - j, 0), nbits)
    cps = (pr[:, None, :] * c1[None] + pi[:, None, :] * c2[None]).reshape(jrows * h, LANES)
    cps = cps.astype(_BF16)
    b_hi = b1.astype(_BF16)
    b_lo = (b1 - b_hi.astype(_F32)).astype(_BF16)
    kf_ref[g] = _dot_nt(b_hi, cps[:th]) + _dot_nt(b_lo, cps[:th]) + jnp.where(diag, d_row, 0.0)
    wo_ref[g] = cps[h:h + th]
    ws_ref[g] = (rr[:t, None, :] * b1[None] + ri[:t, None, :] * b2[None]).reshape(th, LANES).astype(_BF16)
    a1_ref[g] = pr[t:t + 1]
    a2_ref[g] = pi[t:t + 1]


def _s5_kernel(ug_ref, are_ref, aim_ref, ldt_ref, d_ref, mats_ref, yg_ref,
               kf_ref, ws_ref, wo_ref, a1_ref, a2_ref,
               mt_ref, xg_ref, yi_ref, sl_ref, slsw_ref, sin_ref, *, chunks_per_seq):
    nb = ug_ref.shape[0]
    t, th, h = S5_CHUNK, S5_TH, SSM_GROUP
    t_hi_n = t // SUBLANES
    n_chunks = ug_ref.shape[1] // t_hi_n
    n_seq = n_chunks // chunks_per_seq
    pitch = S5_SCAN_PITCH
    assert n_seq * nb <= pitch

    for g in range(nb):
        _s5_prepare_group(g, pl.program_id(0) * nb + g, are_ref, aim_ref, ldt_ref, d_ref, mats_ref,
                          kf_ref, ws_ref, wo_ref, a1_ref, a2_ref)
        xg = jnp.concatenate([ug_ref[g, pl.ds(k, n_chunks, stride=t_hi_n), :]
                              for k in range(t_hi_n)], axis=1).astype(_BF16)
        xg_ref[g] = xg
        s = _dot(xg, ws_ref[g])
        s_sw = pltpu.roll(s, SSM_STATE, axis=1)
        for b in range(n_seq):
            rows = slice(b * chunks_per_seq, (b + 1) * chunks_per_seq)
            sl_ref[pl.ds(b * nb + g, chunks_per_seq, stride=pitch), :] = s[rows]
            slsw_ref[pl.ds(b * nb + g, chunks_per_seq, stride=pitch), :] = s_sw[rows]

    a1 = jnp.concatenate([a1_ref[g] for g in range(nb)] * n_seq, axis=0)
    a2 = jnp.concatenate([a2_ref[g] for g in range(nb)] * n_seq, axis=0) * _re_im_sign()
    n_rows = n_seq * nb

    def times(a, z):
        return a[0] * z[0] + a[1] * z[1], a[0] * z[1] - a[1] * z[0]

    def plus(z, w):
        return z[0] + w[0], z[1] + w[1]

    def local(c):
        return (sl_ref[c * pitch:c * pitch + n_rows, :], slsw_ref[c * pitch:c * pitch + n_rows, :])

    radix = S5_SCAN_RADIX
    powers = [(a1, a2)]
    for _ in range(radix - 1):
        pr, pi_signed = powers[-1]
        sgn = _re_im_sign()
        ar, ai, qr, qi = a1, a2 * sgn, pr, pi_signed * sgn
        powers.append((qr * ar - qi * ai, (qr * ai + qi * ar) * sgn))
    s = (jnp.zeros((n_rows, LANES), _F32), jnp.zeros((n_rows, LANES), _F32))
    for c0 in range(0, chunks_per_seq, radix):
        partial = None
        for j in range(radix):
            if j == 0:
                entering = s[0]
            else:
                entering = powers[j - 1][0] * s[0] + powers[j - 1][1] * s[1] + partial[0]
            sin_ref[(c0 + j) * pitch:(c0 + j) * pitch + n_rows, :] = entering
            partial = local(c0 + j) if partial is None else plus(times(powers[0], partial), local(c0 + j))
        s = plus(times(powers[radix - 1], s), partial)

    for g in range(nb):
        zk = jnp.concatenate([jnp.zeros((h, th), _F32), kf_ref[g]], axis=1)
        for k in range(t):
            mt_ref[g, k * h:(k + 1) * h, :] = zk[:, th - k * h:2 * th - k * h].astype(_BF16)
        yi_ref[g] = _dot(xg_ref[g], mt_ref[g])

    for g in range(nb):
        s_in = jnp.concatenate([sin_ref[pl.ds(b * nb + g, chunks_per_seq, stride=pitch), :]
                                for b in range(n_seq)], axis=0).astype(_BF16)
        y = yi_ref[g] + _dot_nt(s_in, wo_ref[g])
        for k in range(t_hi_n):
            yg_ref[g, pl.ds(k, n_chunks, stride=t_hi_n), :] = y[:, k * LANES:(k + 1) * LANES]


def _s5_core(ug, a_re, a_im, log_dt, d_skip, mats, chunks_per_seq):
    _, n_rows, _ = ug.shape
    h, th, nb = SSM_GROUP, S5_TH, GROUPS_PER_BLOCK
    group_rows = pl.BlockSpec((nb, n_rows, LANES), lambda i: (i, 0, 0))
    return pl.pallas_call(
        functools.partial(_s5_kernel, chunks_per_seq=chunks_per_seq),
        grid=(SSM_GROUPS // nb,),
        in_specs=[group_rows]
        + [pl.BlockSpec(p.shape, lambda i: (0, 0, 0)) for p in (a_re, a_im, log_dt, d_skip)]
        + [pl.BlockSpec((nb, h, mats.shape[-1]), lambda i: (i, 0, 0))],
        out_specs=group_rows,
        out_shape=jax.ShapeDtypeStruct(ug.shape, _F32),
        scratch_shapes=[pltpu.VMEM((nb, h, th), _F32),
                        pltpu.VMEM((nb, th, LANES), _BF16),
                        pltpu.VMEM((nb, th, LANES), _BF16),
                        pltpu.VMEM((nb, 1, LANES), _F32),
                        pltpu.VMEM((nb, 1, LANES), _F32),
                        pltpu.VMEM((nb, th, th), _BF16),
                        pltpu.VMEM((nb, n_rows * SUBLANES // S5_CHUNK, th), _BF16),
                        pltpu.VMEM((nb, n_rows * SUBLANES // S5_CHUNK, th), _F32)]
        + [pltpu.VMEM((chunks_per_seq * S5_SCAN_PITCH, LANES), _F32)] * 3,
        compiler_params=pltpu.CompilerParams(dimension_semantics=("arbitrary",),
                                             vmem_limit_bytes=VMEM_LIMIT_BYTES),
        name="s5_core",
    )(ug, a_re, a_im, log_dt, d_skip, mats)


def _out_kernel(yc_ref, yg_ref, zs_ref, x_ref, wg_ref, bg_ref, wo_ref, gp_ref, o_ref, slab_ref, ys_ref):
    tm = x_ref.shape[0]
    nb, sub = GROUPS_PER_BLOCK, SUBLANES
    rp = tm // OUT_ROW_PARTS
    slabs_per_block = COL_BLOCK // LANES

    def activation(part):
        rows = slice(part * rp, (part + 1) * rp)
        grows = slice(part * rp // sub, (part + 1) * rp // sub)
        acc = {"pre": bg_ref[...], "y": []}

        def block(blk):
            gvs = range(blk * slabs_per_block, (blk + 1) * slabs_per_block)
            for gv in gvs:
                tok = _lane_block_transpose([yg_ref[gv * nb + gl, grows, :] for gl in range(nb)])
                for tl in range(sub):
                    slab_ref[gv, pl.ds(part * rp + tl, rp // sub, stride=sub), :] = tok[tl]
            y = jax.nn.gelu(jnp.concatenate([slab_ref[gv, rows, :] for gv in gvs], axis=1))
            acc["y"].append(y)
            acc["pre"] = acc["pre"] + _dot(y.astype(_BF16), wg_ref[blk * COL_BLOCK:(blk + 1) * COL_BLOCK, :])

        def gate():
            y = jnp.concatenate(acc["y"], axis=1)
            ys_ref[part] = (y * _sigmoid(acc["pre"]) * zs_ref[rows, :].astype(_F32)).astype(_BF16)

        return [functools.partial(block, blk) for blk in range(D_SSM // COL_BLOCK)] + [gate]

    def conv_projection(part):
        rows = slice(part * rp, (part + 1) * rp)

        def cols(c):
            sl = slice(c * OUT_COL_BLOCK, (c + 1) * OUT_COL_BLOCK)
            o_ref[rows, sl] = _dot(yc_ref[rows, :], wo_ref[:D_CONV, sl])

        return [functools.partial(cols, c) for c in range(D_MODEL // OUT_COL_BLOCK)]

    def ssm_projection(part):
        rows = slice(part * rp, (part + 1) * rp)

        def cols(c):
            sl = slice(c * OUT_COL_BLOCK, (c + 1) * OUT_COL_BLOCK)
            o_ref[rows, sl] += _dot(ys_ref[part], wo_ref[D_CONV:, sl])

        return [functools.partial(cols, c) for c in range(D_MODEL // OUT_COL_BLOCK)]

    def norm_residual(part):
        rows = slice(part * rp, (part + 1) * rp)
        o = o_ref[rows, :]
        ms = jnp.mean(o * o, axis=-1, keepdims=True)
        o_ref[rows, :] = x_ref[rows, :] + o * lax.rsqrt(ms + EPS) * gp_ref[...]

    for part in range(OUT_ROW_PARTS + 1):
        mxu, vpu = [], []
        if part > 0:
            mxu += ssm_projection(part - 1)
        if part > 1:
            vpu.append(functools.partial(norm_residual, part - 2))
        if part < OUT_ROW_PARTS:
            mxu += conv_projection(part)
            vpu += activation(part)
        _interleave(mxu, vpu)
    norm_residual(OUT_ROW_PARTS - 1)


def _out_proj(yc, yg, zs, x2d, w_glu_bf16, b_glu, w_out_bf16, norm_g):
    n_tok = x2d.shape[0]
    tm = TOKEN_TILE
    const = lambda i: (0, 0)
    tile = lambda w: pl.BlockSpec((tm, w), lambda i: (i, 0))
    return pl.pallas_call(
        _out_kernel,
        grid=(n_tok // tm,),
        in_specs=[
            tile(D_CONV),
            pl.BlockSpec((SSM_GROUPS, tm // SUBLANES, LANES), lambda i: (0, i, 0)),
            tile(D_SSM), tile(D_MODEL),
            pl.BlockSpec((D_SSM, D_SSM), const, pipeline_mode=pl.Buffered(1)),
            pl.BlockSpec((1, D_SSM), const),
            pl.BlockSpec((D_CONV + D_SSM, D_MODEL), const, pipeline_mode=pl.Buffered(1)),
            pl.BlockSpec((1, D_MODEL), const),
        ],
        out_specs=tile(D_MODEL),
        out_shape=jax.ShapeDtypeStruct((n_tok, D_MODEL), _F32),
        scratch_shapes=[pltpu.VMEM((D_SSM // LANES, tm, LANES), _F32),
                        pltpu.VMEM((OUT_ROW_PARTS, tm // OUT_ROW_PARTS, D_SSM), _BF16)],
        compiler_params=pltpu.CompilerParams(dimension_semantics=("arbitrary",),
                                             vmem_limit_bytes=VMEM_LIMIT_BYTES),
        name="glu_outproj",
    )(yc, yg, zs, x2d, w_glu_bf16, b_glu, w_out_bf16, norm_g)


def kernel(x, norm_pre_g, w_in, conv_w, conv_b, ssm_a_re, ssm_a_im, ssm_log_dt, ssm_b_re, ssm_b_im,
           ssm_c_re, ssm_c_im, ssm_d, w_glu, b_glu, w_out, norm_post_g):
    bsz, seq_len, _ = x.shape
    n_tok = bsz * seq_len
    g, p, h, t = SSM_GROUPS, SSM_STATE, SSM_GROUP, S5_CHUNK
    assert seq_len % TOKEN_TILE == 0 and TOKEN_TILE % t == 0
    x2d = x.reshape(n_tok, D_MODEL)

    yc, ug, zs, w_glu_bf16, w_out_bf16 = _inproj(
        x2d, norm_pre_g.reshape(1, -1), w_in, conv_w, conv_b.reshape(1, -1), w_glu, w_out, seq_len)

    brt, bit = ssm_b_re.transpose(0, 2, 1), ssm_b_im.transpose(0, 2, 1)
    mats = jnp.concatenate([brt, bit, bit, brt, ssm_c_re, ssm_c_im, ssm_c_im, ssm_c_re], axis=-1)
    yg = _s5_core(ug, ssm_a_re.reshape(g, 1, p), ssm_a_im.reshape(g, 1, p), ssm_log_dt.reshape(g, 1, 1),
                  ssm_d.reshape(g, 1, h), mats, seq_len // t)

    out = _out_proj(yc, yg, zs, x2d, w_glu_bf16, b_glu.reshape(1, -1), w_out_bf16,
                    norm_post_g.reshape(1, -1))
    return out.reshape(bsz, seq_len, D_MODEL)
```

```python
import functools

import jax
import jax.numpy as jnp
from jax import lax
from jax.experimental import pallas as pl
from jax.experimental.pallas import tpu as pltpu

D_MODEL = 2048
D_CONV = 1024
D_SSM = 1024
SSM_GROUP = 16
SSM_GROUPS = 64
SSM_STATE = 64
N_IN = 4 * D_CONV + 2 * D_SSM
EPS = 1e-6

S5_CHUNK = 16
S5_TH = S5_CHUNK * SSM_GROUP
S5_POW_ROWS = S5_CHUNK + 8
S5_SCAN_PITCH = 24
S5_SCAN_RADIX = 4
LANES = 128
SUBLANES = 8
GROUPS_PER_BLOCK = LANES // SSM_GROUP
TOKEN_TILE = 512
COL_BLOCK = 256
OUT_ROW_PARTS = 2
OUT_COL_BLOCK = 512
VMEM_LIMIT_BYTES = 56 * 1024 * 1024

_F32 = jnp.float32
_BF16 = jnp.bfloat16


def _dot(a, b):
    return jnp.dot(a, b, preferred_element_type=_F32)


def _dot_nt(a, b):
    return lax.dot_general(a, b, (((1,), (1,)), ((), ())), preferred_element_type=_F32)


def _sigmoid(z):
    return 1.0 / (1.0 + jnp.exp(-z))


def _interleave(mxu_items, vpu_items):
    n = max(len(mxu_items), 1)
    done = 0
    for i, item in enumerate(mxu_items):
        item()
        upto = -(-len(vpu_items) * (i + 1) // n)
        for piece in vpu_items[done:upto]:
            piece()
        done = upto
    for piece in vpu_items[done:]:
        piece()


def _load_weight_as_bf16(w_hbm, w_ref, stage, sem, cols):
    a, r, n = w_hbm.shape

    def chunk(c, slot):
        return pltpu.make_async_copy(w_hbm.at[:, :, pl.ds(c * cols, cols)], stage(slot), sem.at[slot])

    chunk(0, 0).start()
    for c in range(n // cols):
        slot = c % 2
        if c + 1 < n // cols:
            chunk(c + 1, 1 - slot).start()
        chunk(c, slot).wait()
        for k in range(a):
            w_ref[k * r:(k + 1) * r, c * cols:(c + 1) * cols] = stage(slot)[k].astype(_BF16)


def _lane_block_transpose(xs):
    nb = len(xs)
    diag = _diagonal_gather(xs)
    moved = [w if d == 0 else pltpu.roll(w, d * SSM_GROUP, axis=1) for d, w in enumerate(diag)]
    back = _diagonal_gather(moved)
    return [back[(-b) % nb] for b in range(nb)]


def _diagonal_gather(arrs):
    assert len(arrs) == 8
    c = lax.broadcasted_iota(jnp.int32, (1, LANES), 1) // SSM_GROUP
    c0, c1, c2 = c & 1, (c >> 1) & 1, (c >> 2) & 1
    s1 = {(k0, hi): jnp.where((c0 ^ k0) == 1, arrs[2 * hi + 1], arrs[2 * hi])
          for k0 in range(2) for hi in range(4)}
    s2 = {}
    for k0 in range(2):
        for k1 in range(2):
            e1 = c1 ^ k1 ^ (c0 & k0)
            for top in range(2):
                s2[(k0, k1, top)] = jnp.where(e1 == 1, s1[(k0, 2 * top + 1)], s1[(k0, 2 * top)])
    out = []
    for k in range(8):
        k0, k1, k2 = k & 1, (k >> 1) & 1, (k >> 2) & 1
        carry0 = c0 & k0
        carry1 = (c1 & k1) | (c1 & carry0) | (k1 & carry0)
        e2 = c2 ^ k2 ^ carry1
        out.append(jnp.where(e2 == 1, s2[(k0, k1, 1)], s2[(k0, k1, 0)]))
    return out


def _inproj_kernel(x_ref, g_ref, w_hbm, cw_ref, cb_ref, wglu_ref, wout_ref,
                   yc_ref, ug_ref, zs_ref, wglu_bf_ref, wout_bf_ref,
                   carry_ref, slab_ref, raw_ref, w_ref, sem, *, tiles_per_seq):
    tm = x_ref.shape[0]
    nb, sub = GROUPS_PER_BLOCK, SUBLANES
    slabs_per_block = COL_BLOCK // LANES
    n_blocks = D_CONV // COL_BLOCK
    bases = (4 * D_CONV, 0, D_CONV, 2 * D_CONV, 3 * D_CONV, 4 * D_CONV + D_SSM)

    @pl.when(pl.program_id(0) == 0)
    def _():
        _load_weight_as_bf16(w_hbm, w_ref, lambda slot: raw_ref.at[slot, pl.ds(0, w_hbm.shape[0])],
                             sem, COL_BLOCK)

    @pl.when(pl.program_id(0) % tiles_per_seq == 0)
    def _():
        carry_ref[...] = jnp.zeros_like(carry_ref)

    x = x_ref[...]
    ms = jnp.mean(x * x, axis=-1, keepdims=True)
    h = (x * lax.rsqrt(ms + EPS) * g_ref[...]).astype(_BF16)

    wglu_bf_ref[...] = wglu_ref[...].astype(_BF16)
    wout_bf_ref[...] = wout_ref[...].astype(_BF16)

    def project_one(blk, j):
        lo = bases[j] + blk * COL_BLOCK
        res = _dot(h, w_ref[:, lo:lo + COL_BLOCK])
        if j == 0:
            for k in range(slabs_per_block):
                slab_ref[blk * slabs_per_block + k] = res[:, k * LANES:(k + 1) * LANES]
        else:
            raw_ref[blk % 2, j - 1] = res

    def project(blk):
        return [functools.partial(project_one, blk, j) for j in range(len(bases))]

    def finish_gate(blk):
        sl = slice(blk * COL_BLOCK, (blk + 1) * COL_BLOCK)
        z_ssm = raw_ref[blk % 2, 4]
        zs_ref[:, sl] = (z_ssm * _sigmoid(z_ssm)).astype(_BF16)

    def finish_u(gv):
        rows = [slab_ref[gv, pl.ds(tl, tm // sub, stride=sub), :] for tl in range(sub)]
        for gl, o in enumerate(_lane_block_transpose(rows)):
            ug_ref[gv * nb + gl] = o

    def finish_conv(blk, half):
        sl = slice(blk * COL_BLOCK + half * LANES, blk * COL_BLOCK + (half + 1) * LANES)
        hl = slice(half * LANES, (half + 1) * LANES)
        b_gate, c_gate, v, z = (raw_ref[blk % 2, j, :, hl] for j in range(4))
        cv = c_gate * v
        ext = jnp.concatenate([carry_ref[:, sl], cv], axis=0)
        conv = (cb_ref[:, sl] + cw_ref[2:3, sl] * cv + cw_ref[1:2, sl] * ext[sub - 1:sub - 1 + tm]
                + cw_ref[0:1, sl] * ext[sub - 2:sub - 2 + tm])
        carry_ref[:, sl] = cv[tm - sub:]
        yc_ref[:, sl] = (b_gate * conv * (z * _sigmoid(z))).astype(_BF16)

    def finish(blk):
        return ([functools.partial(finish_gate, blk)]
                + [functools.partial(finish_u, blk * slabs_per_block + k) for k in range(slabs_per_block)]
                + [functools.partial(finish_conv, blk, half) for half in range(COL_BLOCK // LANES)])

    _interleave(project(0), [])
    for blk in range(1, n_blocks):
        _interleave(project(blk), finish(blk - 1))
    _interleave([], finish(n_blocks - 1))


def _inproj(x2d, norm_g, w_in, conv_w, conv_b, w_glu, w_out, seq_len):
    n_tok = x2d.shape[0]
    tm = TOKEN_TILE
    steps = n_tok // tm
    const = lambda s: (0, 0)
    tile = lambda w: pl.BlockSpec((tm, w), lambda s: (s, 0))
    row_block = lambda w: pl.BlockSpec((w.shape[0] // steps, w.shape[1]), lambda s: (s, 0))
    out_sds = jax.ShapeDtypeStruct((n_tok, D_CONV), _BF16)
    return pl.pallas_call(
        functools.partial(_inproj_kernel, tiles_per_seq=seq_len // tm),
        grid=(steps,),
        in_specs=[
            tile(D_MODEL),
            pl.BlockSpec((1, D_MODEL), const),
            pl.BlockSpec(memory_space=pl.ANY),
            pl.BlockSpec((3, D_CONV), const),
            pl.BlockSpec((1, D_CONV), const),
            row_block(w_glu), row_block(w_out),
        ],
        out_specs=[tile(D_CONV),
                   pl.BlockSpec((SSM_GROUPS, tm // SUBLANES, LANES), lambda s: (0, s, 0)),
                   tile(D_SSM),
                   row_block(w_glu), row_block(w_out)],
        out_shape=[out_sds,
                   jax.ShapeDtypeStruct((SSM_GROUPS, n_tok // SUBLANES, LANES), _F32),
                   out_sds,
                   jax.ShapeDtypeStruct(w_glu.shape, _BF16),
                   jax.ShapeDtypeStruct(w_out.shape, _BF16)],
        scratch_shapes=[pltpu.VMEM((SUBLANES, D_CONV), _F32),
                        pltpu.VMEM((D_SSM // LANES, tm, LANES), _F32),
                        pltpu.VMEM((2, 5, tm, COL_BLOCK), _F32),
                        pltpu.VMEM((D_MODEL, N_IN), _BF16),
                        pltpu.SemaphoreType.DMA((2,))],
        compiler_params=pltpu.CompilerParams(dimension_semantics=("arbitrary",),
                                             vmem_limit_bytes=VMEM_LIMIT_BYTES),
        name="inproj_conv",
    )(x2d, norm_g, w_in.reshape(D_MODEL // tm, tm, N_IN), conv_w, conv_b, w_glu, w_out)


def _complex_powers(ar, ai, exponent, nbits):
    pr = jnp.ones((exponent.shape[0], ar.shape[1]), _F32)
    pi = jnp.zeros_like(pr)
    cr, ci = ar, ai
    for b in range(nbits):
        bit = ((exponent >> b) & 1) == 1
        fr = jnp.where(bit, cr, 1.0)
        fi = jnp.where(bit, ci, 0.0)
        pr, pi = pr * fr - pi * fi, pr * fi + pi * fr
        cr, ci = cr * cr - ci * ci, 2.0 * cr * ci
    return pr, pi


def _re_im_sign():
    lane = lax.broadcasted_iota(jnp.int32, (1, LANES), 1)
    return jnp.where(lane < SSM_STATE, -1.0, 1.0).astype(_F32)


def _s5_prepare_group(g, group, are_ref, aim_ref, mats_ref, kf_ref, ws_ref, wo_ref, a1_ref, a2_ref):
    t, th, jrows, h = S5_CHUNK, S5_TH, S5_POW_ROWS, SSM_GROUP
    sgn = _re_im_sign()
    nbits = max(1, (jrows - 1).bit_length())
    j = lax.broadcasted_iota(jnp.int32, (jrows, 1), 0)
    diag = (lax.broadcasted_iota(jnp.int32, (h, th), 0) == lax.broadcasted_iota(jnp.int32, (h, th), 1))
    a_re, a_im = are_ref[group], aim_ref[group]
    lr = jnp.concatenate([a_re, a_re], axis=1)
    li = jnp.concatenate([a_im, a_im], axis=1)
    ldt = jnp.broadcast_to(mats_ref[g, 0:1, 4 * LANES + h:4 * LANES + h + 1], (1, LANES))
    d_row = jnp.concatenate([mats_ref[g, 0:1, 4 * LANES:4 * LANES + h], jnp.zeros((1, th - h), _F32)],
                            axis=1)
    bt1, bt2, ct1, ct2 = (mats_ref[g, :, k * LANES:(k + 1) * LANES] for k in range(4))
    dt = jnp.exp(ldt)
    mag = jnp.exp(lr * dt)
    ar, ai = mag * jnp.cos(li * dt), mag * jnp.sin(li * dt)
    nr, ni = ar - 1.0, ai
    den = lr * lr + li * li
    qr = (nr * lr + ni * li) / den
    qi = (ni * lr - nr * li) / den
    b1 = qr * bt1 + (qi * sgn) * bt2
    b2 = (qr * sgn) * bt2 - qi * bt1
    c1 = ct1 * (-sgn)
    c2 = -ct2
    pr, pi = _complex_powers(ar, ai, j, nbits)
    rr, ri = _complex_powers(ar, ai, jnp.maximum(t - 1 - j, 0), nbits)
    cps = (pr[:, None, :] * c1[None] + pi[:, None, :] * c2[None]).reshape(jrows * h, LANES)
    cps = cps.astype(_BF16)
    b_hi = b1.astype(_BF16)
    b_lo = (b1 - b_hi.astype(_F32)).astype(_BF16)
    kf_ref[g] = _dot_nt(b_hi, cps[:th]) + _dot_nt(b_lo, cps[:th]) + jnp.where(diag, d_row, 0.0)
    wo_ref[g] = cps[h:h + th]
    ws_ref[g] = (rr[:t, None, :] * b1[None] + ri[:t, None, :] * b2[None]).reshape(th, LANES).astype(_BF16)
    a1_ref[g] = pr[t:t + 1]
    a2_ref[g] = pi[t:t + 1]


def _s5_kernel(ug_ref, are_ref, aim_ref, mats_ref, yg_ref,
               kf_ref, ws_ref, wo_ref, a1_ref, a2_ref,
               mt_ref, xg_ref, yi_ref, sl_ref, slsw_ref, sin_ref, *, chunks_per_seq):
    nb = ug_ref.shape[0]
    t, th, h = S5_CHUNK, S5_TH, SSM_GROUP
    t_hi_n = t // SUBLANES
    n_chunks = ug_ref.shape[1] // t_hi_n
    n_seq = n_chunks // chunks_per_seq
    pitch = S5_SCAN_PITCH
    assert n_seq * nb <= pitch

    for g in range(nb):
        _s5_prepare_group(g, pl.program_id(0) * nb + g, are_ref, aim_ref, mats_ref,
                          kf_ref, ws_ref, wo_ref, a1_ref, a2_ref)
        xg = jnp.concatenate([ug_ref[g, pl.ds(k, n_chunks, stride=t_hi_n), :]
                              for k in range(t_hi_n)], axis=1).astype(_BF16)
        xg_ref[g] = xg
        s = _dot(xg, ws_ref[g])
        s_sw = pltpu.roll(s, SSM_STATE, axis=1)
        for b in range(n_seq):
            rows = slice(b * chunks_per_seq, (b + 1) * chunks_per_seq)
            sl_ref[pl.ds(b * nb + g, chunks_per_seq, stride=pitch), :] = s[rows]
            slsw_ref[pl.ds(b * nb + g, chunks_per_seq, stride=pitch), :] = s_sw[rows]

    a1 = jnp.concatenate([a1_ref[g] for g in range(nb)] * n_seq, axis=0)
    a2 = jnp.concatenate([a2_ref[g] for g in range(nb)] * n_seq, axis=0) * _re_im_sign()
    n_rows = n_seq * nb

    def times(a, z):
        return a[0] * z[0] + a[1] * z[1], a[0] * z[1] - a[1] * z[0]

    def plus(z, w):
        return z[0] + w[0], z[1] + w[1]

    def local(c):
        return (sl_ref[c * pitch:c * pitch + n_rows, :], slsw_ref[c * pitch:c * pitch + n_rows, :])

    radix = S5_SCAN_RADIX
    powers = [(a1, a2)]
    for _ in range(radix - 1):
        pr, pi_signed = powers[-1]
        sgn = _re_im_sign()
        ar, ai, qr, qi = a1, a2 * sgn, pr, pi_signed * sgn
        powers.append((qr * ar - qi * ai, (qr * ai + qi * ar) * sgn))
    s = (jnp.zeros((n_rows, LANES), _F32), jnp.zeros((n_rows, LANES), _F32))
    for c0 in range(0, chunks_per_seq, radix):
        partial = None
        for j in range(radix):
            if j == 0:
                entering = s[0]
            else:
                entering = powers[j - 1][0] * s[0] + powers[j - 1][1] * s[1] + partial[0]
            sin_ref[(c0 + j) * pitch:(c0 + j) * pitch + n_rows, :] = entering
            partial = local(c0 + j) if partial is None else plus(times(powers[0], partial), local(c0 + j))
        s = plus(times(powers[radix - 1], s), partial)

    for g in range(nb):
        zk = jnp.concatenate([jnp.zeros((h, th), _F32), kf_ref[g]], axis=1)
        for k in range(t):
            mt_ref[g, k * h:(k + 1) * h, :] = zk[:, th - k * h:2 * th - k * h].astype(_BF16)
        yi_ref[g] = _dot(xg_ref[g], mt_ref[g])

    for g in range(nb):
        s_in = jnp.concatenate([sin_ref[pl.ds(b * nb + g, chunks_per_seq, stride=pitch), :]
                                for b in range(n_seq)], axis=0).astype(_BF16)
        y = yi_ref[g] + _dot_nt(s_in, wo_ref[g])
        for k in range(t_hi_n):
            yg_ref[g, pl.ds(k, n_chunks, stride=t_hi_n), :] = y[:, k * LANES:(k + 1) * LANES]


def _s5_core(ug, a_re, a_im, mats, chunks_per_seq):
    _, n_rows, _ = ug.shape
    h, th, nb = SSM_GROUP, S5_TH, GROUPS_PER_BLOCK
    group_rows = pl.BlockSpec((nb, n_rows, LANES), lambda i: (i, 0, 0))
    return pl.pallas_call(
        functools.partial(_s5_kernel, chunks_per_seq=chunks_per_seq),
        grid=(SSM_GROUPS // nb,),
        in_specs=[group_rows]
        + [pl.BlockSpec(p.shape, lambda i: (0, 0, 0)) for p in (a_re, a_im)]
        + [pl.BlockSpec((nb, h, mats.shape[-1]), lambda i: (i, 0, 0))],
        out_specs=group_rows,
        out_shape=jax.ShapeDtypeStruct(ug.shape, _F32),
        scratch_shapes=[pltpu.VMEM((nb, h, th), _F32),
                        pltpu.VMEM((nb, th, LANES), _BF16),
                        pltpu.VMEM((nb, th, LANES), _BF16),
                        pltpu.VMEM((nb, 1, LANES), _F32),
                        pltpu.VMEM((nb, 1, LANES), _F32),
                        pltpu.VMEM((nb, th, th), _BF16),
                        pltpu.VMEM((nb, n_rows * SUBLANES // S5_CHUNK, th), _BF16),
                        pltpu.VMEM((nb, n_rows * SUBLANES // S5_CHUNK, th), _F32)]
        + [pltpu.VMEM((chunks_per_seq * S5_SCAN_PITCH, LANES), _F32)] * 3,
        compiler_params=pltpu.CompilerParams(dimension_semantics=("arbitrary",),
                                             vmem_limit_bytes=VMEM_LIMIT_BYTES),
        name="s5_core",
    )(ug, a_re, a_im, mats)


def _out_kernel(yc_ref, yg_ref, zs_ref, x_ref, wg_ref, bg_ref, wo_ref, gp_ref, o_ref, slab_ref, ys_ref):
    tm = x_ref.shape[0]
    nb, sub = GROUPS_PER_BLOCK, SUBLANES
    rp = tm // OUT_ROW_PARTS
    slabs_per_block = COL_BLOCK // LANES

    def activation(part):
        rows = slice(part * rp, (part + 1) * rp)
        grows = slice(part * rp // sub, (part + 1) * rp // sub)
        acc = {"pre": bg_ref[...], "y": []}

        def block(blk):
            gvs = range(blk * slabs_per_block, (blk + 1) * slabs_per_block)
            for gv in gvs:
                tok = _lane_block_transpose([yg_ref[gv * nb + gl, grows, :] for gl in range(nb)])
                for tl in range(sub):
                    slab_ref[gv, pl.ds(part * rp + tl, rp // sub, stride=sub), :] = tok[tl]
            y = jax.nn.gelu(jnp.concatenate([slab_ref[gv, rows, :] for gv in gvs], axis=1))
            acc["y"].append(y)
            acc["pre"] = acc["pre"] + _dot(y.astype(_BF16), wg_ref[blk * COL_BLOCK:(blk + 1) * COL_BLOCK, :])

        def gate():
            y = jnp.concatenate(acc["y"], axis=1)
            ys_ref[part] = (y * _sigmoid(acc["pre"]) * zs_ref[rows, :].astype(_F32)).astype(_BF16)

        return [functools.partial(block, blk) for blk in range(D_SSM // COL_BLOCK)] + [gate]

    def conv_projection(part):
        rows = slice(part * rp, (part + 1) * rp)

        def cols(c):
            sl = slice(c * OUT_COL_BLOCK, (c + 1) * OUT_COL_BLOCK)
            o_ref[rows, sl] = _dot(yc_ref[rows, :], wo_ref[:D_CONV, sl])

        return [functools.partial(cols, c) for c in range(D_MODEL // OUT_COL_BLOCK)]

    def ssm_projection(part):
        rows = slice(part * rp, (part + 1) * rp)

        def cols(c):
            sl = slice(c * OUT_COL_BLOCK, (c + 1) * OUT_COL_BLOCK)
            o_ref[rows, sl] += _dot(ys_ref[part], wo_ref[D_CONV:, sl])

        return [functools.partial(cols, c) for c in range(D_MODEL // OUT_COL_BLOCK)]

    def norm_residual(part):
        rows = slice(part * rp, (part + 1) * rp)
        o = o_ref[rows, :]
        ms = jnp.mean(o * o, axis=-1, keepdims=True)
        o_ref[rows, :] = x_ref[rows, :] + o * lax.rsqrt(ms + EPS) * gp_ref[...]

    for part in range(OUT_ROW_PARTS + 1):
        mxu, vpu = [], []
        if part > 0:
            mxu += ssm_projection(part - 1)
        if part > 1:
            vpu.append(functools.partial(norm_residual, part - 2))
        if part < OUT_ROW_PARTS:
            mxu += conv_projection(part)
            vpu += activation(part)
        _interleave(mxu, vpu)
    norm_residual(OUT_ROW_PARTS - 1)


def _out_proj(yc, yg, zs, x2d, w_glu_bf16, b_glu, w_out_bf16, norm_g):
    n_tok = x2d.shape[0]
    tm = TOKEN_TILE
    const = lambda i: (0, 0)
    tile = lambda w: pl.BlockSpec((tm, w), lambda i: (i, 0))
    return pl.pallas_call(
        _out_kernel,
        grid=(n_tok // tm,),
        in_specs=[
            tile(D_CONV),
            pl.BlockSpec((SSM_GROUPS, tm // SUBLANES, LANES), lambda i: (0, i, 0)),
            tile(D_SSM), tile(D_MODEL),
            pl.BlockSpec((D_SSM, D_SSM), const, pipeline_mode=pl.Buffered(1)),
            pl.BlockSpec((1, D_SSM), const),
            pl.BlockSpec((D_CONV + D_SSM, D_MODEL), const, pipeline_mode=pl.Buffered(1)),
            pl.BlockSpec((1, D_MODEL), const),
        ],
        out_specs=tile(D_MODEL),
        out_shape=jax.ShapeDtypeStruct((n_tok, D_MODEL), _F32),
        scratch_shapes=[pltpu.VMEM((D_SSM // LANES, tm, LANES), _F32),
                        pltpu.VMEM((OUT_ROW_PARTS, tm // OUT_ROW_PARTS, D_SSM), _BF16)],
        compiler_params=pltpu.CompilerParams(dimension_semantics=("arbitrary",),
                                             vmem_limit_bytes=VMEM_LIMIT_BYTES),
        name="glu_outproj",
    )(yc, yg, zs, x2d, w_glu_bf16, b_glu, w_out_bf16, norm_g)


def kernel(x, norm_pre_g, w_in, conv_w, conv_b, ssm_a_re, ssm_a_im, ssm_log_dt, ssm_b_re, ssm_b_im,
           ssm_c_re, ssm_c_im, ssm_d, w_glu, b_glu, w_out, norm_post_g):
    bsz, seq_len, _ = x.shape
    n_tok = bsz * seq_len
    g, p, h, t = SSM_GROUPS, SSM_STATE, SSM_GROUP, S5_CHUNK
    assert seq_len % TOKEN_TILE == 0 and TOKEN_TILE % t == 0
    x2d = x.reshape(n_tok, D_MODEL)

    yc, ug, zs, w_glu_bf16, w_out_bf16 = _inproj(
        x2d, norm_pre_g.reshape(1, -1), w_in, conv_w, conv_b.reshape(1, -1), w_glu, w_out, seq_len)

    brt, bit = ssm_b_re.transpose(0, 2, 1), ssm_b_im.transpose(0, 2, 1)
    tail = jnp.concatenate([ssm_d.reshape(g, h), ssm_log_dt.reshape(g, 1),
                            jnp.zeros((g, LANES - h - 1), _F32)], axis=-1)
    mats = jnp.concatenate([brt, bit, bit, brt, ssm_c_re, ssm_c_im, ssm_c_im, ssm_c_re,
                            jnp.broadcast_to(tail[:, None, :], (g, h, LANES))], axis=-1)
    yg = _s5_core(ug, ssm_a_re.reshape(g, 1, p), ssm_a_im.reshape(g, 1, p), mats, seq_len // t)

    out = _out_proj(yc, yg, zs, x2d, w_glu_bf16, b_glu.reshape(1, -1), w_out_bf16,
                    norm_post_g.reshape(1, -1))
    return out.reshape(bsz, seq_len, D_MODEL)
```

```python
import functools

import jax
import jax.numpy as jnp
from jax import lax
from jax.experimental import pallas as pl
from jax.experimental.pallas import tpu as pltpu

D_MODEL = 2048
D_CONV = 1024
D_SSM = 1024
SSM_GROUP = 16
SSM_GROUPS = 64
SSM_STATE = 64
N_IN = 4 * D_CONV + 2 * D_SSM
EPS = 1e-6

S5_CHUNK = 16
S5_TH = S5_CHUNK * SSM_GROUP
S5_POW_ROWS = S5_CHUNK + 8
S5_SCAN_PITCH = 24
S5_SCAN_RADIX = 4
LANES = 128
SUBLANES = 8
GROUPS_PER_BLOCK = LANES // SSM_GROUP
TOKEN_TILE = 512
COL_BLOCK = 256
OUT_ROW_PARTS = 2
OUT_COL_BLOCK = 512
VMEM_LIMIT_BYTES = 56 * 1024 * 1024

_F32 = jnp.float32
_BF16 = jnp.bfloat16


def _dot(a, b):
    return jnp.dot(a, b, preferred_element_type=_F32)


def _dot_nt(a, b):
    return lax.dot_general(a, b, (((1,), (1,)), ((), ())), preferred_element_type=_F32)


def _sigmoid(z):
    return 1.0 / (1.0 + jnp.exp(-z))


def _interleave(mxu_items, vpu_items):
    n = max(len(mxu_items), 1)
    done = 0
    for i, item in enumerate(mxu_items):
        item()
        upto = -(-len(vpu_items) * (i + 1) // n)
        for piece in vpu_items[done:upto]:
            piece()
        done = upto
    for piece in vpu_items[done:]:
        piece()


def _load_weight_as_bf16(w_hbm, w_ref, stages, sem, cols):
    a, r, n = w_hbm.shape
    slots, n_chunks = len(stages), n // cols

    def chunk(c):
        return pltpu.make_async_copy(w_hbm.at[:, :, pl.ds(c * cols, cols)], stages[c % slots],
                                     sem.at[c % slots])

    for c in range(min(slots - 1, n_chunks)):
        chunk(c).start()
    for c in range(n_chunks):
        if c + slots - 1 < n_chunks:
            chunk(c + slots - 1).start()
        chunk(c).wait()
        for k in range(a):
            w_ref[k * r:(k + 1) * r, c * cols:(c + 1) * cols] = stages[c % slots][k].astype(_BF16)


def _lane_block_transpose(xs):
    nb = len(xs)
    diag = _diagonal_gather(xs)
    moved = [w if d == 0 else pltpu.roll(w, d * SSM_GROUP, axis=1) for d, w in enumerate(diag)]
    back = _diagonal_gather(moved)
    return [back[(-b) % nb] for b in range(nb)]


def _diagonal_gather(arrs):
    assert len(arrs) == 8
    c = lax.broadcasted_iota(jnp.int32, (1, LANES), 1) // SSM_GROUP
    c0, c1, c2 = c & 1, (c >> 1) & 1, (c >> 2) & 1
    s1 = {(k0, hi): jnp.where((c0 ^ k0) == 1, arrs[2 * hi + 1], arrs[2 * hi])
          for k0 in range(2) for hi in range(4)}
    s2 = {}
    for k0 in range(2):
        for k1 in range(2):
            e1 = c1 ^ k1 ^ (c0 & k0)
            for top in range(2):
                s2[(k0, k1, top)] = jnp.where(e1 == 1, s1[(k0, 2 * top + 1)], s1[(k0, 2 * top)])
    out = []
    for k in range(8):
        k0, k1, k2 = k & 1, (k >> 1) & 1, (k >> 2) & 1
        carry0 = c0 & k0
        carry1 = (c1 & k1) | (c1 & carry0) | (k1 & carry0)
        e2 = c2 ^ k2 ^ carry1
        out.append(jnp.where(e2 == 1, s2[(k0, k1, 1)], s2[(k0, k1, 0)]))
    return out


def _inproj_kernel(x_ref, g_ref, w_hbm, cw_ref, cb_ref, wglu_ref, wout_ref,
                   yc_ref, ug_ref, zs_ref, wglu_bf_ref, wout_bf_ref,
                   carry_ref, slab_ref, raw_ref, w_ref, sem, *, tiles_per_seq):
    tm = x_ref.shape[0]
    nb, sub = GROUPS_PER_BLOCK, SUBLANES
    slabs_per_block = COL_BLOCK // LANES
    n_blocks = D_CONV // COL_BLOCK
    bases = (4 * D_CONV, 0, D_CONV, 2 * D_CONV, 3 * D_CONV, 4 * D_CONV + D_SSM)

    @pl.when(pl.program_id(0) == 0)
    def _():
        stages = [raw_ref.at[slot, pl.ds(0, w_hbm.shape[0]), :, pl.ds(half * LANES, LANES)]
                  for slot in range(raw_ref.shape[0]) for half in range(COL_BLOCK // LANES)]
        _load_weight_as_bf16(w_hbm, w_ref, stages, sem, LANES)

    @pl.when(pl.program_id(0) % tiles_per_seq == 0)
    def _():
        carry_ref[...] = jnp.zeros_like(carry_ref)

    x = x_ref[...]
    ms = jnp.mean(x * x, axis=-1, keepdims=True)
    h = (x * lax.rsqrt(ms + EPS) * g_ref[...]).astype(_BF16)

    wglu_bf_ref[...] = wglu_ref[...].astype(_BF16)
    wout_bf_ref[...] = wout_ref[...].astype(_BF16)

    def project_one(blk, j):
        lo = bases[j] + blk * COL_BLOCK
        res = _dot(h, w_ref[:, lo:lo + COL_BLOCK])
        if j == 0:
            for k in range(slabs_per_block):
                slab_ref[blk * slabs_per_block + k] = res[:, k * LANES:(k + 1) * LANES]
        else:
            raw_ref[blk % 2, j - 1] = res

    def project(blk):
        return [functools.partial(project_one, blk, j) for j in range(len(bases))]

    def finish_gate(blk):
        sl = slice(blk * COL_BLOCK, (blk + 1) * COL_BLOCK)
        z_ssm = raw_ref[blk % 2, 4]
        zs_ref[:, sl] = (z_ssm * _sigmoid(z_ssm)).astype(_BF16)

    def finish_u(gv):
        rows = [slab_ref[gv, pl.ds(tl, tm // sub, stride=sub), :] for tl in range(sub)]
        for gl, o in enumerate(_lane_block_transpose(rows)):
            ug_ref[gv * nb + gl] = o

    def finish_conv(blk, half):
        sl = slice(blk * COL_BLOCK + half * LANES, blk * COL_BLOCK + (half + 1) * LANES)
        hl = slice(half * LANES, (half + 1) * LANES)
        b_gate, c_gate, v, z = (raw_ref[blk % 2, j, :, hl] for j in range(4))
        cv = c_gate * v
        ext = jnp.concatenate([carry_ref[:, sl], cv], axis=0)
        conv = (cb_ref[:, sl] + cw_ref[2:3, sl] * cv + cw_ref[1:2, sl] * ext[sub - 1:sub - 1 + tm]
                + cw_ref[0:1, sl] * ext[sub - 2:sub - 2 + tm])
        carry_ref[:, sl] = cv[tm - sub:]
        yc_ref[:, sl] = (b_gate * conv * (z * _sigmoid(z))).astype(_BF16)

    def finish(blk):
        return ([functools.partial(finish_gate, blk)]
                + [functools.partial(finish_u, blk * slabs_per_block + k) for k in range(slabs_per_block)]
                + [functools.partial(finish_conv, blk, half) for half in range(COL_BLOCK // LANES)])

    _interleave(project(0), [])
    for blk in range(1, n_blocks):
        _interleave(project(blk), finish(blk - 1))
    _interleave([], finish(n_blocks - 1))


def _inproj(x2d, norm_g, w_in, conv_w, conv_b, w_glu, w_out, seq_len):
    n_tok = x2d.shape[0]
    tm = TOKEN_TILE
    steps = n_tok // tm
    const = lambda s: (0, 0)
    tile = lambda w: pl.BlockSpec((tm, w), lambda s: (s, 0))
    row_block = lambda w: pl.BlockSpec((w.shape[0] // steps, w.shape[1]), lambda s: (s, 0))
    out_sds = jax.ShapeDtypeStruct((n_tok, D_CONV), _BF16)
    return pl.pallas_call(
        functools.partial(_inproj_kernel, tiles_per_seq=seq_len // tm),
        grid=(steps,),
        in_specs=[
            tile(D_MODEL),
            pl.BlockSpec((1, D_MODEL), const),
            pl.BlockSpec(memory_space=pl.ANY),
            pl.BlockSpec((3, D_CONV), const),
            pl.BlockSpec((1, D_CONV), const),
            row_block(w_glu), row_block(w_out),
        ],
        out_specs=[tile(D_CONV),
                   pl.BlockSpec((SSM_GROUPS, tm // SUBLANES, LANES), lambda s: (0, s, 0)),
                   tile(D_SSM),
                   row_block(w_glu), row_block(w_out)],
        out_shape=[out_sds,
                   jax.ShapeDtypeStruct((SSM_GROUPS, n_tok // SUBLANES, LANES), _F32),
                   out_sds,
                   jax.ShapeDtypeStruct(w_glu.shape, _BF16),
                   jax.ShapeDtypeStruct(w_out.shape, _BF16)],
        scratch_shapes=[pltpu.VMEM((SUBLANES, D_CONV), _F32),
                        pltpu.VMEM((D_SSM // LANES, tm, LANES), _F32),
                        pltpu.VMEM((2, 5, tm, COL_BLOCK), _F32),
                        pltpu.VMEM((D_MODEL, N_IN), _BF16),
                        pltpu.SemaphoreType.DMA((2 * COL_BLOCK // LANES,))],
        compiler_params=pltpu.CompilerParams(dimension_semantics=("arbitrary",),
                                             vmem_limit_bytes=VMEM_LIMIT_BYTES),
        name="inproj_conv",
    )(x2d, norm_g, w_in.reshape(D_MODEL // tm, tm, N_IN), conv_w, conv_b, w_glu, w_out)


def _complex_powers(ar, ai, exponent, nbits):
    pr = jnp.ones((exponent.shape[0], ar.shape[1]), _F32)
    pi = jnp.zeros_like(pr)
    cr, ci = ar, ai
    for b in range(nbits):
        bit = ((exponent >> b) & 1) == 1
        fr = jnp.where(bit, cr, 1.0)
        fi = jnp.where(bit, ci, 0.0)
        pr, pi = pr * fr - pi * fi, pr * fi + pi * fr
        cr, ci = cr * cr - ci * ci, 2.0 * cr * ci
    return pr, pi


def _re_im_sign():
    lane = lax.broadcasted_iota(jnp.int32, (1, LANES), 1)
    return jnp.where(lane < SSM_STATE, -1.0, 1.0).astype(_F32)


def _s5_prepare_group(g, group, are_ref, aim_ref, mats_ref, kf_ref, ws_ref, wo_ref, a1_ref, a2_ref):
    t, th, jrows, h = S5_CHUNK, S5_TH, S5_POW_ROWS, SSM_GROUP
    sgn = _re_im_sign()
    nbits = max(1, (jrows - 1).bit_length())
    j = lax.broadcasted_iota(jnp.int32, (jrows, 1), 0)
    diag = (lax.broadcasted_iota(jnp.int32, (h, th), 0) == lax.broadcasted_iota(jnp.int32, (h, th), 1))
    a_re, a_im = are_ref[group], aim_ref[group]
    lr = jnp.concatenate([a_re, a_re], axis=1)
    li = jnp.concatenate([a_im, a_im], axis=1)
    ldt = jnp.broadcast_to(mats_ref[g, 0:1, 4 * LANES + h:4 * LANES + h + 1], (1, LANES))
    d_row = jnp.concatenate([mats_ref[g, 0:1, 4 * LANES:4 * LANES + h], jnp.zeros((1, th - h), _F32)],
                            axis=1)
    bt1, bt2, ct1, ct2 = (mats_ref[g, :, k * LANES:(k + 1) * LANES] for k in range(4))
    dt = jnp.exp(ldt)
    mag = jnp.exp(lr * dt)
    ar, ai = mag * jnp.cos(li * dt), mag * jnp.sin(li * dt)
    nr, ni = ar - 1.0, ai
    den = lr * lr + li * li
    qr = (nr * lr + ni * li) / den
    qi = (ni * lr - nr * li) / den
    b1 = qr * bt1 + (qi * sgn) * bt2
    b2 = (qr * sgn) * bt2 - qi * bt1
    c1 = ct1 * (-sgn)
    c2 = -ct2
    pr, pi = _complex_powers(ar, ai, j, nbits)
    rr, ri = _complex_powers(ar, ai, jnp.maximum(t - 1 - j, 0), nbits)
    cps = (pr[:, None, :] * c1[None] + pi[:, None, :] * c2[None]).reshape(jrows * h, LANES)
    cps = cps.astype(_BF16)
    b_hi = b1.astype(_BF16)
    b_lo = (b1 - b_hi.astype(_F32)).astype(_BF16)
    kf_ref[g] = _dot_nt(b_hi, cps[:th]) + _dot_nt(b_lo, cps[:th]) + jnp.where(diag, d_row, 0.0)
    wo_ref[g] = cps[h:h + th]
    ws_ref[g] = (rr[:t, None, :] * b1[None] + ri[:t, None, :] * b2[None]).reshape(th, LANES).astype(_BF16)
    a1_ref[g] = pr[t:t + 1]
    a2_ref[g] = pi[t:t + 1]


def _s5_kernel(ug_ref, are_ref, aim_ref, mats_ref, yg_ref,
               kf_ref, ws_ref, wo_ref, a1_ref, a2_ref,
               mt_ref, xg_ref, yi_ref, sl_ref, slsw_ref, sin_ref, *, chunks_per_seq):
    nb = ug_ref.shape[0]
    t, th, h = S5_CHUNK, S5_TH, SSM_GROUP
    t_hi_n = t // SUBLANES
    n_chunks = ug_ref.shape[1] // t_hi_n
    n_seq = n_chunks // chunks_per_seq
    pitch = S5_SCAN_PITCH
    assert n_seq * nb <= pitch

    for g in range(nb):
        _s5_prepare_group(g, pl.program_id(0) * nb + g, are_ref, aim_ref, mats_ref,
                          kf_ref, ws_ref, wo_ref, a1_ref, a2_ref)
        xg = jnp.concatenate([ug_ref[g, pl.ds(k, n_chunks, stride=t_hi_n), :]
                              for k in range(t_hi_n)], axis=1).astype(_BF16)
        xg_ref[g] = xg
        s = _dot(xg, ws_ref[g])
        s_sw = pltpu.roll(s, SSM_STATE, axis=1)
        for b in range(n_seq):
            rows = slice(b * chunks_per_seq, (b + 1) * chunks_per_seq)
            sl_ref[pl.ds(b * nb + g, chunks_per_seq, stride=pitch), :] = s[rows]
            slsw_ref[pl.ds(b * nb + g, chunks_per_seq, stride=pitch), :] = s_sw[rows]

    a1 = jnp.concatenate([a1_ref[g] for g in range(nb)] * n_seq, axis=0)
    a2 = jnp.concatenate([a2_ref[g] for g in range(nb)] * n_seq, axis=0) * _re_im_sign()
    n_rows = n_seq * nb

    def times(a, z):
        return a[0] * z[0] + a[1] * z[1], a[0] * z[1] - a[1] * z[0]

    def plus(z, w):
        return z[0] + w[0], z[1] + w[1]

    def local(c):
        return (sl_ref[c * pitch:c * pitch + n_rows, :], slsw_ref[c * pitch:c * pitch + n_rows, :])

    radix = S5_SCAN_RADIX
    powers = [(a1, a2)]
    for _ in range(radix - 1):
        pr, pi_signed = powers[-1]
        sgn = _re_im_sign()
        ar, ai, qr, qi = a1, a2 * sgn, pr, pi_signed * sgn
        powers.append((qr * ar - qi * ai, (qr * ai + qi * ar) * sgn))
    s = (jnp.zeros((n_rows, LANES), _F32), jnp.zeros((n_rows, LANES), _F32))
    for c0 in range(0, chunks_per_seq, radix):
        partial = None
        for j in range(radix):
            if j == 0:
                entering = s[0]
            else:
                entering = powers[j - 1][0] * s[0] + powers[j - 1][1] * s[1] + partial[0]
            sin_ref[(c0 + j) * pitch:(c0 + j) * pitch + n_rows, :] = entering
            partial = local(c0 + j) if partial is None else plus(times(powers[0], partial), local(c0 + j))
        s = plus(times(powers[radix - 1], s), partial)

    for g in range(nb):
        zk = jnp.concatenate([jnp.zeros((h, th), _F32), kf_ref[g]], axis=1)
        for k in range(t):
            mt_ref[g, k * h:(k + 1) * h, :] = zk[:, th - k * h:2 * th - k * h].astype(_BF16)
        yi_ref[g] = _dot(xg_ref[g], mt_ref[g])

    for g in range(nb):
        s_in = jnp.concatenate([sin_ref[pl.ds(b * nb + g, chunks_per_seq, stride=pitch), :]
                                for b in range(n_seq)], axis=0).astype(_BF16)
        y = yi_ref[g] + _dot_nt(s_in, wo_ref[g])
        for k in range(t_hi_n):
            yg_ref[g, pl.ds(k, n_chunks, stride=t_hi_n), :] = y[:, k * LANES:(k + 1) * LANES]


def _s5_core(ug, a_re, a_im, mats, chunks_per_seq):
    _, n_rows, _ = ug.shape
    h, th, nb = SSM_GROUP, S5_TH, GROUPS_PER_BLOCK
    group_rows = pl.BlockSpec((nb, n_rows, LANES), lambda i: (i, 0, 0))
    return pl.pallas_call(
        functools.partial(_s5_kernel, chunks_per_seq=chunks_per_seq),
        grid=(SSM_GROUPS // nb,),
        in_specs=[group_rows]
        + [pl.BlockSpec(p.shape, lambda i: (0, 0, 0)) for p in (a_re, a_im)]
        + [pl.BlockSpec((nb, h, mats.shape[-1]), lambda i: (i, 0, 0))],
        out_specs=group_rows,
        out_shape=jax.ShapeDtypeStruct(ug.shape, _F32),
        scratch_shapes=[pltpu.VMEM((nb, h, th), _F32),
                        pltpu.VMEM((nb, th, LANES), _BF16),
                        pltpu.VMEM((nb, th, LANES), _BF16),
                        pltpu.VMEM((nb, 1, LANES), _F32),
                        pltpu.VMEM((nb, 1, LANES), _F32),
                        pltpu.VMEM((nb, th, th), _BF16),
                        pltpu.VMEM((nb, n_rows * SUBLANES // S5_CHUNK, th), _BF16),
                        pltpu.VMEM((nb, n_rows * SUBLANES // S5_CHUNK, th), _F32)]
        + [pltpu.VMEM((chunks_per_seq * S5_SCAN_PITCH, LANES), _F32)] * 3,
        compiler_params=pltpu.CompilerParams(dimension_semantics=("arbitrary",),
                                             vmem_limit_bytes=VMEM_LIMIT_BYTES),
        name="s5_core",
    )(ug, a_re, a_im, mats)


def _out_kernel(yc_ref, yg_ref, zs_ref, x_ref, wg_ref, bg_ref, wo_ref, gp_ref, o_ref, slab_ref, ys_ref):
    tm = x_ref.shape[0]
    nb, sub = GROUPS_PER_BLOCK, SUBLANES
    rp = tm // OUT_ROW_PARTS
    slabs_per_block = COL_BLOCK // LANES

    def activation(part):
        rows = slice(part * rp, (part + 1) * rp)
        grows = slice(part * rp // sub, (part + 1) * rp // sub)
        acc = {"pre": bg_ref[...], "y": []}

        def block(blk):
            gvs = range(blk * slabs_per_block, (blk + 1) * slabs_per_block)
            for gv in gvs:
                tok = _lane_block_transpose([yg_ref[gv * nb + gl, grows, :] for gl in range(nb)])
                for tl in range(sub):
                    slab_ref[gv, pl.ds(part * rp + tl, rp // sub, stride=sub), :] = tok[tl]
            y = jax.nn.gelu(jnp.concatenate([slab_ref[gv, rows, :] for gv in gvs], axis=1))
            acc["y"].append(y)
            acc["pre"] = acc["pre"] + _dot(y.astype(_BF16), wg_ref[blk * COL_BLOCK:(blk + 1) * COL_BLOCK, :])

        def gate():
            y = jnp.concatenate(acc["y"], axis=1)
            ys_ref[part] = (y * _sigmoid(acc["pre"]) * zs_ref[rows, :].astype(_F32)).astype(_BF16)

        return [functools.partial(block, blk) for blk in range(D_SSM // COL_BLOCK)] + [gate]

    def conv_projection(part):
        rows = slice(part * rp, (part + 1) * rp)

        def cols(c):
            sl = slice(c * OUT_COL_BLOCK, (c + 1) * OUT_COL_BLOCK)
            o_ref[rows, sl] = _dot(yc_ref[rows, :], wo_ref[:D_CONV, sl])

        return [functools.partial(cols, c) for c in range(D_MODEL // OUT_COL_BLOCK)]

    def ssm_projection(part):
        rows = slice(part * rp, (part + 1) * rp)

        def cols(c):
            sl = slice(c * OUT_COL_BLOCK, (c + 1) * OUT_COL_BLOCK)
            o_ref[rows, sl] += _dot(ys_ref[part], wo_ref[D_CONV:, sl])

        return [functools.partial(cols, c) for c in range(D_MODEL // OUT_COL_BLOCK)]

    def norm_residual(part):
        rows = slice(part * rp, (part + 1) * rp)
        o = o_ref[rows, :]
        ms = jnp.mean(o * o, axis=-1, keepdims=True)
        o_ref[rows, :] = x_ref[rows, :] + o * lax.rsqrt(ms + EPS) * gp_ref[...]

    for part in range(OUT_ROW_PARTS + 1):
        mxu, vpu = [], []
        if part > 0:
            mxu += ssm_projection(part - 1)
        if part > 1:
            vpu.append(functools.partial(norm_residual, part - 2))
        if part < OUT_ROW_PARTS:
            mxu += conv_projection(part)
            vpu += activation(part)
        _interleave(mxu, vpu)
    norm_residual(OUT_ROW_PARTS - 1)


def _out_proj(yc, yg, zs, x2d, w_glu_bf16, b_glu, w_out_bf16, norm_g):
    n_tok = x2d.shape[0]
    tm = TOKEN_TILE
    const = lambda i: (0, 0)
    tile = lambda w: pl.BlockSpec((tm, w), lambda i: (i, 0))
    return pl.pallas_call(
        _out_kernel,
        grid=(n_tok // tm,),
        in_specs=[
            tile(D_CONV),
            pl.BlockSpec((SSM_GROUPS, tm // SUBLANES, LANES), lambda i: (0, i, 0)),
            tile(D_SSM), tile(D_MODEL),
            pl.BlockSpec((D_SSM, D_SSM), const, pipeline_mode=pl.Buffered(1)),
            pl.BlockSpec((1, D_SSM), const),
            pl.BlockSpec((D_CONV + D_SSM, D_MODEL), const, pipeline_mode=pl.Buffered(1)),
            pl.BlockSpec((1, D_MODEL), const),
        ],
        out_specs=tile(D_MODEL),
        out_shape=jax.ShapeDtypeStruct((n_tok, D_MODEL), _F32),
        scratch_shapes=[pltpu.VMEM((D_SSM // LANES, tm, LANES), _F32),
                        pltpu.VMEM((OUT_ROW_PARTS, tm // OUT_ROW_PARTS, D_SSM), _BF16)],
        compiler_params=pltpu.CompilerParams(dimension_semantics=("arbitrary",),
                                             vmem_limit_bytes=VMEM_LIMIT_BYTES),
        name="glu_outproj",
    )(yc, yg, zs, x2d, w_glu_bf16, b_glu, w_out_bf16, norm_g)


def kernel(x, norm_pre_g, w_in, conv_w, conv_b, ssm_a_re, ssm_a_im, ssm_log_dt, ssm_b_re, ssm_b_im,
           ssm_c_re, ssm_c_im, ssm_d, w_glu, b_glu, w_out, norm_post_g):
    bsz, seq_len, _ = x.shape
    n_tok = bsz * seq_len
    g, p, h, t = SSM_GROUPS, SSM_STATE, SSM_GROUP, S5_CHUNK
    assert seq_len % TOKEN_TILE == 0 and TOKEN_TILE % t == 0
    x2d = x.reshape(n_tok, D_MODEL)

    yc, ug, zs, w_glu_bf16, w_out_bf16 = _inproj(
        x2d, norm_pre_g.reshape(1, -1), w_in, conv_w, conv_b.reshape(1, -1), w_glu, w_out, seq_len)

    brt, bit = ssm_b_re.transpose(0, 2, 1), ssm_b_im.transpose(0, 2, 1)
    tail = jnp.concatenate([ssm_d.reshape(g, h), ssm_log_dt.reshape(g, 1),
                            jnp.zeros((g, LANES - h - 1), _F32)], axis=-1)
    mats = jnp.concatenate([brt, bit, bit, brt, ssm_c_re, ssm_c_im, ssm_c_im, ssm_c_re,
                            jnp.broadcast_to(tail[:, None, :], (g, h, LANES))], axis=-1)
    yg = _s5_core(ug, ssm_a_re.reshape(g, 1, p), ssm_a_im.reshape(g, 1, p), mats, seq_len // t)

    out = _out_proj(yc, yg, zs, x2d, w_glu_bf16, b_glu.reshape(1, -1), w_out_bf16,
                    norm_post_g.reshape(1, -1))
    return out.reshape(bsz, seq_len, D_MODEL)
```

```python
import functools

import jax
import jax.numpy as jnp
from jax import lax
from jax.experimental import pallas as pl
from jax.experimental.pallas import tpu as pltpu

D_MODEL = 2048
D_CONV = 1024
D_SSM = 1024
SSM_GROUP = 16
SSM_GROUPS = 64
SSM_STATE = 64
N_IN = 4 * D_CONV + 2 * D_SSM
EPS = 1e-6

S5_CHUNK = 16
S5_TH = S5_CHUNK * SSM_GROUP
S5_POW_ROWS = S5_CHUNK + 8
S5_SCAN_PITCH = 24
S5_SCAN_RADIX = 4
LANES = 128
SUBLANES = 8
GROUPS_PER_BLOCK = LANES // SSM_GROUP
TOKEN_TILE = 512
COL_BLOCK = 256
WEIGHT_STAGES = 6
OUT_ROW_PARTS = 2
OUT_COL_BLOCK = 512
VMEM_LIMIT_BYTES = 56 * 1024 * 1024

_F32 = jnp.float32
_BF16 = jnp.bfloat16


def _dot(a, b):
    return jnp.dot(a, b, preferred_element_type=_F32)


def _dot_nt(a, b):
    return lax.dot_general(a, b, (((1,), (1,)), ((), ())), preferred_element_type=_F32)


def _sigmoid(z):
    return 1.0 / (1.0 + jnp.exp(-z))


def _interleave(mxu_items, vpu_items):
    n = max(len(mxu_items), 1)
    done = 0
    for i, item in enumerate(mxu_items):
        item()
        upto = -(-len(vpu_items) * (i + 1) // n)
        for piece in vpu_items[done:upto]:
            piece()
        done = upto
    for piece in vpu_items[done:]:
        piece()


def _load_weight_as_bf16(w_hbm, w_ref, stages, sem, cols):
    a, r, n = w_hbm.shape
    slots, n_chunks = len(stages), n // cols

    def chunk(c):
        return pltpu.make_async_copy(w_hbm.at[:, :, pl.ds(c * cols, cols)], stages[c % slots],
                                     sem.at[c % slots])

    for c in range(min(slots - 1, n_chunks)):
        chunk(c).start()
    for c in range(n_chunks):
        if c + slots - 1 < n_chunks:
            chunk(c + slots - 1).start()
        chunk(c).wait()
        for k in range(a):
            w_ref[k * r:(k + 1) * r, c * cols:(c + 1) * cols] = stages[c % slots][k].astype(_BF16)


def _lane_block_transpose(xs):
    nb = len(xs)
    diag = _diagonal_gather(xs)
    moved = [w if d == 0 else pltpu.roll(w, d * SSM_GROUP, axis=1) for d, w in enumerate(diag)]
    back = _diagonal_gather(moved)
    return [back[(-b) % nb] for b in range(nb)]


def _diagonal_gather(arrs):
    assert len(arrs) == 8
    c = lax.broadcasted_iota(jnp.int32, (1, LANES), 1) // SSM_GROUP
    c0, c1, c2 = c & 1, (c >> 1) & 1, (c >> 2) & 1
    s1 = {(k0, hi): jnp.where((c0 ^ k0) == 1, arrs[2 * hi + 1], arrs[2 * hi])
          for k0 in range(2) for hi in range(4)}
    s2 = {}
    for k0 in range(2):
        for k1 in range(2):
            e1 = c1 ^ k1 ^ (c0 & k0)
            for top in range(2):
                s2[(k0, k1, top)] = jnp.where(e1 == 1, s1[(k0, 2 * top + 1)], s1[(k0, 2 * top)])
    out = []
    for k in range(8):
        k0, k1, k2 = k & 1, (k >> 1) & 1, (k >> 2) & 1
        carry0 = c0 & k0
        carry1 = (c1 & k1) | (c1 & carry0) | (k1 & carry0)
        e2 = c2 ^ k2 ^ carry1
        out.append(jnp.where(e2 == 1, s2[(k0, k1, 1)], s2[(k0, k1, 0)]))
    return out


def _inproj_kernel(x_ref, g_ref, w_hbm, cw_ref, cb_ref, wglu_ref, wout_ref,
                   yc_ref, ug_ref, zs_ref, wglu_bf_ref, wout_bf_ref,
                   carry_ref, slab_ref, raw_ref, w_ref, sem, *, tiles_per_seq):
    tm = x_ref.shape[0]
    nb, sub = GROUPS_PER_BLOCK, SUBLANES
    slabs_per_block = COL_BLOCK // LANES
    n_blocks = D_CONV // COL_BLOCK
    bases = (4 * D_CONV, 0, D_CONV, 2 * D_CONV, 3 * D_CONV, 4 * D_CONV + D_SSM)

    @pl.when(pl.program_id(0) == 0)
    def _():
        n_a = w_hbm.shape[0]
        stages = [raw_ref.at[slot, pl.ds(0, n_a), :, pl.ds(half * LANES, LANES)]
                  for slot in range(raw_ref.shape[0]) for half in range(COL_BLOCK // LANES)]
        stages += [slab_ref.at[pl.ds(i * n_a, n_a)] for i in range(slab_ref.shape[0] // n_a)]
        assert len(stages) == WEIGHT_STAGES
        _load_weight_as_bf16(w_hbm, w_ref, stages, sem, LANES)

    @pl.when(pl.program_id(0) % tiles_per_seq == 0)
    def _():
        carry_ref[...] = jnp.zeros_like(carry_ref)

    x = x_ref[...]
    ms = jnp.mean(x * x, axis=-1, keepdims=True)
    h = (x * lax.rsqrt(ms + EPS) * g_ref[...]).astype(_BF16)

    wglu_bf_ref[...] = wglu_ref[...].astype(_BF16)
    wout_bf_ref[...] = wout_ref[...].astype(_BF16)

    def project_one(blk, j):
        lo = bases[j] + blk * COL_BLOCK
        res = _dot(h, w_ref[:, lo:lo + COL_BLOCK])
        if j == 0:
            for k in range(slabs_per_block):
                slab_ref[blk * slabs_per_block + k] = res[:, k * LANES:(k + 1) * LANES]
        else:
            raw_ref[blk % 2, j - 1] = res

    def project(blk):
        return [functools.partial(project_one, blk, j) for j in range(len(bases))]

    def finish_gate(blk):
        sl = slice(blk * COL_BLOCK, (blk + 1) * COL_BLOCK)
        z_ssm = raw_ref[blk % 2, 4]
        zs_ref[:, sl] = (z_ssm * _sigmoid(z_ssm)).astype(_BF16)

    def finish_u(gv):
        rows = [slab_ref[gv, pl.ds(tl, tm // sub, stride=sub), :] for tl in range(sub)]
        for gl, o in enumerate(_lane_block_transpose(rows)):
            ug_ref[gv * nb + gl] = o

    def finish_conv(blk, half):
        sl = slice(blk * COL_BLOCK + half * LANES, blk * COL_BLOCK + (half + 1) * LANES)
        hl = slice(half * LANES, (half + 1) * LANES)
        b_gate, c_gate, v, z = (raw_ref[blk % 2, j, :, hl] for j in range(4))
        cv = c_gate * v
        ext = jnp.concatenate([carry_ref[:, sl], cv], axis=0)
        conv = (cb_ref[:, sl] + cw_ref[2:3, sl] * cv + cw_ref[1:2, sl] * ext[sub - 1:sub - 1 + tm]
                + cw_ref[0:1, sl] * ext[sub - 2:sub - 2 + tm])
        carry_ref[:, sl] = cv[tm - sub:]
        yc_ref[:, sl] = (b_gate * conv * (z * _sigmoid(z))).astype(_BF16)

    def finish(blk):
        return ([functools.partial(finish_gate, blk)]
                + [functools.partial(finish_u, blk * slabs_per_block + k) for k in range(slabs_per_block)]
                + [functools.partial(finish_conv, blk, half) for half in range(COL_BLOCK // LANES)])

    _interleave(project(0), [])
    for blk in range(1, n_blocks):
        _interleave(project(blk), finish(blk - 1))
    _interleave([], finish(n_blocks - 1))


def _inproj(x2d, norm_g, w_in, conv_w, conv_b, w_glu, w_out, seq_len):
    n_tok = x2d.shape[0]
    tm = TOKEN_TILE
    steps = n_tok // tm
    const = lambda s: (0, 0)
    tile = lambda w: pl.BlockSpec((tm, w), lambda s: (s, 0))
    row_block = lambda w: pl.BlockSpec((w.shape[0] // steps, w.shape[1]), lambda s: (s, 0))
    out_sds = jax.ShapeDtypeStruct((n_tok, D_CONV), _BF16)
    return pl.pallas_call(
        functools.partial(_inproj_kernel, tiles_per_seq=seq_len // tm),
        grid=(steps,),
        in_specs=[
            tile(D_MODEL),
            pl.BlockSpec((1, D_MODEL), const),
            pl.BlockSpec(memory_space=pl.ANY),
            pl.BlockSpec((3, D_CONV), const),
            pl.BlockSpec((1, D_CONV), const),
            row_block(w_glu), row_block(w_out),
        ],
        out_specs=[tile(D_CONV),
                   pl.BlockSpec((SSM_GROUPS, tm // SUBLANES, LANES), lambda s: (0, s, 0)),
                   tile(D_SSM),
                   row_block(w_glu), row_block(w_out)],
        out_shape=[out_sds,
                   jax.ShapeDtypeStruct((SSM_GROUPS, n_tok // SUBLANES, LANES), _F32),
                   out_sds,
                   jax.ShapeDtypeStruct(w_glu.shape, _BF16),
                   jax.ShapeDtypeStruct(w_out.shape, _BF16)],
        scratch_shapes=[pltpu.VMEM((SUBLANES, D_CONV), _F32),
                        pltpu.VMEM((D_SSM // LANES, tm, LANES), _F32),
                        pltpu.VMEM((2, 5, tm, COL_BLOCK), _F32),
                        pltpu.VMEM((D_MODEL, N_IN), _BF16),
                        pltpu.SemaphoreType.DMA((WEIGHT_STAGES,))],
        compiler_params=pltpu.CompilerParams(dimension_semantics=("arbitrary",),
                                             vmem_limit_bytes=VMEM_LIMIT_BYTES),
        name="inproj_conv",
    )(x2d, norm_g, w_in.reshape(D_MODEL // tm, tm, N_IN), conv_w, conv_b, w_glu, w_out)


def _complex_powers(ar, ai, exponent, nbits):
    pr = jnp.ones((exponent.shape[0], ar.shape[1]), _F32)
    pi = jnp.zeros_like(pr)
    cr, ci = ar, ai
    for b in range(nbits):
        bit = ((exponent >> b) & 1) == 1
        fr = jnp.where(bit, cr, 1.0)
        fi = jnp.where(bit, ci, 0.0)
        pr, pi = pr * fr - pi * fi, pr * fi + pi * fr
        cr, ci = cr * cr - ci * ci, 2.0 * cr * ci
    return pr, pi


def _re_im_sign():
    lane = lax.broadcasted_iota(jnp.int32, (1, LANES), 1)
    return jnp.where(lane < SSM_STATE, -1.0, 1.0).astype(_F32)


def _s5_prepare_group(g, group, are_ref, aim_ref, mats_ref, kf_ref, ws_ref, wo_ref, a1_ref, a2_ref):
    t, th, jrows, h = S5_CHUNK, S5_TH, S5_POW_ROWS, SSM_GROUP
    sgn = _re_im_sign()
    nbits = max(1, (jrows - 1).bit_length())
    j = lax.broadcasted_iota(jnp.int32, (jrows, 1), 0)
    diag = (lax.broadcasted_iota(jnp.int32, (h, th), 0) == lax.broadcasted_iota(jnp.int32, (h, th), 1))
    a_re, a_im = are_ref[group], aim_ref[group]
    lr = jnp.concatenate([a_re, a_re], axis=1)
    li = jnp.concatenate([a_im, a_im], axis=1)
    ldt = jnp.broadcast_to(mats_ref[g, 0:1, 4 * LANES + h:4 * LANES + h + 1], (1, LANES))
    d_row = jnp.concatenate([mats_ref[g, 0:1, 4 * LANES:4 * LANES + h], jnp.zeros((1, th - h), _F32)],
                            axis=1)
    bt1, bt2, ct1, ct2 = (mats_ref[g, :, k * LANES:(k + 1) * LANES] for k in range(4))
    dt = jnp.exp(ldt)
    mag = jnp.exp(lr * dt)
    ar, ai = mag * jnp.cos(li * dt), mag * jnp.sin(li * dt)
    nr, ni = ar - 1.0, ai
    den = lr * lr + li * li
    qr = (nr * lr + ni * li) / den
    qi = (ni * lr - nr * li) / den
    b1 = qr * bt1 + (qi * sgn) * bt2
    b2 = (qr * sgn) * bt2 - qi * bt1
    c1 = ct1 * (-sgn)
    c2 = -ct2
    pr, pi = _complex_powers(ar, ai, j, nbits)
    rr, ri = _complex_powers(ar, ai, jnp.maximum(t - 1 - j, 0), nbits)
    cps = (pr[:, None, :] * c1[None] + pi[:, None, :] * c2[None]).reshape(jrows * h, LANES)
    cps = cps.astype(_BF16)
    b_hi = b1.astype(_BF16)
    b_lo = (b1 - b_hi.astype(_F32)).astype(_BF16)
    kf_ref[g] = _dot_nt(b_hi, cps[:th]) + _dot_nt(b_lo, cps[:th]) + jnp.where(diag, d_row, 0.0)
    wo_ref[g] = cps[h:h + th]
    ws_ref[g] = (rr[:t, None, :] * b1[None] + ri[:t, None, :] * b2[None]).reshape(th, LANES).astype(_BF16)
    a1_ref[g] = pr[t:t + 1]
    a2_ref[g] = pi[t:t + 1]


def _s5_kernel(ug_ref, are_ref, aim_ref, mats_ref, yg_ref,
               kf_ref, ws_ref, wo_ref, a1_ref, a2_ref,
               mt_ref, xg_ref, yi_ref, sl_ref, slsw_ref, sin_ref, *, chunks_per_seq):
    nb = ug_ref.shape[0]
    t, th, h = S5_CHUNK, S5_TH, SSM_GROUP
    t_hi_n = t // SUBLANES
    n_chunks = ug_ref.shape[1] // t_hi_n
    n_seq = n_chunks // chunks_per_seq
    pitch = S5_SCAN_PITCH
    assert n_seq * nb <= pitch

    for g in range(nb):
        _s5_prepare_group(g, pl.program_id(0) * nb + g, are_ref, aim_ref, mats_ref,
                          kf_ref, ws_ref, wo_ref, a1_ref, a2_ref)
        xg = jnp.concatenate([ug_ref[g, pl.ds(k, n_chunks, stride=t_hi_n), :]
                              for k in range(t_hi_n)], axis=1).astype(_BF16)
        xg_ref[g] = xg
        s = _dot(xg, ws_ref[g])
        s_sw = pltpu.roll(s, SSM_STATE, axis=1)
        for b in range(n_seq):
            rows = slice(b * chunks_per_seq, (b + 1) * chunks_per_seq)
            sl_ref[pl.ds(b * nb + g, chunks_per_seq, stride=pitch), :] = s[rows]
            slsw_ref[pl.ds(b * nb + g, chunks_per_seq, stride=pitch), :] = s_sw[rows]

    a1 = jnp.concatenate([a1_ref[g] for g in range(nb)] * n_seq, axis=0)
    a2 = jnp.concatenate([a2_ref[g] for g in range(nb)] * n_seq, axis=0) * _re_im_sign()
    n_rows = n_seq * nb

    def times(a, z):
        return a[0] * z[0] + a[1] * z[1], a[0] * z[1] - a[1] * z[0]

    def plus(z, w):
        return z[0] + w[0], z[1] + w[1]

    def local(c):
        return (sl_ref[c * pitch:c * pitch + n_rows, :], slsw_ref[c * pitch:c * pitch + n_rows, :])

    radix = S5_SCAN_RADIX
    powers = [(a1, a2)]
    for _ in range(radix - 1):
        pr, pi_signed = powers[-1]
        sgn = _re_im_sign()
        ar, ai, qr, qi = a1, a2 * sgn, pr, pi_signed * sgn
        powers.append((qr * ar - qi * ai, (qr * ai + qi * ar) * sgn))
    s = (jnp.zeros((n_rows, LANES), _F32), jnp.zeros((n_rows, LANES), _F32))
    for c0 in range(0, chunks_per_seq, radix):
        partial = None
        for j in range(radix):
            if j == 0:
                entering = s[0]
            else:
                entering = powers[j - 1][0] * s[0] + powers[j - 1][1] * s[1] + partial[0]
            sin_ref[(c0 + j) * pitch:(c0 + j) * pitch + n_rows, :] = entering
            partial = local(c0 + j) if partial is None else plus(times(powers[0], partial), local(c0 + j))
        s = plus(times(powers[radix - 1], s), partial)

    for g in range(nb):
        zk = jnp.concatenate([jnp.zeros((h, th), _F32), kf_ref[g]], axis=1)
        for k in range(t):
            mt_ref[g, k * h:(k + 1) * h, :] = zk[:, th - k * h:2 * th - k * h].astype(_BF16)
        yi_ref[g] = _dot(xg_ref[g], mt_ref[g])

    for g in range(nb):
        s_in = jnp.concatenate([sin_ref[pl.ds(b * nb + g, chunks_per_seq, stride=pitch), :]
                                for b in range(n_seq)], axis=0).astype(_BF16)
        y = yi_ref[g] + _dot_nt(s_in, wo_ref[g])
        for k in range(t_hi_n):
            yg_ref[g, pl.ds(k, n_chunks, stride=t_hi_n), :] = y[:, k * LANES:(k + 1) * LANES]


def _s5_core(ug, a_re, a_im, mats, chunks_per_seq):
    _, n_rows, _ = ug.shape
    h, th, nb = SSM_GROUP, S5_TH, GROUPS_PER_BLOCK
    group_rows = pl.BlockSpec((nb, n_rows, LANES), lambda i: (i, 0, 0))
    return pl.pallas_call(
        functools.partial(_s5_kernel, chunks_per_seq=chunks_per_seq),
        grid=(SSM_GROUPS // nb,),
        in_specs=[group_rows]
        + [pl.BlockSpec(p.shape, lambda i: (0, 0, 0)) for p in (a_re, a_im)]
        + [pl.BlockSpec((nb, h, mats.shape[-1]), lambda i: (i, 0, 0))],
        out_specs=group_rows,
        out_shape=jax.ShapeDtypeStruct(ug.shape, _F32),
        scratch_shapes=[pltpu.VMEM((nb, h, th), _F32),
                        pltpu.VMEM((nb, th, LANES), _BF16),
                        pltpu.VMEM((nb, th, LANES), _BF16),
                        pltpu.VMEM((nb, 1, LANES), _F32),
                        pltpu.VMEM((nb, 1, LANES), _F32),
                        pltpu.VMEM((nb, th, th), _BF16),
                        pltpu.VMEM((nb, n_rows * SUBLANES // S5_CHUNK, th), _BF16),
                        pltpu.VMEM((nb, n_rows * SUBLANES // S5_CHUNK, th), _F32)]
        + [pltpu.VMEM((chunks_per_seq * S5_SCAN_PITCH, LANES), _F32)] * 3,
        compiler_params=pltpu.CompilerParams(dimension_semantics=("arbitrary",),
                                             vmem_limit_bytes=VMEM_LIMIT_BYTES),
        name="s5_core",
    )(ug, a_re, a_im, mats)


def _out_kernel(yc_ref, yg_ref, zs_ref, x_ref, wg_ref, bg_ref, wo_ref, gp_ref, o_ref, slab_ref, ys_ref):
    tm = x_ref.shape[0]
    nb, sub = GROUPS_PER_BLOCK, SUBLANES
    rp = tm // OUT_ROW_PARTS
    slabs_per_block = COL_BLOCK // LANES

    def activation(part):
        rows = slice(part * rp, (part + 1) * rp)
        grows = slice(part * rp // sub, (part + 1) * rp // sub)
        acc = {"pre": bg_ref[...], "y": []}

        def block(blk):
            gvs = range(blk * slabs_per_block, (blk + 1) * slabs_per_block)
            for gv in gvs:
                tok = _lane_block_transpose([yg_ref[gv * nb + gl, grows, :] for gl in range(nb)])
                for tl in range(sub):
                    slab_ref[gv, pl.ds(part * rp + tl, rp // sub, stride=sub), :] = tok[tl]
            y = jax.nn.gelu(jnp.concatenate([slab_ref[gv, rows, :] for gv in gvs], axis=1))
            acc["y"].append(y)
            acc["pre"] = acc["pre"] + _dot(y.astype(_BF16), wg_ref[blk * COL_BLOCK:(blk + 1) * COL_BLOCK, :])

        def gate():
            y = jnp.concatenate(acc["y"], axis=1)
            ys_ref[part] = (y * _sigmoid(acc["pre"]) * zs_ref[rows, :].astype(_F32)).astype(_BF16)

        return [functools.partial(block, blk) for blk in range(D_SSM // COL_BLOCK)] + [gate]

    def conv_projection(part):
        rows = slice(part * rp, (part + 1) * rp)

        def cols(c):
            sl = slice(c * OUT_COL_BLOCK, (c + 1) * OUT_COL_BLOCK)
            o_ref[rows, sl] = _dot(yc_ref[rows, :], wo_ref[:D_CONV, sl])

        return [functools.partial(cols, c) for c in range(D_MODEL // OUT_COL_BLOCK)]

    def ssm_projection(part):
        rows = slice(part * rp, (part + 1) * rp)

        def cols(c):
            sl = slice(c * OUT_COL_BLOCK, (c + 1) * OUT_COL_BLOCK)
            o_ref[rows, sl] += _dot(ys_ref[part], wo_ref[D_CONV:, sl])

        return [functools.partial(cols, c) for c in range(D_MODEL // OUT_COL_BLOCK)]

    def norm_residual(part):
        rows = slice(part * rp, (part + 1) * rp)
        o = o_ref[rows, :]
        ms = jnp.mean(o * o, axis=-1, keepdims=True)
        o_ref[rows, :] = x_ref[rows, :] + o * lax.rsqrt(ms + EPS) * gp_ref[...]

    for part in range(OUT_ROW_PARTS + 1):
        mxu, vpu = [], []
        if part > 0:
            mxu += ssm_projection(part - 1)
        if part > 1:
            vpu.append(functools.partial(norm_residual, part - 2))
        if part < OUT_ROW_PARTS:
            mxu += conv_projection(part)
            vpu += activation(part)
        _interleave(mxu, vpu)
    norm_residual(OUT_ROW_PARTS - 1)


def _out_proj(yc, yg, zs, x2d, w_glu_bf16, b_glu, w_out_bf16, norm_g):
    n_tok = x2d.shape[0]
    tm = TOKEN_TILE
    const = lambda i: (0, 0)
    tile = lambda w: pl.BlockSpec((tm, w), lambda i: (i, 0))
    return pl.pallas_call(
        _out_kernel,
        grid=(n_tok // tm,),
        in_specs=[
            tile(D_CONV),
            pl.BlockSpec((SSM_GROUPS, tm // SUBLANES, LANES), lambda i: (0, i, 0)),
            tile(D_SSM), tile(D_MODEL),
            pl.BlockSpec((D_SSM, D_SSM), const, pipeline_mode=pl.Buffered(1)),
            pl.BlockSpec((1, D_SSM), const),
            pl.BlockSpec((D_CONV + D_SSM, D_MODEL), const, pipeline_mode=pl.Buffered(1)),
            pl.BlockSpec((1, D_MODEL), const),
        ],
        out_specs=tile(D_MODEL),
        out_shape=jax.ShapeDtypeStruct((n_tok, D_MODEL), _F32),
        scratch_shapes=[pltpu.VMEM((D_SSM // LANES, tm, LANES), _F32),
                        pltpu.VMEM((OUT_ROW_PARTS, tm // OUT_ROW_PARTS, D_SSM), _BF16)],
        compiler_params=pltpu.CompilerParams(dimension_semantics=("arbitrary",),
                                             vmem_limit_bytes=VMEM_LIMIT_BYTES),
        name="glu_outproj",
    )(yc, yg, zs, x2d, w_glu_bf16, b_glu, w_out_bf16, norm_g)


def kernel(x, norm_pre_g, w_in, conv_w, conv_b, ssm_a_re, ssm_a_im, ssm_log_dt, ssm_b_re, ssm_b_im,
           ssm_c_re, ssm_c_im, ssm_d, w_glu, b_glu, w_out, norm_post_g):
    bsz, seq_len, _ = x.shape
    n_tok = bsz * seq_len
    g, p, h, t = SSM_GROUPS, SSM_STATE, SSM_GROUP, S5_CHUNK
    assert seq_len % TOKEN_TILE == 0 and TOKEN_TILE % t == 0
    x2d = x.reshape(n_tok, D_MODEL)

    yc, ug, zs, w_glu_bf16, w_out_bf16 = _inproj(
        x2d, norm_pre_g.reshape(1, -1), w_in, conv_w, conv_b.reshape(1, -1), w_glu, w_out, seq_len)

    brt, bit = ssm_b_re.transpose(0, 2, 1), ssm_b_im.transpose(0, 2, 1)
    tail = jnp.concatenate([ssm_d.reshape(g, h), ssm_log_dt.reshape(g, 1),
                            jnp.zeros((g, LANES - h - 1), _F32)], axis=-1)
    mats = jnp.concatenate([brt, bit, bit, brt, ssm_c_re, ssm_c_im, ssm_c_im, ssm_c_re,
                            jnp.broadcast_to(tail[:, None, :], (g, h, LANES))], axis=-1)
    yg = _s5_core(ug, ssm_a_re.reshape(g, 1, p), ssm_a_im.reshape(g, 1, p), mats, seq_len // t)

    out = _out_proj(yc, yg, zs, x2d, w_glu_bf16, b_glu.reshape(1, -1), w_out_bf16,
                    norm_post_g.reshape(1, -1))
    return out.reshape(bsz, seq_len, D_MODEL)
```

```python
import functools

import jax
import jax.numpy as jnp
from jax import lax
from jax.experimental import pallas as pl
from jax.experimental.pallas import tpu as pltpu

D_MODEL = 2048
D_CONV = 1024
D_SSM = 1024
SSM_GROUP = 16
SSM_GROUPS = 64
SSM_STATE = 64
N_IN = 4 * D_CONV + 2 * D_SSM
EPS = 1e-6

S5_CHUNK = 16
S5_TH = S5_CHUNK * SSM_GROUP
S5_POW_ROWS = S5_CHUNK + 8
S5_SCAN_PITCH = 24
S5_SCAN_RADIX = 4
LANES = 128
SUBLANES = 8
GROUPS_PER_BLOCK = LANES // SSM_GROUP
TOKEN_TILE = 512
COL_BLOCK = 256
WEIGHT_STAGES = 6
OUT_ROW_PARTS = 2
OUT_COL_BLOCK = 512
VMEM_LIMIT_BYTES = 56 * 1024 * 1024

_F32 = jnp.float32
_BF16 = jnp.bfloat16


def _dot(a, b):
    return jnp.dot(a, b, preferred_element_type=_F32)


def _dot_nt(a, b):
    return lax.dot_general(a, b, (((1,), (1,)), ((), ())), preferred_element_type=_F32)


def _sigmoid(z):
    return jax.nn.sigmoid(z)


def _interleave(mxu_items, vpu_items):
    n = max(len(mxu_items), 1)
    done = 0
    for i, item in enumerate(mxu_items):
        item()
        upto = -(-len(vpu_items) * (i + 1) // n)
        for piece in vpu_items[done:upto]:
            piece()
        done = upto
    for piece in vpu_items[done:]:
        piece()


def _load_weight_as_bf16(w_hbm, w_ref, stages, sem, cols):
    a, r, n = w_hbm.shape
    slots, n_chunks = len(stages), n // cols

    def chunk(c):
        return pltpu.make_async_copy(w_hbm.at[:, :, pl.ds(c * cols, cols)], stages[c % slots],
                                     sem.at[c % slots])

    for c in range(min(slots - 1, n_chunks)):
        chunk(c).start()
    for c in range(n_chunks):
        if c + slots - 1 < n_chunks:
            chunk(c + slots - 1).start()
        chunk(c).wait()
        for k in range(a):
            w_ref[k * r:(k + 1) * r, c * cols:(c + 1) * cols] = stages[c % slots][k].astype(_BF16)


def _lane_block_transpose(xs):
    nb = len(xs)
    diag = _diagonal_gather(xs)
    moved = [w if d == 0 else pltpu.roll(w, d * SSM_GROUP, axis=1) for d, w in enumerate(diag)]
    back = _diagonal_gather(moved)
    return [back[(-b) % nb] for b in range(nb)]


def _diagonal_gather(arrs):
    assert len(arrs) == 8
    c = lax.broadcasted_iota(jnp.int32, (1, LANES), 1) // SSM_GROUP
    c0, c1, c2 = c & 1, (c >> 1) & 1, (c >> 2) & 1
    s1 = {(k0, hi): jnp.where((c0 ^ k0) == 1, arrs[2 * hi + 1], arrs[2 * hi])
          for k0 in range(2) for hi in range(4)}
    s2 = {}
    for k0 in range(2):
        for k1 in range(2):
            e1 = c1 ^ k1 ^ (c0 & k0)
            for top in range(2):
                s2[(k0, k1, top)] = jnp.where(e1 == 1, s1[(k0, 2 * top + 1)], s1[(k0, 2 * top)])
    out = []
    for k in range(8):
        k0, k1, k2 = k & 1, (k >> 1) & 1, (k >> 2) & 1
        carry0 = c0 & k0
        carry1 = (c1 & k1) | (c1 & carry0) | (k1 & carry0)
        e2 = c2 ^ k2 ^ carry1
        out.append(jnp.where(e2 == 1, s2[(k0, k1, 1)], s2[(k0, k1, 0)]))
    return out


def _inproj_kernel(x_ref, g_ref, w_hbm, cw_ref, cb_ref, wglu_ref, wout_ref,
                   yc_ref, ug_ref, zs_ref, wglu_bf_ref, wout_bf_ref,
                   carry_ref, slab_ref, raw_ref, w_ref, sem, *, tiles_per_seq):
    tm = x_ref.shape[0]
    nb, sub = GROUPS_PER_BLOCK, SUBLANES
    slabs_per_block = COL_BLOCK // LANES
    n_blocks = D_CONV // COL_BLOCK
    bases = (4 * D_CONV, 0, D_CONV, 2 * D_CONV, 3 * D_CONV, 4 * D_CONV + D_SSM)

    @pl.when(pl.program_id(0) == 0)
    def _():
        n_a = w_hbm.shape[0]
        stages = [raw_ref.at[slot, pl.ds(0, n_a), :, pl.ds(half * LANES, LANES)]
                  for slot in range(raw_ref.shape[0]) for half in range(COL_BLOCK // LANES)]
        stages += [slab_ref.at[pl.ds(i * n_a, n_a)] for i in range(slab_ref.shape[0] // n_a)]
        assert len(stages) == WEIGHT_STAGES
        _load_weight_as_bf16(w_hbm, w_ref, stages, sem, LANES)

    @pl.when(pl.program_id(0) % tiles_per_seq == 0)
    def _():
        carry_ref[...] = jnp.zeros_like(carry_ref)

    x = x_ref[...]
    ms = jnp.mean(x * x, axis=-1, keepdims=True)
    h = (x * lax.rsqrt(ms + EPS) * g_ref[...]).astype(_BF16)

    wglu_bf_ref[...] = wglu_ref[...].astype(_BF16)
    wout_bf_ref[...] = wout_ref[...].astype(_BF16)

    def project_one(blk, j):
        lo = bases[j] + blk * COL_BLOCK
        res = _dot(h, w_ref[:, lo:lo + COL_BLOCK])
        if j == 0:
            for k in range(slabs_per_block):
                slab_ref[blk * slabs_per_block + k] = res[:, k * LANES:(k + 1) * LANES]
        else:
            raw_ref[blk % 2, j - 1] = res

    def project(blk):
        return [functools.partial(project_one, blk, j) for j in range(len(bases))]

    def finish_gate(blk):
        sl = slice(blk * COL_BLOCK, (blk + 1) * COL_BLOCK)
        z_ssm = raw_ref[blk % 2, 4]
        zs_ref[:, sl] = (z_ssm * _sigmoid(z_ssm)).astype(_BF16)

    def finish_u(gv):
        rows = [slab_ref[gv, pl.ds(tl, tm // sub, stride=sub), :] for tl in range(sub)]
        for gl, o in enumerate(_lane_block_transpose(rows)):
            ug_ref[gv * nb + gl] = o

    def finish_conv(blk, half):
        sl = slice(blk * COL_BLOCK + half * LANES, blk * COL_BLOCK + (half + 1) * LANES)
        hl = slice(half * LANES, (half + 1) * LANES)
        b_gate, c_gate, v, z = (raw_ref[blk % 2, j, :, hl] for j in range(4))
        cv = c_gate * v
        ext = jnp.concatenate([carry_ref[:, sl], cv], axis=0)
        conv = (cb_ref[:, sl] + cw_ref[2:3, sl] * cv + cw_ref[1:2, sl] * ext[sub - 1:sub - 1 + tm]
                + cw_ref[0:1, sl] * ext[sub - 2:sub - 2 + tm])
        carry_ref[:, sl] = cv[tm - sub:]
        yc_ref[:, sl] = (b_gate * conv * (z * _sigmoid(z))).astype(_BF16)

    def finish(blk):
        return ([functools.partial(finish_gate, blk)]
                + [functools.partial(finish_u, blk * slabs_per_block + k) for k in range(slabs_per_block)]
                + [functools.partial(finish_conv, blk, half) for half in range(COL_BLOCK // LANES)])

    _interleave(project(0), [])
    for blk in range(1, n_blocks):
        _interleave(project(blk), finish(blk - 1))
    _interleave([], finish(n_blocks - 1))


def _inproj(x2d, norm_g, w_in, conv_w, conv_b, w_glu, w_out, seq_len):
    n_tok = x2d.shape[0]
    tm = TOKEN_TILE
    steps = n_tok // tm
    const = lambda s: (0, 0)
    tile = lambda w: pl.BlockSpec((tm, w), lambda s: (s, 0))
    row_block = lambda w: pl.BlockSpec((w.shape[0] // steps, w.shape[1]), lambda s: (s, 0))
    out_sds = jax.ShapeDtypeStruct((n_tok, D_CONV), _BF16)
    return pl.pallas_call(
        functools.partial(_inproj_kernel, tiles_per_seq=seq_len // tm),
        grid=(steps,),
        in_specs=[
            tile(D_MODEL),
            pl.BlockSpec((1, D_MODEL), const),
            pl.BlockSpec(memory_space=pl.ANY),
            pl.BlockSpec((3, D_CONV), const),
            pl.BlockSpec((1, D_CONV), const),
            row_block(w_glu), row_block(w_out),
        ],
        out_specs=[tile(D_CONV),
                   pl.BlockSpec((SSM_GROUPS, tm // SUBLANES, LANES), lambda s: (0, s, 0)),
                   tile(D_SSM),
                   row_block(w_glu), row_block(w_out)],
        out_shape=[out_sds,
                   jax.ShapeDtypeStruct((SSM_GROUPS, n_tok // SUBLANES, LANES), _F32),
                   out_sds,
                   jax.ShapeDtypeStruct(w_glu.shape, _BF16),
                   jax.ShapeDtypeStruct(w_out.shape, _BF16)],
        scratch_shapes=[pltpu.VMEM((SUBLANES, D_CONV), _F32),
                        pltpu.VMEM((D_SSM // LANES, tm, LANES), _F32),
                        pltpu.VMEM((2, 5, tm, COL_BLOCK), _F32),
                        pltpu.VMEM((D_MODEL, N_IN), _BF16),
                        pltpu.SemaphoreType.DMA((WEIGHT_STAGES,))],
        compiler_params=pltpu.CompilerParams(dimension_semantics=("arbitrary",),
                                             vmem_limit_bytes=VMEM_LIMIT_BYTES),
        name="inproj_conv",
    )(x2d, norm_g, w_in.reshape(D_MODEL // tm, tm, N_IN), conv_w, conv_b, w_glu, w_out)


def _complex_powers(ar, ai, exponent, nbits):
    pr = jnp.ones((exponent.shape[0], ar.shape[1]), _F32)
    pi = jnp.zeros_like(pr)
    cr, ci = ar, ai
    for b in range(nbits):
        bit = ((exponent >> b) & 1) == 1
        fr = jnp.where(bit, cr, 1.0)
        fi = jnp.where(bit, ci, 0.0)
        pr, pi = pr * fr - pi * fi, pr * fi + pi * fr
        cr, ci = cr * cr - ci * ci, 2.0 * cr * ci
    return pr, pi


def _re_im_sign():
    lane = lax.broadcasted_iota(jnp.int32, (1, LANES), 1)
    return jnp.where(lane < SSM_STATE, -1.0, 1.0).astype(_F32)


def _s5_prepare_group(g, group, are_ref, aim_ref, mats_ref, kf_ref, ws_ref, wo_ref, a1_ref, a2_ref):
    t, th, jrows, h = S5_CHUNK, S5_TH, S5_POW_ROWS, SSM_GROUP
    sgn = _re_im_sign()
    nbits = max(1, (jrows - 1).bit_length())
    j = lax.broadcasted_iota(jnp.int32, (jrows, 1), 0)
    diag = (lax.broadcasted_iota(jnp.int32, (h, th), 0) == lax.broadcasted_iota(jnp.int32, (h, th), 1))
    a_re, a_im = are_ref[group], aim_ref[group]
    lr = jnp.concatenate([a_re, a_re], axis=1)
    li = jnp.concatenate([a_im, a_im], axis=1)
    ldt = jnp.broadcast_to(mats_ref[g, 0:1, 4 * LANES + h:4 * LANES + h + 1], (1, LANES))
    d_row = jnp.concatenate([mats_ref[g, 0:1, 4 * LANES:4 * LANES + h], jnp.zeros((1, th - h), _F32)],
                            axis=1)
    bt1, bt2, ct1, ct2 = (mats_ref[g, :, k * LANES:(k + 1) * LANES] for k in range(4))
    dt = jnp.exp(ldt)
    mag = jnp.exp(lr * dt)
    ar, ai = mag * jnp.cos(li * dt), mag * jnp.sin(li * dt)
    nr, ni = ar - 1.0, ai
    den = lr * lr + li * li
    qr = (nr * lr + ni * li) / den
    qi = (ni * lr - nr * li) / den
    b1 = qr * bt1 + (qi * sgn) * bt2
    b2 = (qr * sgn) * bt2 - qi * bt1
    c1 = ct1 * (-sgn)
    c2 = -ct2
    pr, pi = _complex_powers(ar, ai, j, nbits)
    rr, ri = _complex_powers(ar, ai, jnp.maximum(t - 1 - j, 0), nbits)
    cps = (pr[:, None, :] * c1[None] + pi[:, None, :] * c2[None]).reshape(jrows * h, LANES)
    cps = cps.astype(_BF16)
    b_hi = b1.astype(_BF16)
    b_lo = (b1 - b_hi.astype(_F32)).astype(_BF16)
    kf_ref[g] = _dot_nt(b_hi, cps[:th]) + _dot_nt(b_lo, cps[:th]) + jnp.where(diag, d_row, 0.0)
    wo_ref[g] = cps[h:h + th]
    ws_ref[g] = (rr[:t, None, :] * b1[None] + ri[:t, None, :] * b2[None]).reshape(th, LANES).astype(_BF16)
    a1_ref[g] = pr[t:t + 1]
    a2_ref[g] = pi[t:t + 1]


def _s5_kernel(ug_ref, are_ref, aim_ref, mats_ref, yg_ref,
               kf_ref, ws_ref, wo_ref, a1_ref, a2_ref,
               mt_ref, xg_ref, yi_ref, sl_ref, slsw_ref, sin_ref, *, chunks_per_seq):
    nb = ug_ref.shape[0]
    t, th, h = S5_CHUNK, S5_TH, SSM_GROUP
    t_hi_n = t // SUBLANES
    n_chunks = ug_ref.shape[1] // t_hi_n
    n_seq = n_chunks // chunks_per_seq
    pitch = S5_SCAN_PITCH
    assert n_seq * nb <= pitch

    for g in range(nb):
        _s5_prepare_group(g, pl.program_id(0) * nb + g, are_ref, aim_ref, mats_ref,
                          kf_ref, ws_ref, wo_ref, a1_ref, a2_ref)
        xg = jnp.concatenate([ug_ref[g, pl.ds(k, n_chunks, stride=t_hi_n), :]
                              for k in range(t_hi_n)], axis=1).astype(_BF16)
        xg_ref[g] = xg
        s = _dot(xg, ws_ref[g])
        s_sw = pltpu.roll(s, SSM_STATE, axis=1)
        for b in range(n_seq):
            rows = slice(b * chunks_per_seq, (b + 1) * chunks_per_seq)
            sl_ref[pl.ds(b * nb + g, chunks_per_seq, stride=pitch), :] = s[rows]
            slsw_ref[pl.ds(b * nb + g, chunks_per_seq, stride=pitch), :] = s_sw[rows]

    a1 = jnp.concatenate([a1_ref[g] for g in range(nb)] * n_seq, axis=0)
    a2 = jnp.concatenate([a2_ref[g] for g in range(nb)] * n_seq, axis=0) * _re_im_sign()
    n_rows = n_seq * nb

    def times(a, z):
        return a[0] * z[0] + a[1] * z[1], a[0] * z[1] - a[1] * z[0]

    def plus(z, w):
        return z[0] + w[0], z[1] + w[1]

    def local(c):
        return (sl_ref[c * pitch:c * pitch + n_rows, :], slsw_ref[c * pitch:c * pitch + n_rows, :])

    radix = S5_SCAN_RADIX
    powers = [(a1, a2)]
    for _ in range(radix - 1):
        pr, pi_signed = powers[-1]
        sgn = _re_im_sign()
        ar, ai, qr, qi = a1, a2 * sgn, pr, pi_signed * sgn
        powers.append((qr * ar - qi * ai, (qr * ai + qi * ar) * sgn))
    s = (jnp.zeros((n_rows, LANES), _F32), jnp.zeros((n_rows, LANES), _F32))
    for c0 in range(0, chunks_per_seq, radix):
        partial = None
        for j in range(radix):
            if j == 0:
                entering = s[0]
            else:
                entering = powers[j - 1][0] * s[0] + powers[j - 1][1] * s[1] + partial[0]
            sin_ref[(c0 + j) * pitch:(c0 + j) * pitch + n_rows, :] = entering
            partial = local(c0 + j) if partial is None else plus(times(powers[0], partial), local(c0 + j))
        s = plus(times(powers[radix - 1], s), partial)

    for g in range(nb):
        zk = jnp.concatenate([jnp.zeros((h, th), _F32), kf_ref[g]], axis=1)
        for k in range(t):
            mt_ref[g, k * h:(k + 1) * h, :] = zk[:, th - k * h:2 * th - k * h].astype(_BF16)
        yi_ref[g] = _dot(xg_ref[g], mt_ref[g])

    for g in range(nb):
        s_in = jnp.concatenate([sin_ref[pl.ds(b * nb + g, chunks_per_seq, stride=pitch), :]
                                for b in range(n_seq)], axis=0).astype(_BF16)
        y = yi_ref[g] + _dot_nt(s_in, wo_ref[g])
        for k in range(t_hi_n):
            yg_ref[g, pl.ds(k, n_chunks, stride=t_hi_n), :] = y[:, k * LANES:(k + 1) * LANES]


def _s5_core(ug, a_re, a_im, mats, chunks_per_seq):
    _, n_rows, _ = ug.shape
    h, th, nb = SSM_GROUP, S5_TH, GROUPS_PER_BLOCK
    group_rows = pl.BlockSpec((nb, n_rows, LANES), lambda i: (i, 0, 0))
    return pl.pallas_call(
        functools.partial(_s5_kernel, chunks_per_seq=chunks_per_seq),
        grid=(SSM_GROUPS // nb,),
        in_specs=[group_rows]
        + [pl.BlockSpec(p.shape, lambda i: (0, 0, 0)) for p in (a_re, a_im)]
        + [pl.BlockSpec((nb, h, mats.shape[-1]), lambda i: (i, 0, 0))],
        out_specs=group_rows,
        out_shape=jax.ShapeDtypeStruct(ug.shape, _F32),
        scratch_shapes=[pltpu.VMEM((nb, h, th), _F32),
                        pltpu.VMEM((nb, th, LANES), _BF16),
                        pltpu.VMEM((nb, th, LANES), _BF16),
                        pltpu.VMEM((nb, 1, LANES), _F32),
                        pltpu.VMEM((nb, 1, LANES), _F32),
                        pltpu.VMEM((nb, th, th), _BF16),
                        pltpu.VMEM((nb, n_rows * SUBLANES // S5_CHUNK, th), _BF16),
                        pltpu.VMEM((nb, n_rows * SUBLANES // S5_CHUNK, th), _F32)]
        + [pltpu.VMEM((chunks_per_seq * S5_SCAN_PITCH, LANES), _F32)] * 3,
        compiler_params=pltpu.CompilerParams(dimension_semantics=("arbitrary",),
                                             vmem_limit_bytes=VMEM_LIMIT_BYTES),
        name="s5_core",
    )(ug, a_re, a_im, mats)


def _out_kernel(yc_ref, yg_ref, zs_ref, x_ref, wg_ref, bg_ref, wo_ref, gp_ref, o_ref, slab_ref, ys_ref):
    tm = x_ref.shape[0]
    nb, sub = GROUPS_PER_BLOCK, SUBLANES
    rp = tm // OUT_ROW_PARTS
    slabs_per_block = COL_BLOCK // LANES

    def activation(part):
        rows = slice(part * rp, (part + 1) * rp)
        grows = slice(part * rp // sub, (part + 1) * rp // sub)
        acc = {"pre": bg_ref[...], "y": []}

        def block(blk):
            gvs = range(blk * slabs_per_block, (blk + 1) * slabs_per_block)
            for gv in gvs:
                tok = _lane_block_transpose([yg_ref[gv * nb + gl, grows, :] for gl in range(nb)])
                for tl in range(sub):
                    slab_ref[gv, pl.ds(part * rp + tl, rp // sub, stride=sub), :] = tok[tl]
            y = jax.nn.gelu(jnp.concatenate([slab_ref[gv, rows, :] for gv in gvs], axis=1))
            acc["y"].append(y)
            acc["pre"] = acc["pre"] + _dot(y.astype(_BF16), wg_ref[blk * COL_BLOCK:(blk + 1) * COL_BLOCK, :])

        def gate():
            y = jnp.concatenate(acc["y"], axis=1)
            ys_ref[part] = (y * _sigmoid(acc["pre"]) * zs_ref[rows, :].astype(_F32)).astype(_BF16)

        return [functools.partial(block, blk) for blk in range(D_SSM // COL_BLOCK)] + [gate]

    def conv_projection(part):
        rows = slice(part * rp, (part + 1) * rp)

        def cols(c):
            sl = slice(c * OUT_COL_BLOCK, (c + 1) * OUT_COL_BLOCK)
            o_ref[rows, sl] = _dot(yc_ref[rows, :], wo_ref[:D_CONV, sl])

        return [functools.partial(cols, c) for c in range(D_MODEL // OUT_COL_BLOCK)]

    def ssm_projection(part):
        rows = slice(part * rp, (part + 1) * rp)

        def cols(c):
            sl = slice(c * OUT_COL_BLOCK, (c + 1) * OUT_COL_BLOCK)
            o_ref[rows, sl] += _dot(ys_ref[part], wo_ref[D_CONV:, sl])

        return [functools.partial(cols, c) for c in range(D_MODEL // OUT_COL_BLOCK)]

    def norm_residual(part):
        rows = slice(part * rp, (part + 1) * rp)
        o = o_ref[rows, :]
        ms = jnp.mean(o * o, axis=-1, keepdims=True)
        o_ref[rows, :] = x_ref[rows, :] + o * lax.rsqrt(ms + EPS) * gp_ref[...]

    for part in range(OUT_ROW_PARTS + 1):
        mxu, vpu = [], []
        if part > 0:
            mxu += ssm_projection(part - 1)
        if part > 1:
            vpu.append(functools.partial(norm_residual, part - 2))
        if part < OUT_ROW_PARTS:
            mxu += conv_projection(part)
            vpu += activation(part)
        _interleave(mxu, vpu)
    norm_residual(OUT_ROW_PARTS - 1)


def _out_proj(yc, yg, zs, x2d, w_glu_bf16, b_glu, w_out_bf16, norm_g):
    n_tok = x2d.shape[0]
    tm = TOKEN_TILE
    const = lambda i: (0, 0)
    tile = lambda w: pl.BlockSpec((tm, w), lambda i: (i, 0))
    return pl.pallas_call(
        _out_kernel,
        grid=(n_tok // tm,),
        in_specs=[
            tile(D_CONV),
            pl.BlockSpec((SSM_GROUPS, tm // SUBLANES, LANES), lambda i: (0, i, 0)),
            tile(D_SSM), tile(D_MODEL),
            pl.BlockSpec((D_SSM, D_SSM), const, pipeline_mode=pl.Buffered(1)),
            pl.BlockSpec((1, D_SSM), const),
            pl.BlockSpec((D_CONV + D_SSM, D_MODEL), const, pipeline_mode=pl.Buffered(1)),
            pl.BlockSpec((1, D_MODEL), const),
        ],
        out_specs=tile(D_MODEL),
        out_shape=jax.ShapeDtypeStruct((n_tok, D_MODEL), _F32),
        scratch_shapes=[pltpu.VMEM((D_SSM // LANES, tm, LANES), _F32),
                        pltpu.VMEM((OUT_ROW_PARTS, tm // OUT_ROW_PARTS, D_SSM), _BF16)],
        compiler_params=pltpu.CompilerParams(dimension_semantics=("arbitrary",),
                                             vmem_limit_bytes=VMEM_LIMIT_BYTES),
        name="glu_outproj",
    )(yc, yg, zs, x2d, w_glu_bf16, b_glu, w_out_bf16, norm_g)


def kernel(x, norm_pre_g, w_in, conv_w, conv_b, ssm_a_re, ssm_a_im, ssm_log_dt, ssm_b_re, ssm_b_im,
           ssm_c_re, ssm_c_im, ssm_d, w_glu, b_glu, w_out, norm_post_g):
    bsz, seq_len, _ = x.shape
    n_tok = bsz * seq_len
    g, p, h, t = SSM_GROUPS, SSM_STATE, SSM_GROUP, S5_CHUNK
    assert seq_len % TOKEN_TILE == 0 and TOKEN_TILE % t == 0
    x2d = x.reshape(n_tok, D_MODEL)

    yc, ug, zs, w_glu_bf16, w_out_bf16 = _inproj(
        x2d, norm_pre_g.reshape(1, -1), w_in, conv_w, conv_b.reshape(1, -1), w_glu, w_out, seq_len)

    brt, bit = ssm_b_re.transpose(0, 2, 1), ssm_b_im.transpose(0, 2, 1)
    tail = jnp.concatenate([ssm_d.reshape(g, h), ssm_log_dt.reshape(g, 1),
                            jnp.zeros((g, LANES - h - 1), _F32)], axis=-1)
    mats = jnp.concatenate([brt, bit, bit, brt, ssm_c_re, ssm_c_im, ssm_c_im, ssm_c_re,
                            jnp.broadcast_to(tail[:, None, :], (g, h, LANES))], axis=-1)
    yg = _s5_core(ug, ssm_a_re.reshape(g, 1, p), ssm_a_im.reshape(g, 1, p), mats, seq_len // t)

    out = _out_proj(yc, yg, zs, x2d, w_glu_bf16, b_glu.reshape(1, -1), w_out_bf16,
                    norm_post_g.reshape(1, -1))
    return out.reshape(bsz, seq_len, D_MODEL)
```

```python
import functools

import jax
import jax.numpy as jnp
from jax import lax
from jax.experimental import pallas as pl
from jax.experimental.pallas import tpu as pltpu

D_MODEL = 2048
D_CONV = 1024
D_SSM = 1024
SSM_GROUP = 16
SSM_GROUPS = 64
SSM_STATE = 64
N_IN = 4 * D_CONV + 2 * D_SSM
EPS = 1e-6

S5_CHUNK = 16
S5_TH = S5_CHUNK * SSM_GROUP
S5_POW_ROWS = S5_CHUNK + 8
S5_SCAN_PITCH = 24
S5_SCAN_RADIX = 4
LANES = 128
SUBLANES = 8
GROUPS_PER_BLOCK = LANES // SSM_GROUP
TOKEN_TILE = 512
COL_BLOCK = 256
WEIGHT_STAGES = 6
OUT_ROW_PARTS = 2
OUT_COL_BLOCK = 512
VMEM_LIMIT_BYTES = 56 * 1024 * 1024

_F32 = jnp.float32
_BF16 = jnp.bfloat16


def _dot(a, b):
    return jnp.dot(a, b, preferred_element_type=_F32)


def _dot_nt(a, b):
    return lax.dot_general(a, b, (((1,), (1,)), ((), ())), preferred_element_type=_F32)


def _sigmoid(z):
    return jax.nn.sigmoid(z)


def _interleave(mxu_items, vpu_items):
    n = max(len(mxu_items), 1)
    done = 0
    for i, item in enumerate(mxu_items):
        item()
        upto = -(-len(vpu_items) * (i + 1) // n)
        for piece in vpu_items[done:upto]:
            piece()
        done = upto
    for piece in vpu_items[done:]:
        piece()


def _load_weight_as_bf16(w_hbm, w_ref, stages, sem, cols):
    a, r, n = w_hbm.shape
    slots, n_chunks = len(stages), n // cols

    def chunk(c):
        return pltpu.make_async_copy(w_hbm.at[:, :, pl.ds(c * cols, cols)], stages[c % slots],
                                     sem.at[c % slots])

    for c in range(min(slots - 1, n_chunks)):
        chunk(c).start()
    for c in range(n_chunks):
        if c + slots - 1 < n_chunks:
            chunk(c + slots - 1).start()
        chunk(c).wait()
        for k in range(a):
            w_ref[k * r:(k + 1) * r, c * cols:(c + 1) * cols] = stages[c % slots][k].astype(_BF16)


def _lane_block_transpose(xs):
    nb = len(xs)
    diag = _diagonal_gather(xs)
    moved = [w if d == 0 else pltpu.roll(w, d * SSM_GROUP, axis=1) for d, w in enumerate(diag)]
    back = _diagonal_gather(moved)
    return [back[(-b) % nb] for b in range(nb)]


def _diagonal_gather(arrs):
    assert len(arrs) == 8
    c = lax.broadcasted_iota(jnp.int32, (1, LANES), 1) // SSM_GROUP
    c0, c1, c2 = c & 1, (c >> 1) & 1, (c >> 2) & 1
    s1 = {(k0, hi): jnp.where((c0 ^ k0) == 1, arrs[2 * hi + 1], arrs[2 * hi])
          for k0 in range(2) for hi in range(4)}
    s2 = {}
    for k0 in range(2):
        for k1 in range(2):
            e1 = c1 ^ k1 ^ (c0 & k0)
            for top in range(2):
                s2[(k0, k1, top)] = jnp.where(e1 == 1, s1[(k0, 2 * top + 1)], s1[(k0, 2 * top)])
    out = []
    for k in range(8):
        k0, k1, k2 = k & 1, (k >> 1) & 1, (k >> 2) & 1
        carry0 = c0 & k0
        carry1 = (c1 & k1) | (c1 & carry0) | (k1 & carry0)
        e2 = c2 ^ k2 ^ carry1
        out.append(jnp.where(e2 == 1, s2[(k0, k1, 1)], s2[(k0, k1, 0)]))
    return out


def _inproj_kernel(x_ref, g_ref, w_hbm, cw_ref, cb_ref, wglu_ref, wout_ref,
                   yc_ref, ug_ref, zs_ref, wglu_bf_ref, wout_bf_ref,
                   carry_ref, slab_ref, raw_ref, w_ref, sem, *, tiles_per_seq):
    tm = x_ref.shape[0]
    nb, sub = GROUPS_PER_BLOCK, SUBLANES
    slabs_per_block = COL_BLOCK // LANES
    n_blocks = D_CONV // COL_BLOCK
    bases = (4 * D_CONV, 0, D_CONV, 2 * D_CONV, 3 * D_CONV, 4 * D_CONV + D_SSM)

    @pl.when(pl.program_id(0) == 0)
    def _():
        n_a = w_hbm.shape[0]
        stages = [raw_ref.at[slot, pl.ds(0, n_a), :, pl.ds(half * LANES, LANES)]
                  for slot in range(raw_ref.shape[0]) for half in range(COL_BLOCK // LANES)]
        stages += [slab_ref.at[pl.ds(i * n_a, n_a)] for i in range(slab_ref.shape[0] // n_a)]
        assert len(stages) == WEIGHT_STAGES
        _load_weight_as_bf16(w_hbm, w_ref, stages, sem, LANES)

    @pl.when(pl.program_id(0) % tiles_per_seq == 0)
    def _():
        carry_ref[...] = jnp.zeros_like(carry_ref)

    x = x_ref[...]
    ms = jnp.mean(x * x, axis=-1, keepdims=True)
    h = (x * lax.rsqrt(ms + EPS) * g_ref[...]).astype(_BF16)

    wglu_bf_ref[...] = wglu_ref[...].astype(_BF16)
    wout_bf_ref[...] = wout_ref[...].astype(_BF16)

    def project_one(blk, j):
        lo = bases[j] + blk * COL_BLOCK
        res = _dot(h, w_ref[:, lo:lo + COL_BLOCK])
        if j == 0:
            for k in range(slabs_per_block):
                slab_ref[blk * slabs_per_block + k] = res[:, k * LANES:(k + 1) * LANES]
        else:
            raw_ref[blk % 2, j - 1] = res

    def project(blk):
        return [functools.partial(project_one, blk, j) for j in range(len(bases))]

    def finish_gate(blk):
        sl = slice(blk * COL_BLOCK, (blk + 1) * COL_BLOCK)
        z_ssm = raw_ref[blk % 2, 4]
        zs_ref[:, sl] = (z_ssm * _sigmoid(z_ssm)).astype(_BF16)

    def finish_u(gv):
        rows = [slab_ref[gv, pl.ds(tl, tm // sub, stride=sub), :] for tl in range(sub)]
        for gl, o in enumerate(_lane_block_transpose(rows)):
            ug_ref[gv * nb + gl] = o

    def finish_conv(blk, half):
        sl = slice(blk * COL_BLOCK + half * LANES, blk * COL_BLOCK + (half + 1) * LANES)
        hl = slice(half * LANES, (half + 1) * LANES)
        b_gate, c_gate, v, z = (raw_ref[blk % 2, j, :, hl] for j in range(4))
        cv = c_gate * v
        ext = jnp.concatenate([carry_ref[:, sl], cv], axis=0)
        conv = (cb_ref[:, sl] + cw_ref[2:3, sl] * cv + cw_ref[1:2, sl] * ext[sub - 1:sub - 1 + tm]
                + cw_ref[0:1, sl] * ext[sub - 2:sub - 2 + tm])
        carry_ref[:, sl] = cv[tm - sub:]
        yc_ref[:, sl] = (b_gate * conv * (z * _sigmoid(z))).astype(_BF16)

    def finish(blk):
        return ([functools.partial(finish_gate, blk)]
                + [functools.partial(finish_u, blk * slabs_per_block + k) for k in range(slabs_per_block)]
                + [functools.partial(finish_conv, blk, half) for half in range(COL_BLOCK // LANES)])

    _interleave(project(0), [])
    for blk in range(1, n_blocks):
        _interleave(project(blk), finish(blk - 1))
    _interleave([], finish(n_blocks - 1))


def _inproj(x2d, norm_g, w_in, conv_w, conv_b, w_glu, w_out, seq_len):
    n_tok = x2d.shape[0]
    tm = TOKEN_TILE
    steps = n_tok // tm
    const = lambda s: (0, 0)
    tile = lambda w: pl.BlockSpec((tm, w), lambda s: (s, 0))
    row_block = lambda w: pl.BlockSpec((w.shape[0] // steps, w.shape[1]), lambda s: (s, 0))
    out_sds = jax.ShapeDtypeStruct((n_tok, D_CONV), _BF16)
    return pl.pallas_call(
        functools.partial(_inproj_kernel, tiles_per_seq=seq_len // tm),
        grid=(steps,),
        in_specs=[
            tile(D_MODEL),
            pl.BlockSpec((1, D_MODEL), const),
            pl.BlockSpec(memory_space=pl.ANY),
            pl.BlockSpec((3, D_CONV), const),
            pl.BlockSpec((1, D_CONV), const),
            row_block(w_glu), row_block(w_out),
        ],
        out_specs=[tile(D_CONV),
                   pl.BlockSpec((SSM_GROUPS, tm // SUBLANES, LANES), lambda s: (0, s, 0)),
                   tile(D_SSM),
                   row_block(w_glu), row_block(w_out)],
        out_shape=[out_sds,
                   jax.ShapeDtypeStruct((SSM_GROUPS, n_tok // SUBLANES, LANES), _F32),
                   out_sds,
                   jax.ShapeDtypeStruct(w_glu.shape, _BF16),
                   jax.ShapeDtypeStruct(w_out.shape, _BF16)],
        scratch_shapes=[pltpu.VMEM((SUBLANES, D_CONV), _F32),
                        pltpu.VMEM((D_SSM // LANES, tm, LANES), _F32),
                        pltpu.VMEM((2, 5, tm, COL_BLOCK), _F32),
                        pltpu.VMEM((D_MODEL, N_IN), _BF16),
                        pltpu.SemaphoreType.DMA((WEIGHT_STAGES,))],
        compiler_params=pltpu.CompilerParams(dimension_semantics=("arbitrary",),
                                             vmem_limit_bytes=VMEM_LIMIT_BYTES),
        name="inproj_conv",
    )(x2d, norm_g, w_in.reshape(D_MODEL // tm, tm, N_IN), conv_w, conv_b, w_glu, w_out)


def _complex_powers(ar, ai, exponent, nbits):
    pr = jnp.ones((exponent.shape[0], ar.shape[1]), _F32)
    pi = jnp.zeros_like(pr)
    cr, ci = ar, ai
    for b in range(nbits):
        bit = ((exponent >> b) & 1) == 1
        fr = jnp.where(bit, cr, 1.0)
        fi = jnp.where(bit, ci, 0.0)
        pr, pi = pr * fr - pi * fi, pr * fi + pi * fr
        cr, ci = cr * cr - ci * ci, 2.0 * cr * ci
    return pr, pi


def _re_im_sign():
    lane = lax.broadcasted_iota(jnp.int32, (1, LANES), 1)
    return jnp.where(lane < SSM_STATE, -1.0, 1.0).astype(_F32)


def _s5_prepare_group(g, group, are_ref, aim_ref, d_ref, ldt_ref, bre_ref, bim_ref, cre_ref, cim_ref,
                      kf_ref, ws_ref, wo_ref, a1_ref, a2_ref):
    t, th, jrows, h = S5_CHUNK, S5_TH, S5_POW_ROWS, SSM_GROUP
    sgn = _re_im_sign()
    nbits = max(1, (jrows - 1).bit_length())
    j = lax.broadcasted_iota(jnp.int32, (jrows, 1), 0)
    diag = (lax.broadcasted_iota(jnp.int32, (h, th), 0) == lax.broadcasted_iota(jnp.int32, (h, th), 1))
    a_re, a_im = are_ref[group], aim_ref[group]
    lr = jnp.concatenate([a_re, a_re], axis=1)
    li = jnp.concatenate([a_im, a_im], axis=1)
    ldt = jnp.broadcast_to(ldt_ref[group], (1, LANES))
    d_row = jnp.concatenate([d_ref[group], jnp.zeros((1, th - h), _F32)], axis=1)
    brt, bit = bre_ref[g].T, bim_ref[g].T
    c_re, c_im = cre_ref[g], cim_ref[g]
    bt1, bt2 = jnp.concatenate([brt, bit], axis=1), jnp.concatenate([bit, brt], axis=1)
    ct1, ct2 = jnp.concatenate([c_re, c_im], axis=1), jnp.concatenate([c_im, c_re], axis=1)
    dt = jnp.exp(ldt)
    mag = jnp.exp(lr * dt)
    ar, ai = mag * jnp.cos(li * dt), mag * jnp.sin(li * dt)
    nr, ni = ar - 1.0, ai
    den = lr * lr + li * li
    qr = (nr * lr + ni * li) / den
    qi = (ni * lr - nr * li) / den
    b1 = qr * bt1 + (qi * sgn) * bt2
    b2 = (qr * sgn) * bt2 - qi * bt1
    c1 = ct1 * (-sgn)
    c2 = -ct2
    pr, pi = _complex_powers(ar, ai, j, nbits)
    rr, ri = _complex_powers(ar, ai, jnp.maximum(t - 1 - j, 0), nbits)
    cps = (pr[:, None, :] * c1[None] + pi[:, None, :] * c2[None]).reshape(jrows * h, LANES)
    cps = cps.astype(_BF16)
    b_hi = b1.astype(_BF16)
    b_lo = (b1 - b_hi.astype(_F32)).astype(_BF16)
    kf_ref[g] = _dot_nt(b_hi, cps[:th]) + _dot_nt(b_lo, cps[:th]) + jnp.where(diag, d_row, 0.0)
    wo_ref[g] = cps[h:h + th]
    ws_ref[g] = (rr[:t, None, :] * b1[None] + ri[:t, None, :] * b2[None]).reshape(th, LANES).astype(_BF16)
    a1_ref[g] = pr[t:t + 1]
    a2_ref[g] = pi[t:t + 1]


def _s5_kernel(ug_ref, are_ref, aim_ref, d_ref, ldt_ref, bre_ref, bim_ref, cre_ref, cim_ref, yg_ref,
               kf_ref, ws_ref, wo_ref, a1_ref, a2_ref,
               mt_ref, xg_ref, yi_ref, sl_ref, slsw_ref, sin_ref, *, chunks_per_seq):
    nb = ug_ref.shape[0]
    t, th, h = S5_CHUNK, S5_TH, SSM_GROUP
    t_hi_n = t // SUBLANES
    n_chunks = ug_ref.shape[1] // t_hi_n
    n_seq = n_chunks // chunks_per_seq
    pitch = S5_SCAN_PITCH
    assert n_seq * nb <= pitch

    for g in range(nb):
        _s5_prepare_group(g, pl.program_id(0) * nb + g, are_ref, aim_ref, d_ref, ldt_ref,
                          bre_ref, bim_ref, cre_ref, cim_ref, kf_ref, ws_ref, wo_ref, a1_ref, a2_ref)
        xg = jnp.concatenate([ug_ref[g, pl.ds(k, n_chunks, stride=t_hi_n), :]
                              for k in range(t_hi_n)], axis=1).astype(_BF16)
        xg_ref[g] = xg
        s = _dot(xg, ws_ref[g])
        s_sw = pltpu.roll(s, SSM_STATE, axis=1)
        for b in range(n_seq):
            rows = slice(b * chunks_per_seq, (b + 1) * chunks_per_seq)
            sl_ref[pl.ds(b * nb + g, chunks_per_seq, stride=pitch), :] = s[rows]
            slsw_ref[pl.ds(b * nb + g, chunks_per_seq, stride=pitch), :] = s_sw[rows]

    a1 = jnp.concatenate([a1_ref[g] for g in range(nb)] * n_seq, axis=0)
    a2 = jnp.concatenate([a2_ref[g] for g in range(nb)] * n_seq, axis=0) * _re_im_sign()
    n_rows = n_seq * nb

    def times(a, z):
        return a[0] * z[0] + a[1] * z[1], a[0] * z[1] - a[1] * z[0]

    def plus(z, w):
        return z[0] + w[0], z[1] + w[1]

    def local(c):
        return (sl_ref[c * pitch:c * pitch + n_rows, :], slsw_ref[c * pitch:c * pitch + n_rows, :])

    radix = S5_SCAN_RADIX
    powers = [(a1, a2)]
    for _ in range(radix - 1):
        pr, pi_signed = powers[-1]
        sgn = _re_im_sign()
        ar, ai, qr, qi = a1, a2 * sgn, pr, pi_signed * sgn
        powers.append((qr * ar - qi * ai, (qr * ai + qi * ar) * sgn))
    s = (jnp.zeros((n_rows, LANES), _F32), jnp.zeros((n_rows, LANES), _F32))
    for c0 in range(0, chunks_per_seq, radix):
        partial = None
        for j in range(radix):
            if j == 0:
                entering = s[0]
            else:
                entering = powers[j - 1][0] * s[0] + powers[j - 1][1] * s[1] + partial[0]
            sin_ref[(c0 + j) * pitch:(c0 + j) * pitch + n_rows, :] = entering
            partial = local(c0 + j) if partial is None else plus(times(powers[0], partial), local(c0 + j))
        s = plus(times(powers[radix - 1], s), partial)

    for g in range(nb):
        zk = jnp.concatenate([jnp.zeros((h, th), _F32), kf_ref[g]], axis=1)
        for k in range(t):
            mt_ref[g, k * h:(k + 1) * h, :] = zk[:, th - k * h:2 * th - k * h].astype(_BF16)
        yi_ref[g] = _dot(xg_ref[g], mt_ref[g])

    for g in range(nb):
        s_in = jnp.concatenate([sin_ref[pl.ds(b * nb + g, chunks_per_seq, stride=pitch), :]
                                for b in range(n_seq)], axis=0).astype(_BF16)
        y = yi_ref[g] + _dot_nt(s_in, wo_ref[g])
        for k in range(t_hi_n):
            yg_ref[g, pl.ds(k, n_chunks, stride=t_hi_n), :] = y[:, k * LANES:(k + 1) * LANES]


def _s5_core(ug, per_model, per_group, chunks_per_seq):
    _, n_rows, _ = ug.shape
    h, th, nb = SSM_GROUP, S5_TH, GROUPS_PER_BLOCK
    group_rows = pl.BlockSpec((nb, n_rows, LANES), lambda i: (i, 0, 0))
    return pl.pallas_call(
        functools.partial(_s5_kernel, chunks_per_seq=chunks_per_seq),
        grid=(SSM_GROUPS // nb,),
        in_specs=[group_rows]
        + [pl.BlockSpec(p.shape, lambda i: (0, 0, 0)) for p in per_model]
        + [pl.BlockSpec((nb,) + p.shape[1:], lambda i: (i, 0, 0)) for p in per_group],
        out_specs=group_rows,
        out_shape=jax.ShapeDtypeStruct(ug.shape, _F32),
        scratch_shapes=[pltpu.VMEM((nb, h, th), _F32),
                        pltpu.VMEM((nb, th, LANES), _BF16),
                        pltpu.VMEM((nb, th, LANES), _BF16),
                        pltpu.VMEM((nb, 1, LANES), _F32),
                        pltpu.VMEM((nb, 1, LANES), _F32),
                        pltpu.VMEM((nb, th, th), _BF16),
                        pltpu.VMEM((nb, n_rows * SUBLANES // S5_CHUNK, th), _BF16),
                        pltpu.VMEM((nb, n_rows * SUBLANES // S5_CHUNK, th), _F32)]
        + [pltpu.VMEM((chunks_per_seq * S5_SCAN_PITCH, LANES), _F32)] * 3,
        compiler_params=pltpu.CompilerParams(dimension_semantics=("arbitrary",),
                                             vmem_limit_bytes=VMEM_LIMIT_BYTES),
        name="s5_core",
    )(ug, *per_model, *per_group)


def _out_kernel(yc_ref, yg_ref, zs_ref, x_ref, wg_ref, bg_ref, wo_ref, gp_ref, o_ref, slab_ref, ys_ref):
    tm = x_ref.shape[0]
    nb, sub = GROUPS_PER_BLOCK, SUBLANES
    rp = tm // OUT_ROW_PARTS
    slabs_per_block = COL_BLOCK // LANES

    def activation(part):
        rows = slice(part * rp, (part + 1) * rp)
        grows = slice(part * rp // sub, (part + 1) * rp // sub)
        acc = {"pre": bg_ref[...], "y": []}

        def block(blk):
            gvs = range(blk * slabs_per_block, (blk + 1) * slabs_per_block)
            for gv in gvs:
                tok = _lane_block_transpose([yg_ref[gv * nb + gl, grows, :] for gl in range(nb)])
                for tl in range(sub):
                    slab_ref[gv, pl.ds(part * rp + tl, rp // sub, stride=sub), :] = tok[tl]
            y = jax.nn.gelu(jnp.concatenate([slab_ref[gv, rows, :] for gv in gvs], axis=1))
            acc["y"].append(y)
            acc["pre"] = acc["pre"] + _dot(y.astype(_BF16), wg_ref[blk * COL_BLOCK:(blk + 1) * COL_BLOCK, :])

        def gate():
            y = jnp.concatenate(acc["y"], axis=1)
            ys_ref[part] = (y * _sigmoid(acc["pre"]) * zs_ref[rows, :].astype(_F32)).astype(_BF16)

        return [functools.partial(block, blk) for blk in range(D_SSM // COL_BLOCK)] + [gate]

    def conv_projection(part):
        rows = slice(part * rp, (part + 1) * rp)

        def cols(c):
            sl = slice(c * OUT_COL_BLOCK, (c + 1) * OUT_COL_BLOCK)
            o_ref[rows, sl] = _dot(yc_ref[rows, :], wo_ref[:D_CONV, sl])

        return [functools.partial(cols, c) for c in range(D_MODEL // OUT_COL_BLOCK)]

    def ssm_projection(part):
        rows = slice(part * rp, (part + 1) * rp)

        def cols(c):
            sl = slice(c * OUT_COL_BLOCK, (c + 1) * OUT_COL_BLOCK)
            o_ref[rows, sl] += _dot(ys_ref[part], wo_ref[D_CONV:, sl])

        return [functools.partial(cols, c) for c in range(D_MODEL // OUT_COL_BLOCK)]

    def norm_residual(part):
        rows = slice(part * rp, (part + 1) * rp)
        o = o_ref[rows, :]
        ms = jnp.mean(o * o, axis=-1, keepdims=True)
        o_ref[rows, :] = x_ref[rows, :] + o * lax.rsqrt(ms + EPS) * gp_ref[...]

    for part in range(OUT_ROW_PARTS + 1):
        mxu, vpu = [], []
        if part > 0:
            mxu += ssm_projection(part - 1)
        if part > 1:
            vpu.append(functools.partial(norm_residual, part - 2))
        if part < OUT_ROW_PARTS:
            mxu += conv_projection(part)
            vpu += activation(part)
        _interleave(mxu, vpu)
    norm_residual(OUT_ROW_PARTS - 1)


def _out_proj(yc, yg, zs, x2d, w_glu_bf16, b_glu, w_out_bf16, norm_g):
    n_tok = x2d.shape[0]
    tm = TOKEN_TILE
    const = lambda i: (0, 0)
    tile = lambda w: pl.BlockSpec((tm, w), lambda i: (i, 0))
    return pl.pallas_call(
        _out_kernel,
        grid=(n_tok // tm,),
        in_specs=[
            tile(D_CONV),
            pl.BlockSpec((SSM_GROUPS, tm // SUBLANES, LANES), lambda i: (0, i, 0)),
            tile(D_SSM), tile(D_MODEL),
            pl.BlockSpec((D_SSM, D_SSM), const, pipeline_mode=pl.Buffered(1)),
            pl.BlockSpec((1, D_SSM), const),
            pl.BlockSpec((D_CONV + D_SSM, D_MODEL), const, pipeline_mode=pl.Buffered(1)),
            pl.BlockSpec((1, D_MODEL), const),
        ],
        out_specs=tile(D_MODEL),
        out_shape=jax.ShapeDtypeStruct((n_tok, D_MODEL), _F32),
        scratch_shapes=[pltpu.VMEM((D_SSM // LANES, tm, LANES), _F32),
                        pltpu.VMEM((OUT_ROW_PARTS, tm // OUT_ROW_PARTS, D_SSM), _BF16)],
        compiler_params=pltpu.CompilerParams(dimension_semantics=("arbitrary",),
                                             vmem_limit_bytes=VMEM_LIMIT_BYTES),
        name="glu_outproj",
    )(yc, yg, zs, x2d, w_glu_bf16, b_glu, w_out_bf16, norm_g)


def kernel(x, norm_pre_g, w_in, conv_w, conv_b, ssm_a_re, ssm_a_im, ssm_log_dt, ssm_b_re, ssm_b_im,
           ssm_c_re, ssm_c_im, ssm_d, w_glu, b_glu, w_out, norm_post_g):
    bsz, seq_len, _ = x.shape
    n_tok = bsz * seq_len
    g, p, h, t = SSM_GROUPS, SSM_STATE, SSM_GROUP, S5_CHUNK
    assert seq_len % TOKEN_TILE == 0 and TOKEN_TILE % t == 0
    x2d = x.reshape(n_tok, D_MODEL)

    yc, ug, zs, w_glu_bf16, w_out_bf16 = _inproj(
        x2d, norm_pre_g.reshape(1, -1), w_in, conv_w, conv_b.reshape(1, -1), w_glu, w_out, seq_len)

    yg = _s5_core(ug,
                  (ssm_a_re.reshape(g, 1, p), ssm_a_im.reshape(g, 1, p), ssm_d.reshape(g, 1, h),
                   ssm_log_dt.reshape(g, 1, 1)),
                  (ssm_b_re, ssm_b_im, ssm_c_re, ssm_c_im), seq_len // t)

    out = _out_proj(yc, yg, zs, x2d, w_glu_bf16, b_glu.reshape(1, -1), w_out_bf16,
                    norm_post_g.reshape(1, -1))
    return out.reshape(bsz, seq_len, D_MODEL)
```

```python
import functools

import jax
import jax.numpy as jnp
from jax import lax
from jax.experimental import pallas as pl
from jax.experimental.pallas import tpu as pltpu

D_MODEL = 2048
D_CONV = 1024
D_SSM = 1024
SSM_GROUP = 16
SSM_GROUPS = 64
SSM_STATE = 64
N_IN = 4 * D_CONV + 2 * D_SSM
EPS = 1e-6

S5_CHUNK = 16
S5_TH = S5_CHUNK * SSM_GROUP
S5_POW_ROWS = S5_CHUNK + 8
S5_SCAN_PITCH = 24
S5_SCAN_RADIX = 4
LANES = 128
SUBLANES = 8
GROUPS_PER_BLOCK = LANES // SSM_GROUP
TOKEN_TILE = 512
COL_BLOCK = 256
WEIGHT_STAGES = 6
OUT_ROW_PARTS = 2
OUT_COL_BLOCK = 512
VMEM_LIMIT_BYTES = 56 * 1024 * 1024

_F32 = jnp.float32
_BF16 = jnp.bfloat16


def _dot(a, b):
    return jnp.dot(a, b, preferred_element_type=_F32)


def _dot_nt(a, b):
    return lax.dot_general(a, b, (((1,), (1,)), ((), ())), preferred_element_type=_F32)


def _sigmoid(z):
    return jax.nn.sigmoid(z)


def _interleave(mxu_items, vpu_items):
    n = max(len(mxu_items), 1)
    done = 0
    for i, item in enumerate(mxu_items):
        item()
        upto = -(-len(vpu_items) * (i + 1) // n)
        for piece in vpu_items[done:upto]:
            piece()
        done = upto
    for piece in vpu_items[done:]:
        piece()


def _load_weight_as_bf16(w_hbm, w_ref, stages, sem, cols):
    a, r, n = w_hbm.shape
    slots, n_chunks = len(stages), n // cols

    def chunk(c):
        return pltpu.make_async_copy(w_hbm.at[:, :, pl.ds(c * cols, cols)], stages[c % slots],
                                     sem.at[c % slots])

    for c in range(min(slots - 1, n_chunks)):
        chunk(c).start()
    for c in range(n_chunks):
        if c + slots - 1 < n_chunks:
            chunk(c + slots - 1).start()
        chunk(c).wait()
        for k in range(a):
            w_ref[k * r:(k + 1) * r, c * cols:(c + 1) * cols] = stages[c % slots][k].astype(_BF16)


def _lane_block_transpose(xs):
    nb = len(xs)
    diag = _diagonal_gather(xs)
    moved = [w if d == 0 else pltpu.roll(w, d * SSM_GROUP, axis=1) for d, w in enumerate(diag)]
    back = _diagonal_gather(moved)
    return [back[(-b) % nb] for b in range(nb)]


def _diagonal_gather(arrs):
    assert len(arrs) == 8
    c = lax.broadcasted_iota(jnp.int32, (1, LANES), 1) // SSM_GROUP
    c0, c1, c2 = c & 1, (c >> 1) & 1, (c >> 2) & 1
    s1 = {(k0, hi): jnp.where((c0 ^ k0) == 1, arrs[2 * hi + 1], arrs[2 * hi])
          for k0 in range(2) for hi in range(4)}
    s2 = {}
    for k0 in range(2):
        for k1 in range(2):
            e1 = c1 ^ k1 ^ (c0 & k0)
            for top in range(2):
                s2[(k0, k1, top)] = jnp.where(e1 == 1, s1[(k0, 2 * top + 1)], s1[(k0, 2 * top)])
    out = []
    for k in range(8):
        k0, k1, k2 = k & 1, (k >> 1) & 1, (k >> 2) & 1
        carry0 = c0 & k0
        carry1 = (c1 & k1) | (c1 & carry0) | (k1 & carry0)
        e2 = c2 ^ k2 ^ carry1
        out.append(jnp.where(e2 == 1, s2[(k0, k1, 1)], s2[(k0, k1, 0)]))
    return out


def _inproj_kernel(x_ref, g_ref, w_hbm, cw_ref, cb_ref, wglu_ref, wout_ref,
                   yc_ref, ug_ref, zs_ref, wglu_bf_ref, wout_bf_ref,
                   carry_ref, slab_ref, raw_ref, w_ref, sem, *, tiles_per_seq):
    tm = x_ref.shape[0]
    nb, sub = GROUPS_PER_BLOCK, SUBLANES
    slabs_per_block = COL_BLOCK // LANES
    n_blocks = D_CONV // COL_BLOCK
    bases = (4 * D_CONV, 0, D_CONV, 2 * D_CONV, 3 * D_CONV, 4 * D_CONV + D_SSM)

    @pl.when(pl.program_id(0) == 0)
    def _():
        n_a = w_hbm.shape[0]
        stages = [raw_ref.at[slot, pl.ds(0, n_a), :, pl.ds(half * LANES, LANES)]
                  for slot in range(raw_ref.shape[0]) for half in range(COL_BLOCK // LANES)]
        stages += [slab_ref.at[pl.ds(i * n_a, n_a)] for i in range(slab_ref.shape[0] // n_a)]
        assert len(stages) == WEIGHT_STAGES
        _load_weight_as_bf16(w_hbm, w_ref, stages, sem, LANES)

    @pl.when(pl.program_id(0) % tiles_per_seq == 0)
    def _():
        carry_ref[...] = jnp.zeros_like(carry_ref)

    x = x_ref[...]
    ms = jnp.mean(x * x, axis=-1, keepdims=True)
    h = (x * lax.rsqrt(ms + EPS) * g_ref[...]).astype(_BF16)

    wglu_bf_ref[...] = wglu_ref[...].astype(_BF16)
    wout_bf_ref[...] = wout_ref[...].astype(_BF16)

    def project_one(blk, j):
        lo = bases[j] + blk * COL_BLOCK
        res = _dot(h, w_ref[:, lo:lo + COL_BLOCK])
        if j == 0:
            for k in range(slabs_per_block):
                slab_ref[blk * slabs_per_block + k] = res[:, k * LANES:(k + 1) * LANES]
        else:
            raw_ref[blk % 2, j - 1] = res

    def project(blk):
        return [functools.partial(project_one, blk, j) for j in range(len(bases))]

    def finish_gate(blk):
        sl = slice(blk * COL_BLOCK, (blk + 1) * COL_BLOCK)
        z_ssm = raw_ref[blk % 2, 4]
        zs_ref[:, sl] = (z_ssm * _sigmoid(z_ssm)).astype(_BF16)

    def finish_u(gv):
        rows = [slab_ref[gv, pl.ds(tl, tm // sub, stride=sub), :] for tl in range(sub)]
        for gl, o in enumerate(_lane_block_transpose(rows)):
            ug_ref[gv * nb + gl] = o

    def finish_conv(blk, half):
        sl = slice(blk * COL_BLOCK + half * LANES, blk * COL_BLOCK + (half + 1) * LANES)
        hl = slice(half * LANES, (half + 1) * LANES)
        b_gate, c_gate, v, z = (raw_ref[blk % 2, j, :, hl] for j in range(4))
        cv = c_gate * v
        ext = jnp.concatenate([carry_ref[:, sl], cv], axis=0)
        conv = (cb_ref[:, sl] + cw_ref[2:3, sl] * cv + cw_ref[1:2, sl] * ext[sub - 1:sub - 1 + tm]
                + cw_ref[0:1, sl] * ext[sub - 2:sub - 2 + tm])
        carry_ref[:, sl] = cv[tm - sub:]
        yc_ref[:, sl] = (b_gate * conv * (z * _sigmoid(z))).astype(_BF16)

    def finish(blk):
        return ([functools.partial(finish_gate, blk)]
                + [functools.partial(finish_u, blk * slabs_per_block + k) for k in range(slabs_per_block)]
                + [functools.partial(finish_conv, blk, half) for half in range(COL_BLOCK // LANES)])

    _interleave(project(0), [])
    for blk in range(1, n_blocks):
        _interleave(project(blk), finish(blk - 1))
    _interleave([], finish(n_blocks - 1))


def _inproj(x2d, norm_g, w_in, conv_w, conv_b, w_glu, w_out, seq_len):
    n_tok = x2d.shape[0]
    tm = TOKEN_TILE
    steps = n_tok // tm
    const = lambda s: (0, 0)
    tile = lambda w: pl.BlockSpec((tm, w), lambda s: (s, 0))
    row_block = lambda w: pl.BlockSpec((w.shape[0] // steps, w.shape[1]), lambda s: (s, 0))
    out_sds = jax.ShapeDtypeStruct((n_tok, D_CONV), _BF16)
    return pl.pallas_call(
        functools.partial(_inproj_kernel, tiles_per_seq=seq_len // tm),
        grid=(steps,),
        in_specs=[
            tile(D_MODEL),
            pl.BlockSpec((1, D_MODEL), const),
            pl.BlockSpec(memory_space=pl.ANY),
            pl.BlockSpec((3, D_CONV), const),
            pl.BlockSpec((1, D_CONV), const),
            row_block(w_glu), row_block(w_out),
        ],
        out_specs=[tile(D_CONV),
                   pl.BlockSpec((SSM_GROUPS, tm // SUBLANES, LANES), lambda s: (0, s, 0)),
                   tile(D_SSM),
                   row_block(w_glu), row_block(w_out)],
        out_shape=[out_sds,
                   jax.ShapeDtypeStruct((SSM_GROUPS, n_tok // SUBLANES, LANES), _F32),
                   out_sds,
                   jax.ShapeDtypeStruct(w_glu.shape, _BF16),
                   jax.ShapeDtypeStruct(w_out.shape, _BF16)],
        scratch_shapes=[pltpu.VMEM((SUBLANES, D_CONV), _F32),
                        pltpu.VMEM((D_SSM // LANES, tm, LANES), _F32),
                        pltpu.VMEM((2, 5, tm, COL_BLOCK), _F32),
                        pltpu.VMEM((D_MODEL, N_IN), _BF16),
                        pltpu.SemaphoreType.DMA((WEIGHT_STAGES,))],
        compiler_params=pltpu.CompilerParams(dimension_semantics=("arbitrary",),
                                             vmem_limit_bytes=VMEM_LIMIT_BYTES),
        name="inproj_conv",
    )(x2d, norm_g, w_in.reshape(D_MODEL // tm, tm, N_IN), conv_w, conv_b, w_glu, w_out)


def _complex_powers(ar, ai, exponent, nbits):
    pr = jnp.ones((exponent.shape[0], ar.shape[1]), _F32)
    pi = jnp.zeros_like(pr)
    cr, ci = ar, ai
    for b in range(nbits):
        bit = ((exponent >> b) & 1) == 1
        fr = jnp.where(bit, cr, 1.0)
        fi = jnp.where(bit, ci, 0.0)
        pr, pi = pr * fr - pi * fi, pr * fi + pi * fr
        cr, ci = cr * cr - ci * ci, 2.0 * cr * ci
    return pr, pi


def _re_im_sign():
    lane = lax.broadcasted_iota(jnp.int32, (1, LANES), 1)
    return jnp.where(lane < SSM_STATE, -1.0, 1.0).astype(_F32)


def _s5_prepare_group(g, group, are_ref, aim_ref, mats_ref, cre_ref, cim_ref,
                      kf_ref, ws_ref, wo_ref, a1_ref, a2_ref):
    t, th, jrows, h = S5_CHUNK, S5_TH, S5_POW_ROWS, SSM_GROUP
    sgn = _re_im_sign()
    nbits = max(1, (jrows - 1).bit_length())
    j = lax.broadcasted_iota(jnp.int32, (jrows, 1), 0)
    diag = (lax.broadcasted_iota(jnp.int32, (h, th), 0) == lax.broadcasted_iota(jnp.int32, (h, th), 1))
    a_re, a_im = are_ref[group], aim_ref[group]
    lr = jnp.concatenate([a_re, a_re], axis=1)
    li = jnp.concatenate([a_im, a_im], axis=1)
    ldt = jnp.broadcast_to(mats_ref[g, 0:1, LANES + h:LANES + h + 1], (1, LANES))
    d_row = jnp.concatenate([mats_ref[g, 0:1, LANES:LANES + h], jnp.zeros((1, th - h), _F32)], axis=1)
    c_re, c_im = cre_ref[g], cim_ref[g]
    bt1 = mats_ref[g, :, 0:LANES]
    bt2 = pltpu.roll(bt1, SSM_STATE, axis=1)
    ct1, ct2 = jnp.concatenate([c_re, c_im], axis=1), jnp.concatenate([c_im, c_re], axis=1)
    dt = jnp.exp(ldt)
    mag = jnp.exp(lr * dt)
    ar, ai = mag * jnp.cos(li * dt), mag * jnp.sin(li * dt)
    nr, ni = ar - 1.0, ai
    den = lr * lr + li * li
    qr = (nr * lr + ni * li) / den
    qi = (ni * lr - nr * li) / den
    b1 = qr * bt1 + (qi * sgn) * bt2
    b2 = (qr * sgn) * bt2 - qi * bt1
    c1 = ct1 * (-sgn)
    c2 = -ct2
    pr, pi = _complex_powers(ar, ai, j, nbits)
    rr, ri = _complex_powers(ar, ai, jnp.maximum(t - 1 - j, 0), nbits)
    cps = (pr[:, None, :] * c1[None] + pi[:, None, :] * c2[None]).reshape(jrows * h, LANES)
    cps = cps.astype(_BF16)
    b_hi = b1.astype(_BF16)
    b_lo = (b1 - b_hi.astype(_F32)).astype(_BF16)
    kf_ref[g] = _dot_nt(b_hi, cps[:th]) + _dot_nt(b_lo, cps[:th]) + jnp.where(diag, d_row, 0.0)
    wo_ref[g] = cps[h:h + th]
    ws_ref[g] = (rr[:t, None, :] * b1[None] + ri[:t, None, :] * b2[None]).reshape(th, LANES).astype(_BF16)
    a1_ref[g] = pr[t:t + 1]
    a2_ref[g] = pi[t:t + 1]


def _s5_kernel(ug_ref, are_ref, aim_ref, mats_ref, cre_ref, cim_ref, yg_ref,
               kf_ref, ws_ref, wo_ref, a1_ref, a2_ref,
               mt_ref, xg_ref, yi_ref, sl_ref, slsw_ref, sin_ref, *, chunks_per_seq):
    nb = ug_ref.shape[0]
    t, th, h = S5_CHUNK, S5_TH, SSM_GROUP
    t_hi_n = t // SUBLANES
    n_chunks = ug_ref.shape[1] // t_hi_n
    n_seq = n_chunks // chunks_per_seq
    pitch = S5_SCAN_PITCH
    assert n_seq * nb <= pitch

    for g in range(nb):
        _s5_prepare_group(g, pl.program_id(0) * nb + g, are_ref, aim_ref, mats_ref,
                          cre_ref, cim_ref, kf_ref, ws_ref, wo_ref, a1_ref, a2_ref)
        xg = jnp.concatenate([ug_ref[g, pl.ds(k, n_chunks, stride=t_hi_n), :]
                              for k in range(t_hi_n)], axis=1).astype(_BF16)
        xg_ref[g] = xg
        s = _dot(xg, ws_ref[g])
        s_sw = pltpu.roll(s, SSM_STATE, axis=1)
        for b in range(n_seq):
            rows = slice(b * chunks_per_seq, (b + 1) * chunks_per_seq)
            sl_ref[pl.ds(b * nb + g, chunks_per_seq, stride=pitch), :] = s[rows]
            slsw_ref[pl.ds(b * nb + g, chunks_per_seq, stride=pitch), :] = s_sw[rows]

    a1 = jnp.concatenate([a1_ref[g] for g in range(nb)] * n_seq, axis=0)
    a2 = jnp.concatenate([a2_ref[g] for g in range(nb)] * n_seq, axis=0) * _re_im_sign()
    n_rows = n_seq * nb

    def times(a, z):
        return a[0] * z[0] + a[1] * z[1], a[0] * z[1] - a[1] * z[0]

    def plus(z, w):
        return z[0] + w[0], z[1] + w[1]

    def local(c):
        return (sl_ref[c * pitch:c * pitch + n_rows, :], slsw_ref[c * pitch:c * pitch + n_rows, :])

    radix = S5_SCAN_RADIX
    powers = [(a1, a2)]
    for _ in range(radix - 1):
        pr, pi_signed = powers[-1]
        sgn = _re_im_sign()
        ar, ai, qr, qi = a1, a2 * sgn, pr, pi_signed * sgn
        powers.append((qr * ar - qi * ai, (qr * ai + qi * ar) * sgn))
    s = (jnp.zeros((n_rows, LANES), _F32), jnp.zeros((n_rows, LANES), _F32))
    for c0 in range(0, chunks_per_seq, radix):
        partial = None
        for j in range(radix):
            if j == 0:
                entering = s[0]
            else:
                entering = powers[j - 1][0] * s[0] + powers[j - 1][1] * s[1] + partial[0]
            sin_ref[(c0 + j) * pitch:(c0 + j) * pitch + n_rows, :] = entering
            partial = local(c0 + j) if partial is None else plus(times(powers[0], partial), local(c0 + j))
        s = plus(times(powers[radix - 1], s), partial)

    for g in range(nb):
        zk = jnp.concatenate([jnp.zeros((h, th), _F32), kf_ref[g]], axis=1)
        for k in range(t):
            mt_ref[g, k * h:(k + 1) * h, :] = zk[:, th - k * h:2 * th - k * h].astype(_BF16)
        yi_ref[g] = _dot(xg_ref[g], mt_ref[g])

    for g in range(nb):
        s_in = jnp.concatenate([sin_ref[pl.ds(b * nb + g, chunks_per_seq, stride=pitch), :]
                                for b in range(n_seq)], axis=0).astype(_BF16)
        y = yi_ref[g] + _dot_nt(s_in, wo_ref[g])
        for k in range(t_hi_n):
            yg_ref[g, pl.ds(k, n_chunks, stride=t_hi_n), :] = y[:, k * LANES:(k + 1) * LANES]


def _s5_core(ug, per_model, per_group, chunks_per_seq):
    _, n_rows, _ = ug.shape
    h, th, nb = SSM_GROUP, S5_TH, GROUPS_PER_BLOCK
    group_rows = pl.BlockSpec((nb, n_rows, LANES), lambda i: (i, 0, 0))
    return pl.pallas_call(
        functools.partial(_s5_kernel, chunks_per_seq=chunks_per_seq),
        grid=(SSM_GROUPS // nb,),
        in_specs=[group_rows]
        + [pl.BlockSpec(p.shape, lambda i: (0, 0, 0)) for p in per_model]
        + [pl.BlockSpec((nb,) + p.shape[1:], lambda i: (i, 0, 0)) for p in per_group],
        out_specs=group_rows,
        out_shape=jax.ShapeDtypeStruct(ug.shape, _F32),
        scratch_shapes=[pltpu.VMEM((nb, h, th), _F32),
                        pltpu.VMEM((nb, th, LANES), _BF16),
                        pltpu.VMEM((nb, th, LANES), _BF16),
                        pltpu.VMEM((nb, 1, LANES), _F32),
                        pltpu.VMEM((nb, 1, LANES), _F32),
                        pltpu.VMEM((nb, th, th), _BF16),
                        pltpu.VMEM((nb, n_rows * SUBLANES // S5_CHUNK, th), _BF16),
                        pltpu.VMEM((nb, n_rows * SUBLANES // S5_CHUNK, th), _F32)]
        + [pltpu.VMEM((chunks_per_seq * S5_SCAN_PITCH, LANES), _F32)] * 3,
        compiler_params=pltpu.CompilerParams(dimension_semantics=("arbitrary",),
                                             vmem_limit_bytes=VMEM_LIMIT_BYTES),
        name="s5_core",
    )(ug, *per_model, *per_group)


def _out_kernel(yc_ref, yg_ref, zs_ref, x_ref, wg_ref, bg_ref, wo_ref, gp_ref, o_ref, slab_ref, ys_ref):
    tm = x_ref.shape[0]
    nb, sub = GROUPS_PER_BLOCK, SUBLANES
    rp = tm // OUT_ROW_PARTS
    slabs_per_block = COL_BLOCK // LANES

    def activation(part):
        rows = slice(part * rp, (part + 1) * rp)
        grows = slice(part * rp // sub, (part + 1) * rp // sub)
        acc = {"pre": bg_ref[...], "y": []}

        def block(blk):
            gvs = range(blk * slabs_per_block, (blk + 1) * slabs_per_block)
            for gv in gvs:
                tok = _lane_block_transpose([yg_ref[gv * nb + gl, grows, :] for gl in range(nb)])
                for tl in range(sub):
                    slab_ref[gv, pl.ds(part * rp + tl, rp // sub, stride=sub), :] = tok[tl]
            y = jax.nn.gelu(jnp.concatenate([slab_ref[gv, rows, :] for gv in gvs], axis=1))
            acc["y"].append(y)
            acc["pre"] = acc["pre"] + _dot(y.astype(_BF16), wg_ref[blk * COL_BLOCK:(blk + 1) * COL_BLOCK, :])

        def gate():
            y = jnp.concatenate(acc["y"], axis=1)
            ys_ref[part] = (y * _sigmoid(acc["pre"]) * zs_ref[rows, :].astype(_F32)).astype(_BF16)

        return [functools.partial(block, blk) for blk in range(D_SSM // COL_BLOCK)] + [gate]

    def conv_projection(part):
        rows = slice(part * rp, (part + 1) * rp)

        def cols(c):
            sl = slice(c * OUT_COL_BLOCK, (c + 1) * OUT_COL_BLOCK)
            o_ref[rows, sl] = _dot(yc_ref[rows, :], wo_ref[:D_CONV, sl])

        return [functools.partial(cols, c) for c in range(D_MODEL // OUT_COL_BLOCK)]

    def ssm_projection(part):
        rows = slice(part * rp, (part + 1) * rp)

        def cols(c):
            sl = slice(c * OUT_COL_BLOCK, (c + 1) * OUT_COL_BLOCK)
            o_ref[rows, sl] += _dot(ys_ref[part], wo_ref[D_CONV:, sl])

        return [functools.partial(cols, c) for c in range(D_MODEL // OUT_COL_BLOCK)]

    def norm_residual(part):
        rows = slice(part * rp, (part + 1) * rp)
        o = o_ref[rows, :]
        ms = jnp.mean(o * o, axis=-1, keepdims=True)
        o_ref[rows, :] = x_ref[rows, :] + o * lax.rsqrt(ms + EPS) * gp_ref[...]

    for part in range(OUT_ROW_PARTS + 1):
        mxu, vpu = [], []
        if part > 0:
            mxu += ssm_projection(part - 1)
        if part > 1:
            vpu.append(functools.partial(norm_residual, part - 2))
        if part < OUT_ROW_PARTS:
            mxu += conv_projection(part)
            vpu += activation(part)
        _interleave(mxu, vpu)
    norm_residual(OUT_ROW_PARTS - 1)


def _out_proj(yc, yg, zs, x2d, w_glu_bf16, b_glu, w_out_bf16, norm_g):
    n_tok = x2d.shape[0]
    tm = TOKEN_TILE
    const = lambda i: (0, 0)
    tile = lambda w: pl.BlockSpec((tm, w), lambda i: (i, 0))
    return pl.pallas_call(
        _out_kernel,
        grid=(n_tok // tm,),
        in_specs=[
            tile(D_CONV),
            pl.BlockSpec((SSM_GROUPS, tm // SUBLANES, LANES), lambda i: (0, i, 0)),
            tile(D_SSM), tile(D_MODEL),
            pl.BlockSpec((D_SSM, D_SSM), const, pipeline_mode=pl.Buffered(1)),
            pl.BlockSpec((1, D_SSM), const),
            pl.BlockSpec((D_CONV + D_SSM, D_MODEL), const, pipeline_mode=pl.Buffered(1)),
            pl.BlockSpec((1, D_MODEL), const),
        ],
        out_specs=tile(D_MODEL),
        out_shape=jax.ShapeDtypeStruct((n_tok, D_MODEL), _F32),
        scratch_shapes=[pltpu.VMEM((D_SSM // LANES, tm, LANES), _F32),
                        pltpu.VMEM((OUT_ROW_PARTS, tm // OUT_ROW_PARTS, D_SSM), _BF16)],
        compiler_params=pltpu.CompilerParams(dimension_semantics=("arbitrary",),
                                             vmem_limit_bytes=VMEM_LIMIT_BYTES),
        name="glu_outproj",
    )(yc, yg, zs, x2d, w_glu_bf16, b_glu, w_out_bf16, norm_g)


def kernel(x, norm_pre_g, w_in, conv_w, conv_b, ssm_a_re, ssm_a_im, ssm_log_dt, ssm_b_re, ssm_b_im,
           ssm_c_re, ssm_c_im, ssm_d, w_glu, b_glu, w_out, norm_post_g):
    bsz, seq_len, _ = x.shape
    n_tok = bsz * seq_len
    g, p, h, t = SSM_GROUPS, SSM_STATE, SSM_GROUP, S5_CHUNK
    assert seq_len % TOKEN_TILE == 0 and TOKEN_TILE % t == 0
    x2d = x.reshape(n_tok, D_MODEL)

    yc, ug, zs, w_glu_bf16, w_out_bf16 = _inproj(
        x2d, norm_pre_g.reshape(1, -1), w_in, conv_w, conv_b.reshape(1, -1), w_glu, w_out, seq_len)

    tail = jnp.concatenate([ssm_d.reshape(g, h), ssm_log_dt.reshape(g, 1),
                            jnp.zeros((g, LANES - h - 1), _F32)], axis=-1)
    mats = jnp.concatenate([ssm_b_re.transpose(0, 2, 1), ssm_b_im.transpose(0, 2, 1),
                            jnp.broadcast_to(tail[:, None, :], (g, h, LANES))], axis=-1)
    yg = _s5_core(ug, (ssm_a_re.reshape(g, 1, p), ssm_a_im.reshape(g, 1, p)),
                  (mats, ssm_c_re, ssm_c_im), seq_len // t)

    out = _out_proj(yc, yg, zs, x2d, w_glu_bf16, b_glu.reshape(1, -1), w_out_bf16,
                    norm_post_g.reshape(1, -1))
    return out.reshape(bsz, seq_len, D_MODEL)
```

```python
import functools

import jax
import jax.numpy as jnp
from jax import lax
from jax.experimental import pallas as pl
from jax.experimental.pallas import tpu as pltpu

D_MODEL = 2048
D_CONV = 1024
D_SSM = 1024
SSM_GROUP = 16
SSM_GROUPS = 64
SSM_STATE = 64
N_IN = 4 * D_CONV + 2 * D_SSM
EPS = 1e-6

S5_CHUNK = 16
S5_TH = S5_CHUNK * SSM_GROUP
S5_POW_ROWS = S5_CHUNK + 8
S5_SCAN_PITCH = 24
S5_SCAN_RADIX = 4
LANES = 128
SUBLANES = 8
GROUPS_PER_BLOCK = LANES // SSM_GROUP
TOKEN_TILE = 512
COL_BLOCK = 256
WEIGHT_STAGES = 6
OUT_ROW_PARTS = 2
OUT_COL_BLOCK = 512
VMEM_LIMIT_BYTES = 56 * 1024 * 1024

_F32 = jnp.float32
_BF16 = jnp.bfloat16


def _dot(a, b):
    return jnp.dot(a, b, preferred_element_type=_F32)


def _dot_nt(a, b):
    return lax.dot_general(a, b, (((1,), (1,)), ((), ())), preferred_element_type=_F32)


def _sigmoid(z):
    return jax.nn.sigmoid(z)


def _interleave(mxu_items, vpu_items):
    n = max(len(mxu_items), 1)
    done = 0
    for i, item in enumerate(mxu_items):
        item()
        upto = -(-len(vpu_items) * (i + 1) // n)
        for piece in vpu_items[done:upto]:
            piece()
        done = upto
    for piece in vpu_items[done:]:
        piece()


def _load_weight_as_bf16(w_hbm, w_ref, stages, sem, cols):
    a, r, n = w_hbm.shape
    slots, n_chunks = len(stages), n // cols

    def chunk(c):
        return pltpu.make_async_copy(w_hbm.at[:, :, pl.ds(c * cols, cols)], stages[c % slots],
                                     sem.at[c % slots])

    for c in range(min(slots - 1, n_chunks)):
        chunk(c).start()
    for c in range(n_chunks):
        if c + slots - 1 < n_chunks:
            chunk(c + slots - 1).start()
        chunk(c).wait()
        for k in range(a):
            w_ref[k * r:(k + 1) * r, c * cols:(c + 1) * cols] = stages[c % slots][k].astype(_BF16)


def _lane_block_transpose(xs):
    nb = len(xs)
    diag = _diagonal_gather(xs)
    moved = [w if d == 0 else pltpu.roll(w, d * SSM_GROUP, axis=1) for d, w in enumerate(diag)]
    back = _diagonal_gather(moved)
    return [back[(-b) % nb] for b in range(nb)]


def _diagonal_gather(arrs):
    assert len(arrs) == 8
    c = lax.broadcasted_iota(jnp.int32, (1, LANES), 1) // SSM_GROUP
    c0, c1, c2 = c & 1, (c >> 1) & 1, (c >> 2) & 1
    s1 = {(k0, hi): jnp.where((c0 ^ k0) == 1, arrs[2 * hi + 1], arrs[2 * hi])
          for k0 in range(2) for hi in range(4)}
    s2 = {}
    for k0 in range(2):
        for k1 in range(2):
            e1 = c1 ^ k1 ^ (c0 & k0)
            for top in range(2):
                s2[(k0, k1, top)] = jnp.where(e1 == 1, s1[(k0, 2 * top + 1)], s1[(k0, 2 * top)])
    out = []
    for k in range(8):
        k0, k1, k2 = k & 1, (k >> 1) & 1, (k >> 2) & 1
        carry0 = c0 & k0
        carry1 = (c1 & k1) | (c1 & carry0) | (k1 & carry0)
        e2 = c2 ^ k2 ^ carry1
        out.append(jnp.where(e2 == 1, s2[(k0, k1, 1)], s2[(k0, k1, 0)]))
    return out


def _inproj_kernel(x_ref, g_ref, w_hbm, cw_ref, cb_ref, wglu_ref, wout_ref,
                   yc_ref, ug_ref, zs_ref, wglu_bf_ref, wout_bf_ref,
                   carry_ref, slab_ref, raw_ref, w_ref, sem, *, tiles_per_seq):
    tm = x_ref.shape[0]
    nb, sub = GROUPS_PER_BLOCK, SUBLANES
    slabs_per_block = COL_BLOCK // LANES
    n_blocks = D_CONV // COL_BLOCK
    bases = (4 * D_CONV, 0, D_CONV, 2 * D_CONV, 3 * D_CONV, 4 * D_CONV + D_SSM)

    @pl.when(pl.program_id(0) == 0)
    def _():
        n_a = w_hbm.shape[0]
        stages = [raw_ref.at[slot, pl.ds(0, n_a), :, pl.ds(half * LANES, LANES)]
                  for slot in range(raw_ref.shape[0]) for half in range(COL_BLOCK // LANES)]
        stages += [slab_ref.at[pl.ds(i * n_a, n_a)] for i in range(slab_ref.shape[0] // n_a)]
        assert len(stages) == WEIGHT_STAGES
        _load_weight_as_bf16(w_hbm, w_ref, stages, sem, LANES)

    @pl.when(pl.program_id(0) % tiles_per_seq == 0)
    def _():
        carry_ref[...] = jnp.zeros_like(carry_ref)

    x = x_ref[...]
    ms = jnp.mean(x * x, axis=-1, keepdims=True)
    h = (x * lax.rsqrt(ms + EPS) * g_ref[...]).astype(_BF16)

    wglu_bf_ref[...] = wglu_ref[...].astype(_BF16)
    wout_bf_ref[...] = wout_ref[...].astype(_BF16)

    def project_one(blk, j):
        lo = bases[j] + blk * COL_BLOCK
        res = _dot(h, w_ref[:, lo:lo + COL_BLOCK])
        if j == 0:
            for k in range(slabs_per_block):
                slab_ref[blk * slabs_per_block + k] = res[:, k * LANES:(k + 1) * LANES]
        else:
            raw_ref[blk % 2, j - 1] = res

    def project(blk):
        return [functools.partial(project_one, blk, j) for j in range(len(bases))]

    def finish_gate(blk):
        sl = slice(blk * COL_BLOCK, (blk + 1) * COL_BLOCK)
        z_ssm = raw_ref[blk % 2, 4]
        zs_ref[:, sl] = (z_ssm * _sigmoid(z_ssm)).astype(_BF16)

    def finish_u(gv):
        rows = [slab_ref[gv, pl.ds(tl, tm // sub, stride=sub), :] for tl in range(sub)]
        for gl, o in enumerate(_lane_block_transpose(rows)):
            ug_ref[gv * nb + gl] = o

    def finish_conv(blk, half):
        sl = slice(blk * COL_BLOCK + half * LANES, blk * COL_BLOCK + (half + 1) * LANES)
        hl = slice(half * LANES, (half + 1) * LANES)
        b_gate, c_gate, v, z = (raw_ref[blk % 2, j, :, hl] for j in range(4))
        cv = c_gate * v
        ext = jnp.concatenate([carry_ref[:, sl], cv], axis=0)
        conv = (cb_ref[:, sl] + cw_ref[2:3, sl] * cv + cw_ref[1:2, sl] * ext[sub - 1:sub - 1 + tm]
                + cw_ref[0:1, sl] * ext[sub - 2:sub - 2 + tm])
        carry_ref[:, sl] = cv[tm - sub:]
        yc_ref[:, sl] = (b_gate * conv * (z * _sigmoid(z))).astype(_BF16)

    def finish(blk):
        return ([functools.partial(finish_gate, blk)]
                + [functools.partial(finish_u, blk * slabs_per_block + k) for k in range(slabs_per_block)]
                + [functools.partial(finish_conv, blk, half) for half in range(COL_BLOCK // LANES)])

    _interleave(project(0), [])
    for blk in range(1, n_blocks):
        _interleave(project(blk), finish(blk - 1))
    _interleave([], finish(n_blocks - 1))


def _inproj(x2d, norm_g, w_in, conv_w, conv_b, w_glu, w_out, seq_len):
    n_tok = x2d.shape[0]
    tm = TOKEN_TILE
    steps = n_tok // tm
    const = lambda s: (0, 0)
    tile = lambda w: pl.BlockSpec((tm, w), lambda s: (s, 0))
    row_block = lambda w: pl.BlockSpec((w.shape[0] // steps, w.shape[1]), lambda s: (s, 0))
    out_sds = jax.ShapeDtypeStruct((n_tok, D_CONV), _BF16)
    return pl.pallas_call(
        functools.partial(_inproj_kernel, tiles_per_seq=seq_len // tm),
        grid=(steps,),
        in_specs=[
            tile(D_MODEL),
            pl.BlockSpec((1, D_MODEL), const),
            pl.BlockSpec(memory_space=pl.ANY),
            pl.BlockSpec((3, D_CONV), const),
            pl.BlockSpec((1, D_CONV), const),
            row_block(w_glu), row_block(w_out),
        ],
        out_specs=[tile(D_CONV),
                   pl.BlockSpec((SSM_GROUPS, tm // SUBLANES, LANES), lambda s: (0, s, 0)),
                   tile(D_SSM),
                   row_block(w_glu), row_block(w_out)],
        out_shape=[out_sds,
                   jax.ShapeDtypeStruct((SSM_GROUPS, n_tok // SUBLANES, LANES), _F32),
                   out_sds,
                   jax.ShapeDtypeStruct(w_glu.shape, _BF16),
                   jax.ShapeDtypeStruct(w_out.shape, _BF16)],
        scratch_shapes=[pltpu.VMEM((SUBLANES, D_CONV), _F32),
                        pltpu.VMEM((D_SSM // LANES, tm, LANES), _F32),
                        pltpu.VMEM((2, 5, tm, COL_BLOCK), _F32),
                        pltpu.VMEM((D_MODEL, N_IN), _BF16),
                        pltpu.SemaphoreType.DMA((WEIGHT_STAGES,))],
        compiler_params=pltpu.CompilerParams(dimension_semantics=("arbitrary",),
                                             vmem_limit_bytes=VMEM_LIMIT_BYTES),
        name="inproj_conv",
    )(x2d, norm_g, w_in.reshape(D_MODEL // tm, tm, N_IN), conv_w, conv_b, w_glu, w_out)


def _complex_powers(ar, ai, exponent, nbits):
    pr = jnp.ones((exponent.shape[0], ar.shape[1]), _F32)
    pi = jnp.zeros_like(pr)
    cr, ci = ar, ai
    for b in range(nbits):
        bit = ((exponent >> b) & 1) == 1
        fr = jnp.where(bit, cr, 1.0)
        fi = jnp.where(bit, ci, 0.0)
        pr, pi = pr * fr - pi * fi, pr * fi + pi * fr
        cr, ci = cr * cr - ci * ci, 2.0 * cr * ci
    return pr, pi


def _re_im_sign():
    lane = lax.broadcasted_iota(jnp.int32, (1, LANES), 1)
    return jnp.where(lane < SSM_STATE, -1.0, 1.0).astype(_F32)


def _s5_prepare_group(g, mats_ref, cre_ref, cim_ref, kf_ref, ws_ref, wo_ref, a1_ref, a2_ref):
    t, th, jrows, h = S5_CHUNK, S5_TH, S5_POW_ROWS, SSM_GROUP
    sgn = _re_im_sign()
    nbits = max(1, (jrows - 1).bit_length())
    j = lax.broadcasted_iota(jnp.int32, (jrows, 1), 0)
    diag = (lax.broadcasted_iota(jnp.int32, (h, th), 0) == lax.broadcasted_iota(jnp.int32, (h, th), 1))
    lr = mats_ref[g, 0:1, 2 * LANES:3 * LANES]
    li = mats_ref[g, 0:1, 3 * LANES:4 * LANES]
    ldt = jnp.broadcast_to(mats_ref[g, 0:1, LANES + h:LANES + h + 1], (1, LANES))
    d_row = jnp.concatenate([mats_ref[g, 0:1, LANES:LANES + h], jnp.zeros((1, th - h), _F32)], axis=1)
    c_re, c_im = cre_ref[g], cim_ref[g]
    bt1 = mats_ref[g, :, 0:LANES]
    bt2 = pltpu.roll(bt1, SSM_STATE, axis=1)
    ct1, ct2 = jnp.concatenate([c_re, c_im], axis=1), jnp.concatenate([c_im, c_re], axis=1)
    dt = jnp.exp(ldt)
    mag = jnp.exp(lr * dt)
    ar, ai = mag * jnp.cos(li * dt), mag * jnp.sin(li * dt)
    nr, ni = ar - 1.0, ai
    den = lr * lr + li * li
    qr = (nr * lr + ni * li) / den
    qi = (ni * lr - nr * li) / den
    b1 = qr * bt1 + (qi * sgn) * bt2
    b2 = (qr * sgn) * bt2 - qi * bt1
    c1 = ct1 * (-sgn)
    c2 = -ct2
    pr, pi = _complex_powers(ar, ai, j, nbits)
    rr, ri = _complex_powers(ar, ai, jnp.maximum(t - 1 - j, 0), nbits)
    cps = (pr[:, None, :] * c1[None] + pi[:, None, :] * c2[None]).reshape(jrows * h, LANES)
    cps = cps.astype(_BF16)
    b_hi = b1.astype(_BF16)
    b_lo = (b1 - b_hi.astype(_F32)).astype(_BF16)
    kf_ref[g] = _dot_nt(b_hi, cps[:th]) + _dot_nt(b_lo, cps[:th]) + jnp.where(diag, d_row, 0.0)
    wo_ref[g] = cps[h:h + th]
    ws_ref[g] = (rr[:t, None, :] * b1[None] + ri[:t, None, :] * b2[None]).reshape(th, LANES).astype(_BF16)
    a1_ref[g] = pr[t:t + 1]
    a2_ref[g] = pi[t:t + 1]


def _s5_kernel(ug_ref, mats_ref, cre_ref, cim_ref, yg_ref,
               kf_ref, ws_ref, wo_ref, a1_ref, a2_ref,
               mt_ref, xg_ref, yi_ref, sl_ref, slsw_ref, sin_ref, *, chunks_per_seq):
    nb = ug_ref.shape[0]
    t, th, h = S5_CHUNK, S5_TH, SSM_GROUP
    t_hi_n = t // SUBLANES
    n_chunks = ug_ref.shape[1] // t_hi_n
    n_seq = n_chunks // chunks_per_seq
    pitch = S5_SCAN_PITCH
    assert n_seq * nb <= pitch

    for g in range(nb):
        _s5_prepare_group(g, mats_ref, cre_ref, cim_ref, kf_ref, ws_ref, wo_ref, a1_ref, a2_ref)
        xg = jnp.concatenate([ug_ref[g, pl.ds(k, n_chunks, stride=t_hi_n), :]
                              for k in range(t_hi_n)], axis=1).astype(_BF16)
        xg_ref[g] = xg
        s = _dot(xg, ws_ref[g])
        s_sw = pltpu.roll(s, SSM_STATE, axis=1)
        for b in range(n_seq):
            rows = slice(b * chunks_per_seq, (b + 1) * chunks_per_seq)
            sl_ref[pl.ds(b * nb + g, chunks_per_seq, stride=pitch), :] = s[rows]
            slsw_ref[pl.ds(b * nb + g, chunks_per_seq, stride=pitch), :] = s_sw[rows]

    a1 = jnp.concatenate([a1_ref[g] for g in range(nb)] * n_seq, axis=0)
    a2 = jnp.concatenate([a2_ref[g] for g in range(nb)] * n_seq, axis=0) * _re_im_sign()
    n_rows = n_seq * nb

    def times(a, z):
        return a[0] * z[0] + a[1] * z[1], a[0] * z[1] - a[1] * z[0]

    def plus(z, w):
        return z[0] + w[0], z[1] + w[1]

    def local(c):
        return (sl_ref[c * pitch:c * pitch + n_rows, :], slsw_ref[c * pitch:c * pitch + n_rows, :])

    radix = S5_SCAN_RADIX
    powers = [(a1, a2)]
    for _ in range(radix - 1):
        pr, pi_signed = powers[-1]
        sgn = _re_im_sign()
        ar, ai, qr, qi = a1, a2 * sgn, pr, pi_signed * sgn
        powers.append((qr * ar - qi * ai, (qr * ai + qi * ar) * sgn))
    s = (jnp.zeros((n_rows, LANES), _F32), jnp.zeros((n_rows, LANES), _F32))
    for c0 in range(0, chunks_per_seq, radix):
        partial = None
        for j in range(radix):
            if j == 0:
                entering = s[0]
            else:
                entering = powers[j - 1][0] * s[0] + powers[j - 1][1] * s[1] + partial[0]
            sin_ref[(c0 + j) * pitch:(c0 + j) * pitch + n_rows, :] = entering
            partial = local(c0 + j) if partial is None else plus(times(powers[0], partial), local(c0 + j))
        s = plus(times(powers[radix - 1], s), partial)

    for g in range(nb):
        zk = jnp.concatenate([jnp.zeros((h, th), _F32), kf_ref[g]], axis=1)
        for k in range(t):
            mt_ref[g, k * h:(k + 1) * h, :] = zk[:, th - k * h:2 * th - k * h].astype(_BF16)
        yi_ref[g] = _dot(xg_ref[g], mt_ref[g])

    for g in range(nb):
        s_in = jnp.concatenate([sin_ref[pl.ds(b * nb + g, chunks_per_seq, stride=pitch), :]
                                for b in range(n_seq)], axis=0).astype(_BF16)
        y = yi_ref[g] + _dot_nt(s_in, wo_ref[g])
        for k in range(t_hi_n):
            yg_ref[g, pl.ds(k, n_chunks, stride=t_hi_n), :] = y[:, k * LANES:(k + 1) * LANES]


def _s5_core(ug, per_group, chunks_per_seq):
    _, n_rows, _ = ug.shape
    h, th, nb = SSM_GROUP, S5_TH, GROUPS_PER_BLOCK
    group_rows = pl.BlockSpec((nb, n_rows, LANES), lambda i: (i, 0, 0))
    return pl.pallas_call(
        functools.partial(_s5_kernel, chunks_per_seq=chunks_per_seq),
        grid=(SSM_GROUPS // nb,),
        in_specs=[group_rows]
        + [pl.BlockSpec((nb,) + p.shape[1:], lambda i: (i, 0, 0)) for p in per_group],
        out_specs=group_rows,
        out_shape=jax.ShapeDtypeStruct(ug.shape, _F32),
        scratch_shapes=[pltpu.VMEM((nb, h, th), _F32),
                        pltpu.VMEM((nb, th, LANES), _BF16),
                        pltpu.VMEM((nb, th, LANES), _BF16),
                        pltpu.VMEM((nb, 1, LANES), _F32),
                        pltpu.VMEM((nb, 1, LANES), _F32),
                        pltpu.VMEM((nb, th, th), _BF16),
                        pltpu.VMEM((nb, n_rows * SUBLANES // S5_CHUNK, th), _BF16),
                        pltpu.VMEM((nb, n_rows * SUBLANES // S5_CHUNK, th), _F32)]
        + [pltpu.VMEM((chunks_per_seq * S5_SCAN_PITCH, LANES), _F32)] * 3,
        compiler_params=pltpu.CompilerParams(dimension_semantics=("arbitrary",),
                                             vmem_limit_bytes=VMEM_LIMIT_BYTES),
        name="s5_core",
    )(ug, *per_group)


def _out_kernel(yc_ref, yg_ref, zs_ref, x_ref, wg_ref, bg_ref, wo_ref, gp_ref, o_ref, slab_ref, ys_ref):
    tm = x_ref.shape[0]
    nb, sub = GROUPS_PER_BLOCK, SUBLANES
    rp = tm // OUT_ROW_PARTS
    slabs_per_block = COL_BLOCK // LANES

    def activation(part):
        rows = slice(part * rp, (part + 1) * rp)
        grows = slice(part * rp // sub, (part + 1) * rp // sub)
        acc = {"pre": bg_ref[...], "y": []}

        def block(blk):
            gvs = range(blk * slabs_per_block, (blk + 1) * slabs_per_block)
            for gv in gvs:
                tok = _lane_block_transpose([yg_ref[gv * nb + gl, grows, :] for gl in range(nb)])
                for tl in range(sub):
                    slab_ref[gv, pl.ds(part * rp + tl, rp // sub, stride=sub), :] = tok[tl]
            y = jax.nn.gelu(jnp.concatenate([slab_ref[gv, rows, :] for gv in gvs], axis=1))
            acc["y"].append(y)
            acc["pre"] = acc["pre"] + _dot(y.astype(_BF16), wg_ref[blk * COL_BLOCK:(blk + 1) * COL_BLOCK, :])

        def gate():
            y = jnp.concatenate(acc["y"], axis=1)
            ys_ref[part] = (y * _sigmoid(acc["pre"]) * zs_ref[rows, :].astype(_F32)).astype(_BF16)

        return [functools.partial(block, blk) for blk in range(D_SSM // COL_BLOCK)] + [gate]

    def conv_projection(part):
        rows = slice(part * rp, (part + 1) * rp)

        def cols(c):
            sl = slice(c * OUT_COL_BLOCK, (c + 1) * OUT_COL_BLOCK)
            o_ref[rows, sl] = _dot(yc_ref[rows, :], wo_ref[:D_CONV, sl])

        return [functools.partial(cols, c) for c in range(D_MODEL // OUT_COL_BLOCK)]

    def ssm_projection(part):
        rows = slice(part * rp, (part + 1) * rp)

        def cols(c):
            sl = slice(c * OUT_COL_BLOCK, (c + 1) * OUT_COL_BLOCK)
            o_ref[rows, sl] += _dot(ys_ref[part], wo_ref[D_CONV:, sl])

        return [functools.partial(cols, c) for c in range(D_MODEL // OUT_COL_BLOCK)]

    def norm_residual(part):
        rows = slice(part * rp, (part + 1) * rp)
        o = o_ref[rows, :]
        ms = jnp.mean(o * o, axis=-1, keepdims=True)
        o_ref[rows, :] = x_ref[rows, :] + o * lax.rsqrt(ms + EPS) * gp_ref[...]

    for part in range(OUT_ROW_PARTS + 1):
        mxu, vpu = [], []
        if part > 0:
            mxu += ssm_projection(part - 1)
        if part > 1:
            vpu.append(functools.partial(norm_residual, part - 2))
        if part < OUT_ROW_PARTS:
            mxu += conv_projection(part)
            vpu += activation(part)
        _interleave(mxu, vpu)
    norm_residual(OUT_ROW_PARTS - 1)


def _out_proj(yc, yg, zs, x2d, w_glu_bf16, b_glu, w_out_bf16, norm_g):
    n_tok = x2d.shape[0]
    tm = TOKEN_TILE
    const = lambda i: (0, 0)
    tile = lambda w: pl.BlockSpec((tm, w), lambda i: (i, 0))
    return pl.pallas_call(
        _out_kernel,
        grid=(n_tok // tm,),
        in_specs=[
            tile(D_CONV),
            pl.BlockSpec((SSM_GROUPS, tm // SUBLANES, LANES), lambda i: (0, i, 0)),
            tile(D_SSM), tile(D_MODEL),
            pl.BlockSpec((D_SSM, D_SSM), const, pipeline_mode=pl.Buffered(1)),
            pl.BlockSpec((1, D_SSM), const),
            pl.BlockSpec((D_CONV + D_SSM, D_MODEL), const, pipeline_mode=pl.Buffered(1)),
            pl.BlockSpec((1, D_MODEL), const),
        ],
        out_specs=tile(D_MODEL),
        out_shape=jax.ShapeDtypeStruct((n_tok, D_MODEL), _F32),
        scratch_shapes=[pltpu.VMEM((D_SSM // LANES, tm, LANES), _F32),
                        pltpu.VMEM((OUT_ROW_PARTS, tm // OUT_ROW_PARTS, D_SSM), _BF16)],
        compiler_params=pltpu.CompilerParams(dimension_semantics=("arbitrary",),
                                             vmem_limit_bytes=VMEM_LIMIT_BYTES),
        name="glu_outproj",
    )(yc, yg, zs, x2d, w_glu_bf16, b_glu, w_out_bf16, norm_g)


def kernel(x, norm_pre_g, w_in, conv_w, conv_b, ssm_a_re, ssm_a_im, ssm_log_dt, ssm_b_re, ssm_b_im,
           ssm_c_re, ssm_c_im, ssm_d, w_glu, b_glu, w_out, norm_post_g):
    bsz, seq_len, _ = x.shape
    n_tok = bsz * seq_len
    g, h, t = SSM_GROUPS, SSM_GROUP, S5_CHUNK
    assert seq_len % TOKEN_TILE == 0 and TOKEN_TILE % t == 0
    x2d = x.reshape(n_tok, D_MODEL)

    yc, ug, zs, w_glu_bf16, w_out_bf16 = _inproj(
        x2d, norm_pre_g.reshape(1, -1), w_in, conv_w, conv_b.reshape(1, -1), w_glu, w_out, seq_len)

    tail = jnp.concatenate([ssm_d.reshape(g, h), ssm_log_dt.reshape(g, 1),
                            jnp.zeros((g, LANES - h - 1), _F32)], axis=-1)
    rows = jnp.concatenate([tail, ssm_a_re, ssm_a_re, ssm_a_im, ssm_a_im], axis=-1)
    mats = jnp.concatenate([ssm_b_re.transpose(0, 2, 1), ssm_b_im.transpose(0, 2, 1),
                            jnp.broadcast_to(rows[:, None, :], (g, h, 3 * LANES))], axis=-1)
    yg = _s5_core(ug, (mats, ssm_c_re, ssm_c_im), seq_len // t)

    out = _out_proj(yc, yg, zs, x2d, w_glu_bf16, b_glu.reshape(1, -1), w_out_bf16,
                    norm_post_g.reshape(1, -1))
    return out.reshape(bsz, seq_len, D_MODEL)
```
